```python
import jax, jax.numpy as jnp
from jax import lax
import numpy as np

D_MODEL = 1024
BATCH = 8
SEQ = 4096
DEPTH = 4

FOX_HEADS = 8
FOX_HEAD_DIM = 64
FOX_WIDTH = FOX_HEADS * FOX_HEAD_DIM
QBLK = 128
HG_HEADS = 4
HG_DK = 64
HG_DV = 64
HG_KWIDTH = HG_HEADS * HG_DK
HG_WIDTH = HG_HEADS * HG_DV
HG_CHUNK = 64
POOL_WINDOWS = (2, 4, 8, 16)
POOL_GROUPS = 4
POOL_GROUP_DIM = 64
POOL_WIDTH = POOL_GROUPS * POOL_GROUP_DIM
MIX_WIDTH = FOX_WIDTH + HG_WIDTH + POOL_WIDTH
IN_COLS = 3 * FOX_WIDTH + FOX_HEADS + 2 * HG_KWIDTH + 2 * HG_WIDTH + POOL_WIDTH
N_EXPERTS = 32
TOP_K = 4
D_FF = 1024
SWIGLU_LIMIT = 7.0
SWIGLU_ALPHA = 1.702
MOE_BLOCK = 256
RMS_EPS = 1e-5

kernel_name = 'hybrid_fox_hgrn2_pool_moe_trunk'


def rmsnorm(x, g):
    xf = x.astype(jnp.float32)
    y = xf * lax.rsqrt(jnp.mean(xf * xf, axis=-1, keepdims=True) + RMS_EPS)
    return (y * g.astype(jnp.float32)).astype(x.dtype)


def split_projection(proj):
    sizes = (FOX_WIDTH, FOX_WIDTH, FOX_WIDTH, FOX_HEADS,
             HG_KWIDTH, HG_KWIDTH, HG_WIDTH, HG_WIDTH, POOL_WIDTH)
    points, acc = [], 0
    for s in sizes[:-1]:
        acc += s
        points.append(acc)
    return jnp.split(proj, points, axis=-1)


def fox_attention(q, k, v, logf):
    B, H, S, Dh = q.shape
    c = jnp.cumsum(logf, axis=-1)
    scale = 1.0 / math.sqrt(Dh) if False else Dh ** -0.5
    kpos = jnp.arange(S)
    nq = S // QBLK

    def block(i):
        start = i * QBLK
        qb = lax.dynamic_slice_in_dim(q, start, QBLK, axis=2)
        cb = lax.dynamic_slice_in_dim(c, start, QBLK, axis=2)
        s = jnp.einsum('bhqd,bhkd->bhqk', qb, k).astype(jnp.float32) * scale
        s = s + cb[..., None] - c[:, :, None, :]
        qpos = start + jnp.arange(QBLK)
        s = jnp.where(kpos[None, :] <= qpos[:, None], s, -jnp.inf)
        p = jax.nn.softmax(s, axis=-1)
        return jnp.einsum('bhqk,bhkd->bhqd', p.astype(v.dtype), v)

    o = lax.map(block, jnp.arange(nq))
    return o.transpose(1, 2, 0, 3, 4).reshape(B, H, S, Dh)


def hgrn2_chunked(q, f_logit, v, lb):
    B, S, H, Dk = q.shape
    Dv = v.shape[-1]
    C = HG_CHUNK
    nC = S // C
    fl = f_logit.astype(jnp.float32)
    lbf = lb.astype(jnp.float32)
    logf = jnp.logaddexp(jnp.log(lbf), jnp.log1p(-lbf) + jax.nn.log_sigmoid(fl))
    kk = (1.0 - lbf) * jax.nn.sigmoid(-fl)

    def chunks(t):
        return t.reshape(B, nC, C, H, t.shape[-1]).transpose(1, 0, 3, 2, 4)

    a = jnp.cumsum(chunks(logf), axis=3)
    causal = jnp.tril(jnp.ones((C, C), dtype=bool))

    def step(state, xs):
        qc, kc, vc, ac = xs
        o_inter = jnp.einsum('bhtk,bhkv->bhtv', qc * jnp.exp(ac), state)
        diff = ac[:, :, :, None, :] - ac[:, :, None, :, :]
        decay = jnp.exp(jnp.where(causal[:, :, None], diff, -jnp.inf))
        scores = jnp.einsum('bhtk,bhsk,bhtsk->bhts', qc, kc, decay)
        o = o_inter + jnp.einsum('bhts,bhsv->bhtv', scores, vc)
        a_last = ac[:, :, -1]
        new_state = jnp.exp(a_last)[..., None] * state + jnp.einsum(
            'bhsk,bhsv->bhkv', kc * jnp.exp(a_last[:, :, None] - ac), vc)
        return new_state, o

    init = jnp.zeros((B, H, Dk, Dv), jnp.float32)
    _, o = lax.scan(step, init, (chunks(q.astype(jnp.float32)), chunks(kk),
                                 chunks(v.astype(jnp.float32)), a))
    return o.transpose(1, 0, 3, 2, 4).reshape(B, S, H, Dv)


def pool_mixer(u, pool_w, pool_scale):
    B, S, _ = u.shape
    uf = u.astype(jnp.float32).reshape(B, S, POOL_GROUPS, POOL_GROUP_DIM)
    cs = jnp.cumsum(uf, axis=1)
    pos = jnp.arange(1, S + 1, dtype=jnp.float32)
    outs = []
    for g, w in enumerate(POOL_WINDOWS):
        c = cs[:, :, g]
        prev = jnp.pad(c[:, :-w], ((0, 0), (w, 0), (0, 0)))
        mean = (c - prev) / jnp.minimum(pos, float(w))[None, :, None]
        outs.append(mean - uf[:, :, g])
    pooled = jnp.stack(outs, axis=2)
    y = jnp.einsum('bsgc,gcd->bsgd', pooled, pool_w.astype(jnp.float32))
    y = y.reshape(B, S, POOL_WIDTH) * pool_scale.astype(jnp.float32)
    return y.astype(u.dtype)


def mixer(h, w_in, b_f, fox_gain, lb, hg_gain, pool_w, pool_scale, w_out):
    B, S, _ = h.shape
    proj = h @ w_in
    fq, fk, fv, ff, gq, gf, gi, gg, pu = split_projection(proj)

    def heads(t):
        return t.reshape(B, S, FOX_HEADS, FOX_HEAD_DIM).transpose(0, 2, 1, 3)
    logf = jax.nn.log_sigmoid((ff + b_f).astype(jnp.float32)).transpose(0, 2, 1)
    o_fox = fox_attention(heads(fq), heads(fk), heads(fv), logf)
    o_fox = rmsnorm(o_fox, fox_gain.reshape(FOX_HEADS, 1, FOX_HEAD_DIM))
    o_fox = o_fox.transpose(0, 2, 1, 3).reshape(B, S, FOX_WIDTH)

    o_hg = hgrn2_chunked(gq.reshape(B, S, HG_HEADS, HG_DK), gf.reshape(B, S, HG_HEADS, HG_DK),
                         gi.reshape(B, S, HG_HEADS, HG_DV), lb.reshape(HG_HEADS, HG_DK))
    o_hg = o_hg.astype(h.dtype)
    o_hg = rmsnorm(o_hg, hg_gain.reshape(HG_HEADS, HG_DV)) * jax.nn.silu(gg.reshape(B, S, HG_HEADS, HG_DV))
    o_hg = o_hg.reshape(B, S, HG_WIDTH)

    o_pool = pool_mixer(pu, pool_w, pool_scale)

    return jnp.concatenate([o_fox, o_hg, o_pool], axis=-1) @ w_out


def moe(h, w_router, b_router, w1, b1, w2, b2):
    B, S, D = h.shape
    T = B * S
    M = T * TOP_K
    xt = h.reshape(T, D)
    logits = xt.astype(jnp.float32) @ w_router.astype(jnp.float32) + b_router.astype(jnp.float32)
    top_val, top_idx = lax.top_k(logits, TOP_K)
    gate = jax.nn.softmax(top_val, axis=-1)
    e_flat = top_idx.reshape(M)
    order = jnp.argsort(e_flat, stable=True)
    e_sorted = e_flat[order]
    tok_sorted = order // TOP_K
    gate_sorted = gate.reshape(M)[order]
    counts = jnp.bincount(e_flat, length=N_EXPERTS)
    padded = (counts + MOE_BLOCK - 1) // MOE_BLOCK * MOE_BLOCK
    start = jnp.cumsum(counts) - counts
    pend = jnp.cumsum(padded)
    pstart = pend - padded
    dst = pstart[e_sorted] + (jnp.arange(M) - start[e_sorted])
    R = -(-M // MOE_BLOCK) * MOE_BLOCK + N_EXPERTS * MOE_BLOCK
    NB = R // MOE_BLOCK
    buf = jnp.zeros((R, D), h.dtype).at[dst].set(xt[tok_sorted])
    block_e = jnp.minimum(jnp.searchsorted(pend, jnp.arange(NB) * MOE_BLOCK, side='right'),
                          N_EXPERTS - 1)

    def expert_block(args):
        xb, e = args
        hh = xb @ w1[e] + b1[e]
        glu = jnp.minimum(hh[:, 0::2], SWIGLU_LIMIT)
        lin = jnp.clip(hh[:, 1::2], -SWIGLU_LIMIT, SWIGLU_LIMIT)
        act = glu * jax.nn.sigmoid(SWIGLU_ALPHA * glu) * (lin + 1.0)
        return act @ w2[e] + b2[e]

    yb = lax.map(expert_block, (buf.reshape(NB, MOE_BLOCK, D), block_e))
    y_sorted = yb.reshape(R, D)[dst] * gate_sorted[:, None].astype(h.dtype)
    out = jax.ops.segment_sum(y_sorted, tok_sorted, num_segments=T)
    return out.reshape(B, S, D)


def setup_inputs(seed: int = 0) -> dict:
    key = jax.random.key(seed)
    ks = jax.random.split(key, 18)
    nrm = jax.random.normal
    L, D, E, F = DEPTH, D_MODEL, N_EXPERTS, D_FF
    res_scale = (2 * DEPTH) ** -0.5
    return {
        'x': nrm(ks[0], (BATCH, SEQ, D), jnp.float32),
        'norm_mix': 1.0 + 0.02 * nrm(ks[1], (L, D), jnp.float32),
        'w_in': nrm(ks[2], (L, D, IN_COLS), jnp.float32) * D ** -0.5,
        'b_fox_f': 1.5 + 0.5 * nrm(ks[3], (L, FOX_HEADS), jnp.float32),
        'fox_gain': 1.0 + 0.02 * nrm(ks[4], (L, FOX_WIDTH), jnp.float32),
        'hgrn_lb': 0.1 * nrm(ks[5], (L, HG_KWIDTH), jnp.float32),
        'hgrn_gain': 1.0 + 0.02 * nrm(ks[6], (L, HG_WIDTH), jnp.float32),
        'pool_w': nrm(ks[7], (L, POOL_GROUPS, POOL_GROUP_DIM, POOL_GROUP_DIM), jnp.float32) * POOL_GROUP_DIM ** -0.5,
        'pool_scale': 1.0 + 0.1 * nrm(ks[8], (L, POOL_WIDTH), jnp.float32),
        'w_out': nrm(ks[9], (L, MIX_WIDTH, D), jnp.float32) * MIX_WIDTH ** -0.5 * res_scale,
        'norm_moe': 1.0 + 0.02 * nrm(ks[10], (L, D), jnp.float32),
        'w_router': nrm(ks[11], (L, D, E), jnp.float32) * D ** -0.5,
        'b_router': 0.01 * nrm(ks[12], (L, E), jnp.float32),
        'w1': nrm(ks[13], (L, E, D, 2 * F), jnp.float32) * D ** -0.5,
        'b1': 0.01 * nrm(ks[14], (L, E, 2 * F), jnp.float32),
        'w2': nrm(ks[15], (L, E, F, D), jnp.float32) * F ** -0.5 * res_scale,
        'b2': 0.01 * nrm(ks[16], (L, E, D), jnp.float32),
        'norm_final': 1.0 + 0.02 * nrm(ks[17], (D,), jnp.float32),
    }


def reference(x, norm_mix, w_in, b_fox_f, fox_gain, hgrn_lb, hgrn_gain, pool_w, pool_scale,
              w_out, norm_moe, w_router, b_router, w1, b1, w2, b2, norm_final):
    p = jax.nn.softmax(hgrn_lb.astype(jnp.float32), axis=0)
    cs = jnp.cumsum(p, axis=0)
    lbs = cs - cs[0:1]
    h = x
    for l in range(DEPTH):
        h = h + mixer(rmsnorm(h, norm_mix[l]), w_in[l], b_fox_f[l], fox_gain[l], lbs[l],
                      hgrn_gain[l], pool_w[l], pool_scale[l], w_out[l])
        h = h + moe(rmsnorm(h, norm_moe[l]), w_router[l], b_router[l], w1[l], b1[l], w2[l], b2[l])
    return rmsnorm(h, norm_final)
```

```python
import functools

import jax
import jax.numpy as jnp
from jax import lax
from jax.experimental import pallas as pl
from jax.experimental.pallas import tpu as pltpu

F32 = jnp.float32
BF16 = jnp.bfloat16
I32 = jnp.int32

LANES = 128
VMEM_LIMIT = 56 * 1024 * 1024

FOX_HEADS = 8
FOX_HEAD_DIM = 64
FOX_WIDTH = FOX_HEADS * FOX_HEAD_DIM
HG_HEADS = 4
HG_DK = 64
HG_WIDTH = HG_HEADS * HG_DK
HG_CHUNK = 64
HG_SUB = 16
POOL_WINDOWS = (2, 4, 8, 16)
POOL_GROUP_DIM = 64
POOL_WIDTH = len(POOL_WINDOWS) * POOL_GROUP_DIM
POOL_HALO = 16
N_EXPERTS = 32
TOP_K = 4
SWIGLU_LIMIT = 7.0
SWIGLU_ALPHA = 1.702
RMS_EPS = 1e-5
NEG_BIG = -1e30

QBLK0, KBLK0, VBLK0 = 0, 4, 8
GQ0, GI0, GG0 = 12, 14, 16
PU_BLK256 = 9
A_COLS = 2560
B_COLS = 384


def _cparams(sem, vmem=VMEM_LIMIT):
    return pltpu.CompilerParams(dimension_semantics=sem, vmem_limit_bytes=vmem)


def _split3(x):
    x1 = x.astype(BF16)
    r1 = x - x1.astype(F32)
    x2 = r1.astype(BF16)
    x3 = (r1 - x2.astype(F32)).astype(BF16)
    return x1, x2, x3


def _tri_dot(tri, x):
    x1, x2, x3 = _split3(x)
    d = lambda a: jnp.dot(tri, a, preferred_element_type=F32)
    return d(x1) + d(x2) + d(x3)


def _log_sigmoid(z):
    return jnp.minimum(z, 0.0) - jnp.log1p(jnp.exp(-jnp.abs(z)))


def _inproj_body(x_ref, g_ref, wa_ref, wb_ref, a_ref, b_ref):
    x = x_ref[...]
    ms = jnp.mean(x * x, axis=-1, keepdims=True)
    xn = (x * lax.rsqrt(ms + RMS_EPS) * g_ref[...]).astype(BF16)
    a_ref[...] = jnp.dot(xn, wa_ref[...], preferred_element_type=F32).astype(BF16)
    b_ref[...] = jnp.dot(xn, wb_ref[...], preferred_element_type=F32)


def _inproj(h, g, wa, wb, tm):
    T, D = h.shape
    return pl.pallas_call(
        _inproj_body,
        grid=(T // tm,),
        in_specs=[
            pl.BlockSpec((tm, D), lambda i: (i, 0)),
            pl.BlockSpec((1, D), lambda i: (0, 0)),
            pl.BlockSpec((D, A_COLS), lambda i: (0, 0)),
            pl.BlockSpec((D, B_COLS), lambda i: (0, 0)),
        ],
        out_specs=[
            pl.BlockSpec((tm, A_COLS), lambda i: (i, 0)),
            pl.BlockSpec((tm, B_COLS), lambda i: (i, 0)),
        ],
        out_shape=[jax.ShapeDtypeStruct((T, A_COLS), BF16), jax.ShapeDtypeStruct((T, B_COLS), F32)],
        compiler_params=_cparams(("parallel",)),
        name="inproj",
    )(h, g, wa, wb)


def _foxprep_body(ff_ref, bf_ref, tri_ref, c_ref, carry_ref, *, ts):
    @pl.when(pl.program_id(1) == 0)
    def _():
        carry_ref[...] = jnp.zeros_like(carry_ref)

    logf = _log_sigmoid(ff_ref[0] + bf_ref[...])
    c = _tri_dot(tri_ref[...], logf) + carry_ref[...]
    carry_ref[...] = c[ts - 1:ts, :]
    c_ref[0] = c.T[0:FOX_HEADS, :]


def _foxprep(b3, bf_pad, tri, ts):
    B, S, _ = b3.shape
    return pl.pallas_call(
        functools.partial(_foxprep_body, ts=ts),
        grid=(B, S // ts),
        in_specs=[
            pl.BlockSpec((1, ts, LANES), lambda b, s: (b, s, 2)),
            pl.BlockSpec((1, LANES), lambda b, s: (0, 0)),
            pl.BlockSpec((ts, ts), lambda b, s: (0, 0)),
        ],
        out_specs=pl.BlockSpec((1, FOX_HEADS, ts), lambda b, s: (b, 0, s)),
        out_shape=jax.ShapeDtypeStruct((B, FOX_HEADS, S), F32),
        scratch_shapes=[pltpu.VMEM((1, LANES), F32)],
        compiler_params=_cparams(("parallel", "arbitrary")),
        name="foxprep",
    )(b3, bf_pad, tri)


def _fox_body(q_ref, k_ref, v_ref, c_ref, g_ref, o_ref, *, t):
    hp = pl.program_id(1)
    qi = pl.program_id(2)
    q = q_ref[0]
    lane = lax.broadcasted_iota(I32, (1, LANES), 1)
    lo = lane < FOX_HEAD_DIM
    zero = jnp.zeros_like(q)
    qh = (jnp.where(lo, q, zero), jnp.where(lo, zero, q))
    row = lax.broadcasted_iota(I32, (t, t), 0)
    col = lax.broadcasted_iota(I32, (t, t), 1)
    causal = col <= row
    nk = pl.num_programs(2)
    crow = lambda hh, j: c_ref[0, (2 * hp + hh) * nk + j]
    c0 = tuple(crow(hh, qi)[:, 0:1] for hh in range(2))

    def step(j, carry, masked):
        start = pl.multiple_of(j * t, t)
        kt = k_ref[0, pl.ds(start, t), :]
        vt = v_ref[0, pl.ds(start, t), :]
        out = []
        for hh in range(2):
            m, l, acc = carry[hh]
            s = lax.dot_general(qh[hh], kt, (((1,), (1,)), ((), ())), preferred_element_type=F32)
            s = s + (c0[hh] - crow(hh, j))
            if masked:
                s = jnp.where(causal, s, -jnp.inf)
            m_new = jnp.maximum(m, jnp.max(s, axis=-1, keepdims=True))
            alpha = jnp.exp(m - m_new)
            p = jnp.exp(s - m_new)
            l_new = alpha * l + jnp.sum(p, axis=-1, keepdims=True)
            acc_new = alpha * acc + jnp.dot(p.astype(BF16), vt, preferred_element_type=F32)
            out.append((m_new, l_new, acc_new))
        return tuple(out)

    init1 = (jnp.full((t, 1), -jnp.inf, F32), jnp.zeros((t, 1), F32), jnp.zeros((t, LANES), F32))
    carry = lax.fori_loop(0, qi, lambda j, c: step(j, c, False), (init1, init1))
    (_, l0, a0), (_, l1, a1) = step(qi, carry, True)
    o = jnp.where(lo, a0 / l0, a1 / l1)
    sq = o * o
    ms0 = jnp.sum(jnp.where(lo, sq, 0.0), axis=-1, keepdims=True) * (1.0 / FOX_HEAD_DIM)
    ms1 = jnp.sum(jnp.where(lo, 0.0, sq), axis=-1, keepdims=True) * (1.0 / FOX_HEAD_DIM)
    inv = jnp.where(lo, lax.rsqrt(ms0 + RMS_EPS), lax.rsqrt(ms1 + RMS_EPS))
    o_ref[0] = (o * inv * g_ref[...]).astype(BF16)


def _fox(a3, c4, gain, t):
    B, S, _ = a3.shape
    return pl.pallas_call(
        functools.partial(_fox_body, t=t),
        grid=(B, FOX_HEADS // 2, S // t),
        in_specs=[
            pl.BlockSpec((1, t, LANES), lambda b, hp, qi: (b, qi, QBLK0 + hp)),
            pl.BlockSpec((1, S, LANES), lambda b, hp, qi: (b, 0, KBLK0 + hp)),
            pl.BlockSpec((1, S, LANES), lambda b, hp, qi: (b, 0, VBLK0 + hp)),
            pl.BlockSpec((1, FOX_HEADS * (S // t), 1, t), lambda b, hp, qi: (b, 0, 0, 0)),
            pl.BlockSpec((1, LANES), lambda b, hp, qi: (0, hp)),
        ],
        out_specs=pl.BlockSpec((1, t, LANES), lambda b, hp, qi: (b, qi, hp)),
        out_shape=jax.ShapeDtypeStruct((B, S, FOX_WIDTH), BF16),
        compiler_params=_cparams(("parallel", "parallel", "arbitrary")),
        name="fox_attention",
    )(a3, a3, a3, c4, gain)


def _hgrn_body(q_ref, f_ref, v_ref, gg_ref, lb_ref, gain_ref, t64_ref, t16_ref, o_ref, st_ref, *, tc):
    @pl.when(pl.program_id(2) == 0)
    def _():
        st_ref[...] = jnp.zeros_like(st_ref)

    C, SB = HG_CHUNK, HG_SUB
    lane = lax.broadcasted_iota(I32, (1, LANES), 1)
    lo = lane < HG_DK
    lb = lb_ref[...]
    fl = f_ref[0]
    t2 = jnp.log1p(-lb) + _log_sigmoid(fl)
    t1 = jnp.where(lb > 0.0, jnp.log(jnp.maximum(lb, 1e-37)), NEG_BIG)
    logf = jnp.maximum(t1, t2) + jnp.log1p(jnp.exp(-jnp.abs(t1 - t2)))
    kk = (1.0 - lb) / (1.0 + jnp.exp(fl))
    q = q_ref[0].astype(F32)
    a = _tri_dot(t64_ref[...], logf)
    r = _tri_dot(t16_ref[...], logf)
    e0 = a - r
    qr = q * jnp.exp(r)
    qa = (q * jnp.exp(a)).astype(BF16)

    rowblk = lax.broadcasted_iota(I32, (C, 1), 0) // SB
    r64 = lax.broadcasted_iota(I32, (C, C), 0)
    c64 = lax.broadcasted_iota(I32, (C, C), 1)
    causal = c64 <= r64
    rr = lax.broadcasted_iota(I32, (LANES, LANES), 0)
    cc = lax.broadcasted_iota(I32, (LANES, LANES), 1)
    same_head = (rr < HG_DK) == (cc < HG_DK)
    nt = (((1,), (1,)), ((), ()))
    tn = (((0,), (0,)), ((), ()))

    st = st_ref[...]
    for ch in range(tc // C):
        sl = slice(ch * C, (ch + 1) * C)
        a_c, kk_c, v_c = a[sl], kk[sl], v_ref[0, sl, :]
        a_last = a_c[C - 1:C, :]
        rows0, rows1 = [], []
        for blk in range(C // SB):
            b0 = ch * C + blk * SB
            e_blk = e0[b0:b0 + 1, :]
            live = rowblk <= blk
            k_blk = jnp.where(live, kk_c * jnp.exp(jnp.where(live, e_blk - a_c, 0.0)), 0.0).astype(BF16)
            q_blk = qr[b0:b0 + SB, :]
            lhs = jnp.concatenate([jnp.where(lo, q_blk, 0.0), jnp.where(lo, 0.0, q_blk)], axis=0).astype(BF16)
            sc = lax.dot_general(lhs, k_blk, nt, preferred_element_type=F32)
            rows0.append(sc[0:SB])
            rows1.append(sc[SB:2 * SB])
        s0 = jnp.where(causal, jnp.concatenate(rows0, axis=0), 0.0).astype(BF16)
        s1 = jnp.where(causal, jnp.concatenate(rows1, axis=0), 0.0).astype(BF16)
        intra = jnp.where(lo, jnp.dot(s0, v_c, preferred_element_type=F32),
                          jnp.dot(s1, v_c, preferred_element_type=F32))
        inter = lax.dot_general(qa[sl], st.astype(BF16), nt, preferred_element_type=F32)
        o = inter + intra
        kd = (kk_c * jnp.exp(a_last - a_c)).astype(BF16)
        upd = lax.dot_general(v_c, kd, tn, preferred_element_type=F32)
        st = jnp.where(same_head, st * jnp.exp(a_last) + upd, 0.0)

        sq = o * o
        ms0 = jnp.sum(jnp.where(lo, sq, 0.0), axis=-1, keepdims=True) * (1.0 / HG_DK)
        ms1 = jnp.sum(jnp.where(lo, 0.0, sq), axis=-1, keepdims=True) * (1.0 / HG_DK)
        inv = jnp.where(lo, lax.rsqrt(ms0 + RMS_EPS), lax.rsqrt(ms1 + RMS_EPS))
        g = gg_ref[0, sl, :].astype(F32)
        silu = g / (1.0 + jnp.exp(-g))
        o_ref[0, sl, :] = (o * inv * gain_ref[...] * silu).astype(BF16)
    st_ref[...] = st


def _hgrn(a3, b3, lb, gain, t64, t16, tc):
    B, S, _ = a3.shape
    blk = lambda c0: pl.BlockSpec((1, tc, LANES), lambda b, hp, s: (b, s, c0 + hp))
    row = pl.BlockSpec((1, LANES), lambda b, hp, s: (0, hp))
    tri = pl.BlockSpec((tc, tc), lambda b, hp, s: (0, 0))
    return pl.pallas_call(
        functools.partial(_hgrn_body, tc=tc),
        grid=(B, HG_HEADS // 2, S // tc),
        in_specs=[blk(GQ0), blk(0), blk(GI0), blk(GG0), row, row, tri, tri],
        out_specs=pl.BlockSpec((1, tc, LANES), lambda b, hp, s: (b, s, hp)),
        out_shape=jax.ShapeDtypeStruct((B, S, HG_WIDTH), BF16),
        scratch_shapes=[pltpu.VMEM((LANES, LANES), F32)],
        compiler_params=_cparams(("parallel", "parallel", "arbitrary")),
        name="hgrn2",
    )(a3, b3, a3, a3, lb, gain, t64, t16)


def _pool_body(u_ref, w_ref, sc_ref, o_ref, ext_ref, *, tc):
    si = pl.program_id(1)

    @pl.when(si == 0)
    def _():
        ext_ref[0:POOL_HALO, :] = jnp.zeros((POOL_HALO, POOL_WIDTH), F32)

    u = u_ref[0].astype(F32)
    ext_ref[POOL_HALO:POOL_HALO + tc, :] = u
    acc = u
    sums = {}
    for j in range(1, max(POOL_WINDOWS)):
        acc = acc + ext_ref[POOL_HALO - j:POOL_HALO - j + tc, :]
        if j + 1 in POOL_WINDOWS:
            sums[j + 1] = acc
    lane = lax.broadcasted_iota(I32, (1, POOL_WIDTH), 1)
    grp = lane // POOL_GROUP_DIM
    win = sums[POOL_WINDOWS[-1]]
    wlen = jnp.full((1, POOL_WIDTH), float(POOL_WINDOWS[-1]), F32)
    for gi in range(len(POOL_WINDOWS) - 2, -1, -1):
        win = jnp.where(grp == gi, sums[POOL_WINDOWS[gi]], win)
        wlen = jnp.where(grp == gi, float(POOL_WINDOWS[gi]), wlen)
    pos = (si * tc + lax.broadcasted_iota(I32, (tc, 1), 0) + 1).astype(F32)
    pooled = win / jnp.minimum(pos, wlen) - u
    y = jnp.dot(pooled.astype(BF16), w_ref[...], preferred_element_type=F32) * sc_ref[...]
    o_ref[0] = y.astype(BF16)
    ext_ref[0:POOL_HALO, :] = u[tc - POOL_HALO:tc, :]


def _pool(a3, wbd, scale, tc):
    B, S, _ = a3.shape
    return pl.pallas_call(
        functools.partial(_pool_body, tc=tc),
        grid=(B, S // tc),
        in_specs=[
            pl.BlockSpec((1, tc, POOL_WIDTH), lambda b, s: (b, s, PU_BLK256)),
            pl.BlockSpec((POOL_WIDTH, POOL_WIDTH), lambda b, s: (0, 0)),
            pl.BlockSpec((1, POOL_WIDTH), lambda b, s: (0, 0)),
        ],
        out_specs=pl.BlockSpec((1, tc, POOL_WIDTH), lambda b, s: (b, s, 0)),
        out_shape=jax.ShapeDtypeStruct((B, S, POOL_WIDTH), BF16),
        scratch_shapes=[pltpu.VMEM((POOL_HALO + tc, POOL_WIDTH), F32)],
        compiler_params=_cparams(("parallel", "arbitrary")),
        name="pool_mixer",
    )(a3, wbd, scale)


def _router_body(of_ref, oh_ref, op_ref, h_ref, wo_ref, g_ref, wr_ref, br_ref, tri_ref,
                 hn_ref, ri_ref, rg_ref, cnt_ref, carry_ref, *, tm):
    @pl.when(pl.program_id(0) == 0)
    def _():
        carry_ref[...] = jnp.zeros_like(carry_ref)

    f0, f1 = FOX_WIDTH, FOX_WIDTH + HG_WIDTH
    mix = (jnp.dot(of_ref[...], wo_ref[0:f0, :], preferred_element_type=F32)
           + jnp.dot(oh_ref[...], wo_ref[f0:f1, :], preferred_element_type=F32)
           + jnp.dot(op_ref[...], wo_ref[f1:f1 + POOL_WIDTH, :], preferred_element_type=F32))
    hn = h_ref[...] + mix
    hn_ref[...] = hn
    ms = jnp.mean(hn * hn, axis=-1, keepdims=True)
    xn = hn * lax.rsqrt(ms + RMS_EPS) * g_ref[...]
    x1 = xn.astype(BF16)
    x2 = (xn - x1.astype(F32)).astype(BF16)
    d = lambda a, b: jnp.dot(a, b, preferred_element_type=F32)
    logits = d(x1, wr_ref[0]) + d(x1, wr_ref[1]) + d(x2, wr_ref[0]) + br_ref[...]

    lanef = lax.broadcasted_iota(I32, (tm, LANES), 1).astype(F32)
    l = logits
    vals, idxs, sels = [], [], []
    for _ in range(TOP_K):
        mk = jnp.max(l, axis=-1, keepdims=True)
        ik = jnp.min(jnp.where(l == mk, lanef, float(LANES)), axis=-1, keepdims=True)
        sk = lanef == ik
        vals.append(mk)
        idxs.append(ik)
        sels.append(sk)
        l = jnp.where(sk, -jnp.inf, l)
    es = [jnp.exp(v - vals[0]) for v in vals]
    den = es[0] + es[1] + es[2] + es[3]
    sel = jnp.zeros((tm, LANES), F32)
    for sk in sels:
        sel = jnp.where(sk, 1.0, sel)
    rank = jnp.dot(tri_ref[...], sel.astype(BF16), preferred_element_type=F32) + carry_ref[...]
    carry = carry_ref[...] + jnp.sum(sel, axis=0, keepdims=True)
    carry_ref[...] = carry
    cnt_ref[...] = carry
    ri = jnp.zeros((tm, LANES), F32)
    rg = jnp.zeros((tm, LANES), F32)
    for k in range(TOP_K):
        pos_k = jnp.sum(jnp.where(sels[k], rank, 0.0), axis=-1, keepdims=True)
        ri = jnp.where(lanef == float(k), idxs[k], ri)
        ri = jnp.where(lanef == float(TOP_K + k), pos_k, ri)
        rg = jnp.where(lanef == float(k), es[k] / den, rg)
    ri_ref[...] = ri.astype(I32)
    rg_ref[...] = rg


def _router(o_fox, o_hg, o_pool, h, wo, g, wr, br, tri, tm):
    T, D = h.shape
    rowblk = lambda w: pl.BlockSpec((tm, w), lambda i: (i, 0))
    const = lambda shp: pl.BlockSpec(shp, lambda i: tuple(0 for _ in shp))
    return pl.pallas_call(
        functools.partial(_router_body, tm=tm),
        grid=(T // tm,),
        in_specs=[rowblk(FOX_WIDTH), rowblk(HG_WIDTH), rowblk(POOL_WIDTH), rowblk(D),
                  const((D, D)), const((1, D)), const((2, D, LANES)), const((1, LANES)), const((tm, tm))],
        out_specs=[rowblk(D), rowblk(LANES), rowblk(LANES), const((1, LANES))],
        out_shape=[jax.ShapeDtypeStruct((T, D), F32), jax.ShapeDtypeStruct((T, LANES), I32),
                   jax.ShapeDtypeStruct((T, LANES), F32), jax.ShapeDtypeStruct((1, LANES), F32)],
        scratch_shapes=[pltpu.VMEM((1, LANES), F32)],
        compiler_params=_cparams(("arbitrary",)),
        name="outproj_router",
    )(o_fox, o_hg, o_pool, h, wo, g, wr, br, tri)


def _row_copy(src_ref, src_row, dst_ref, dst_row, sem):
    return pltpu.make_async_copy(src_ref.at[pl.ds(src_row, 1)], dst_ref.at[pl.ds(dst_row, 1)], sem)


def _dispatch_body(dst_ref, h_ref, g_ref, buf_in_ref, buf_ref, xn_ref, sem, *, tt):
    del buf_in_ref
    x = h_ref[...]
    ms = jnp.mean(x * x, axis=-1, keepdims=True)
    xn_ref[...] = x * lax.rsqrt(ms + RMS_EPS) * g_ref[...]

    def issue(r, _):
        for k in range(TOP_K):
            _row_copy(xn_ref, r, buf_ref, dst_ref[0, 0, r * TOP_K + k], sem).start()
        return 0

    def drain(r, _):
        for k in range(TOP_K):
            _row_copy(xn_ref, r, buf_ref, dst_ref[0, 0, r * TOP_K + k], sem).wait()
        return 0

    lax.fori_loop(0, tt, issue, 0)
    lax.fori_loop(0, tt, drain, 0)


def _dispatch(dst3, hn, g, buf0, tt):
    T, D = hn.shape
    return pl.pallas_call(
        functools.partial(_dispatch_body, tt=tt),
        grid=(T // tt,),
        in_specs=[
            pl.BlockSpec((1, 1, tt * TOP_K), lambda i: (i, 0, 0), memory_space=pltpu.SMEM),
            pl.BlockSpec((tt, D), lambda i: (i, 0)),
            pl.BlockSpec((1, D), lambda i: (0, 0)),
            pl.BlockSpec(memory_space=pl.ANY),
        ],
        out_specs=pl.BlockSpec(memory_space=pl.ANY),
        out_shape=jax.ShapeDtypeStruct(buf0.shape, F32),
        scratch_shapes=[pltpu.VMEM((tt, D), F32), pltpu.SemaphoreType.DMA],
        input_output_aliases={3: 0},
        compiler_params=_cparams(("arbitrary",)),
        name="moe_dispatch",
    )(dst3, hn, g, buf0)


def _expert_body(te_ref, nu_ref, x_ref, w1g_ref, w1l_ref, w2_ref, b1g_ref, b1l_ref, b2_ref, y_ref):
    del te_ref

    @pl.when(pl.program_id(0) < nu_ref[0])
    def _():
        x = x_ref[...].astype(BF16)
        glu = jnp.dot(x, w1g_ref[0], preferred_element_type=F32) + b1g_ref[0]
        lin = jnp.dot(x, w1l_ref[0], preferred_element_type=F32) + b1l_ref[0]
        glu = jnp.minimum(glu, SWIGLU_LIMIT)
        lin = jnp.clip(lin, -SWIGLU_LIMIT, SWIGLU_LIMIT)
        act = glu / (1.0 + jnp.exp(-SWIGLU_ALPHA * glu)) * (lin + 1.0)
        y_ref[...] = jnp.dot(act.astype(BF16), w2_ref[0], preferred_element_type=F32) + b2_ref[0]

    @pl.when(pl.program_id(0) >= nu_ref[0])
    def _():
        y_ref[...] = jnp.zeros_like(y_ref)


def _experts(tile_e, n_used, buf, w1g, w1l, w2, b1g, b1l, b2, bm):
    R, D = buf.shape
    F = w1g.shape[-1]
    rows = pl.BlockSpec((bm, D), lambda i, te, nu: (jnp.minimum(i, nu[0] - 1), 0))
    wspec = lambda a, b: pl.BlockSpec((1, a, b), lambda i, te, nu: (te[i], 0, 0))
    grid_spec = pltpu.PrefetchScalarGridSpec(
        num_scalar_prefetch=2,
        grid=(R // bm,),
        in_specs=[rows, wspec(D, F), wspec(D, F), wspec(F, D), wspec(1, F), wspec(1, F), wspec(1, D)],
        out_specs=pl.BlockSpec((bm, D), lambda i, te, nu: (i, 0)),
    )
    return pl.pallas_call(
        _expert_body,
        grid_spec=grid_spec,
        out_shape=jax.ShapeDtypeStruct((R, D), F32),
        compiler_params=_cparams(("arbitrary",)),
        name="moe_experts",
    )(tile_e, n_used, buf, w1g, w1l, w2, b1g, b1l, b2)


def _combine_body(dst_ref, h_ref, rg_ref, gf_ref, y_ref, o_ref, rows_ref, sem, *, tt, final_norm):
    def issue(r, _):
        for k in range(TOP_K):
            _row_copy(y_ref, dst_ref[0, 0, r * TOP_K + k], rows_ref.at[k], r, sem).start()
        return 0

    def drain(r, _):
        for k in range(TOP_K):
            _row_copy(y_ref, dst_ref[0, 0, r * TOP_K + k], rows_ref.at[k], r, sem).wait()
        return 0

    lax.fori_loop(0, tt, issue, 0)
    lax.fori_loop(0, tt, drain, 0)
    out = h_ref[...]
    gates = rg_ref[...]
    for k in range(TOP_K):
        out = out + gates[:, k:k + 1] * rows_ref[k]
    if final_norm:
        ms = jnp.mean(out * out, axis=-1, keepdims=True)
        out = out * lax.rsqrt(ms + RMS_EPS) * gf_ref[...]
    o_ref[...] = out


def _combine(dst3, hn, rg, g_final, ybuf, tt, final_norm):
    T, D = hn.shape
    return pl.pallas_call(
        functools.partial(_combine_body, tt=tt, final_norm=final_norm),
        grid=(T // tt,),
        in_specs=[
            pl.BlockSpec((1, 1, tt * TOP_K), lambda i: (i, 0, 0), memory_space=pltpu.SMEM),
            pl.BlockSpec((tt, D), lambda i: (i, 0)),
            pl.BlockSpec((tt, LANES), lambda i: (i, 0)),
            pl.BlockSpec((1, D), lambda i: (0, 0)),
            pl.BlockSpec(memory_space=pl.ANY),
        ],
        out_specs=pl.BlockSpec((tt, D), lambda i: (i, 0)),
        out_shape=jax.ShapeDtypeStruct((T, D), F32),
        scratch_shapes=[pltpu.VMEM((TOP_K, tt, D), F32), pltpu.SemaphoreType.DMA],
        compiler_params=_cparams(("arbitrary",)),
        name="moe_combine",
    )(dst3, hn, rg, g_final, ybuf)


def _tiles(B, S):
    T = B * S
    ts = min(512, S)
    tm = min(512, T)
    tt = min(256, T)
    bm = min(512, T)
    return ts, tm, tt, bm


def _block_tri(n, blk, strict=False):
    r = jnp.arange(n)[:, None]
    c = jnp.arange(n)[None, :]
    keep = (c < r) if strict else (c <= r)
    return (keep & (r // blk == c // blk)).astype(BF16)


def kernel(x, norm_mix, w_in, b_fox_f, fox_gain, hgrn_lb, hgrn_gain, pool_w, pool_scale, w_out,
           norm_moe, w_router, b_router, w1, b1, w2, b2, norm_final):
    B, S, D = x.shape
    L = w_in.shape[0]
    T = B * S
    E = w1.shape[1]
    ts, tm, tt, bm = _tiles(B, S)

    offs = [0]
    for wdt in (FOX_WIDTH, FOX_WIDTH, FOX_WIDTH, FOX_HEADS, HG_WIDTH, HG_WIDTH, HG_WIDTH, HG_WIDTH, POOL_WIDTH):
        offs.append(offs[-1] + wdt)
    col = lambda i: w_in[:, :, offs[i]:offs[i + 1]]
    scale = FOX_HEAD_DIM ** -0.5
    wa = jnp.concatenate([col(0) * scale, col(1), col(2), col(4), col(6), col(7), col(8)], axis=-1).astype(BF16)
    wb = jnp.concatenate([col(5), col(3), jnp.zeros((L, D, LANES - FOX_HEADS), F32)], axis=-1).astype(BF16)
    bf_pad = jnp.pad(b_fox_f, ((0, 0), (0, LANES - FOX_HEADS)))
    p = jax.nn.softmax(hgrn_lb.astype(F32), axis=0)
    cs = jnp.cumsum(p, axis=0)
    lbs = cs - cs[0:1]
    eye = jnp.eye(len(POOL_WINDOWS), dtype=F32)
    pool_bd = jnp.einsum("lgcd,gh->lgchd", pool_w, eye).reshape(L, POOL_WIDTH, POOL_WIDTH).astype(BF16)
    wo = w_out.astype(BF16)
    wr_pad = jnp.pad(w_router, ((0, 0), (0, 0), (0, LANES - E)))
    wr_hi = wr_pad.astype(BF16)
    wr_lo = (wr_pad - wr_hi.astype(F32)).astype(BF16)
    wr = jnp.stack([wr_hi, wr_lo], axis=1)
    br_pad = jnp.pad(b_router, ((0, 0), (0, LANES - E)), constant_values=NEG_BIG)
    w1g = w1[..., 0::2].astype(BF16)
    w1l = w1[..., 1::2].astype(BF16)
    w2b = w2.astype(BF16)
    b1g = b1[..., 0::2][:, :, None, :]
    b1l = b1[..., 1::2][:, :, None, :]
    b2r = b2[:, :, None, :]

    tri_seq = _block_tri(ts, ts)
    tri64 = _block_tri(ts, HG_CHUNK)
    tri16 = _block_tri(ts, HG_SUB)
    tri_tok = _block_tri(tm, tm, strict=True)

    M = T * TOP_K
    R = -(-M // bm) * bm + E * bm
    NB = R // bm

    h = x.reshape(T, D)
    for l in range(L):
        a2, b2d = _inproj(h, norm_mix[l][None], wa[l], wb[l], tm)
        a3 = a2.reshape(B, S, A_COLS)
        b3 = b2d.reshape(B, S, B_COLS)
        c = _foxprep(b3, bf_pad[l][None], tri_seq, ts)
        c4 = c.reshape(B, FOX_HEADS * (S // ts), 1, ts)
        o_fox = _fox(a3, c4, fox_gain[l][None], ts)
        o_hg = _hgrn(a3, b3, lbs[l][None], hgrn_gain[l][None], tri64, tri16, ts)
        o_pool = _pool(a3, pool_bd[l], pool_scale[l][None], ts)
        hn, ri, rg, cnt = _router(o_fox.reshape(T, FOX_WIDTH), o_hg.reshape(T, HG_WIDTH),
                                  o_pool.reshape(T, POOL_WIDTH), h, wo[l], norm_moe[l][None],
                                  wr[l], br_pad[l][None], tri_tok, tm)
        counts = cnt[0, :E].astype(I32)
        padded = (counts + bm - 1) // bm * bm
        pend = jnp.cumsum(padded)
        pstart = pend - padded
        dst = pstart[ri[:, 0:TOP_K]] + ri[:, TOP_K:2 * TOP_K]
        n_used = (pend[-1] // bm).astype(I32)
        tile_ids = jnp.minimum(jnp.arange(NB, dtype=I32), n_used - 1)
        tile_e = jnp.minimum(jnp.searchsorted(pend, tile_ids * bm, side="right"), E - 1).astype(I32)
        dst3 = dst.reshape(T // tt, 1, tt * TOP_K)
        buf = _dispatch(dst3, hn, norm_moe[l][None], jnp.zeros((R, D), F32), tt)
        ybuf = _experts(tile_e, n_used.reshape(1), buf, w1g[l], w1l[l], w2b[l], b1g[l], b1l[l], b2r[l], bm)
        h = _combine(dst3, hn, rg, norm_final[None], ybuf, tt, final_norm=(l == L - 1))
    return h.reshape(B, S, D)
```

```python
import functools

import jax
import jax.numpy as jnp
from jax import lax
from jax.experimental import pallas as pl
from jax.experimental.pallas import tpu as pltpu

F32 = jnp.float32
BF16 = jnp.bfloat16
I32 = jnp.int32

LANES = 128
VMEM_LIMIT = 56 * 1024 * 1024

FOX_HEADS = 8
FOX_HEAD_DIM = 64
FOX_WIDTH = FOX_HEADS * FOX_HEAD_DIM
HG_HEADS = 4
HG_DK = 64
HG_WIDTH = HG_HEADS * HG_DK
HG_CHUNK = 64
HG_SUB = 16
POOL_WINDOWS = (2, 4, 8, 16)
POOL_GROUP_DIM = 64
POOL_WIDTH = len(POOL_WINDOWS) * POOL_GROUP_DIM
POOL_HALO = 16
N_EXPERTS = 32
TOP_K = 4
SWIGLU_LIMIT = 7.0
SWIGLU_ALPHA = 1.702
RMS_EPS = 1e-5
NEG_BIG = -1e30
DMA_UNROLL = 8

QBLK0, KBLK0, VBLK0 = 0, 4, 8
GQ0, GI0, GG0 = 12, 14, 16
PU_BLK256 = 9
A_COLS = 2560
B_COLS = 384


def _cparams(sem, vmem=VMEM_LIMIT):
    return pltpu.CompilerParams(dimension_semantics=sem, vmem_limit_bytes=vmem)


def _split3(x):
    x1 = x.astype(BF16)
    r1 = x - x1.astype(F32)
    x2 = r1.astype(BF16)
    x3 = (r1 - x2.astype(F32)).astype(BF16)
    return x1, x2, x3


def _tri_dot(tri, x):
    x1, x2, x3 = _split3(x)
    d = lambda a: jnp.dot(tri, a, preferred_element_type=F32)
    return d(x1) + d(x2) + d(x3)


def _log_sigmoid(z):
    return jnp.minimum(z, 0.0) - jnp.log1p(jnp.exp(-jnp.abs(z)))


def _inproj_body(x_ref, g_ref, wa_ref, wb_ref, a_ref, b_ref):
    x = x_ref[...]
    ms = jnp.mean(x * x, axis=-1, keepdims=True)
    xn = (x * lax.rsqrt(ms + RMS_EPS) * g_ref[...]).astype(BF16)
    a_ref[...] = jnp.dot(xn, wa_ref[...], preferred_element_type=F32).astype(BF16)
    b_ref[...] = jnp.dot(xn, wb_ref[...], preferred_element_type=F32)


def _inproj(h, g, wa, wb, tm):
    T, D = h.shape
    return pl.pallas_call(
        _inproj_body,
        grid=(T // tm,),
        in_specs=[
            pl.BlockSpec((tm, D), lambda i: (i, 0)),
            pl.BlockSpec((1, D), lambda i: (0, 0)),
            pl.BlockSpec((D, A_COLS), lambda i: (0, 0)),
            pl.BlockSpec((D, B_COLS), lambda i: (0, 0)),
        ],
        out_specs=[
            pl.BlockSpec((tm, A_COLS), lambda i: (i, 0)),
            pl.BlockSpec((tm, B_COLS), lambda i: (i, 0)),
        ],
        out_shape=[jax.ShapeDtypeStruct((T, A_COLS), BF16), jax.ShapeDtypeStruct((T, B_COLS), F32)],
        compiler_params=_cparams(("parallel",)),
        name="inproj",
    )(h, g, wa, wb)


def _foxprep_body(ff_ref, bf_ref, tri_ref, c_ref, carry_ref, *, ts):
    @pl.when(pl.program_id(1) == 0)
    def _():
        carry_ref[...] = jnp.zeros_like(carry_ref)

    logf = _log_sigmoid(ff_ref[0] + bf_ref[...])
    c = _tri_dot(tri_ref[...], logf) + carry_ref[...]
    carry_ref[...] = c[ts - 1:ts, :]
    c_ref[0] = c.T[0:FOX_HEADS, :]


def _foxprep(b3, bf_pad, tri, ts):
    B, S, _ = b3.shape
    return pl.pallas_call(
        functools.partial(_foxprep_body, ts=ts),
        grid=(B, S // ts),
        in_specs=[
            pl.BlockSpec((1, ts, LANES), lambda b, s: (b, s, 2)),
            pl.BlockSpec((1, LANES), lambda b, s: (0, 0)),
            pl.BlockSpec((ts, ts), lambda b, s: (0, 0)),
        ],
        out_specs=pl.BlockSpec((1, FOX_HEADS, ts), lambda b, s: (b, 0, s)),
        out_shape=jax.ShapeDtypeStruct((B, FOX_HEADS, S), F32),
        scratch_shapes=[pltpu.VMEM((1, LANES), F32)],
        compiler_params=_cparams(("parallel", "arbitrary")),
        name="foxprep",
    )(b3, bf_pad, tri)


def _fox_body(q_ref, k_ref, v_ref, c_ref, g_ref, o_ref, *, t):
    hp = pl.program_id(1)
    qi = pl.program_id(2)
    q = q_ref[0]
    lane = lax.broadcasted_iota(I32, (1, LANES), 1)
    lo = lane < FOX_HEAD_DIM
    zero = jnp.zeros_like(q)
    qh = (jnp.where(lo, q, zero), jnp.where(lo, zero, q))
    row = lax.broadcasted_iota(I32, (t, t), 0)
    col = lax.broadcasted_iota(I32, (t, t), 1)
    causal = col <= row
    nk = pl.num_programs(2)
    crow = lambda hh, j: c_ref[0, (2 * hp + hh) * nk + j]
    c0 = tuple(crow(hh, qi)[:, 0:1] for hh in range(2))

    def step(j, carry, masked):
        start = pl.multiple_of(j * t, t)
        kt = k_ref[0, pl.ds(start, t), :]
        vt = v_ref[0, pl.ds(start, t), :]
        out = []
        for hh in range(2):
            m, l, acc = carry[hh]
            s = lax.dot_general(qh[hh], kt, (((1,), (1,)), ((), ())), preferred_element_type=F32)
            s = s + (c0[hh] - crow(hh, j))
            if masked:
                s = jnp.where(causal, s, -jnp.inf)
            m_new = jnp.maximum(m, jnp.max(s, axis=-1, keepdims=True))
            alpha = jnp.exp(m - m_new)
            p = jnp.exp(s - m_new)
            l_new = alpha * l + jnp.sum(p, axis=-1, keepdims=True)
            acc_new = alpha * acc + jnp.dot(p.astype(BF16), vt, preferred_element_type=F32)
            out.append((m_new, l_new, acc_new))
        return tuple(out)

    init1 = (jnp.full((t, 1), -jnp.inf, F32), jnp.zeros((t, 1), F32), jnp.zeros((t, LANES), F32))
    carry = lax.fori_loop(0, qi, lambda j, c: step(j, c, False), (init1, init1))
    (_, l0, a0), (_, l1, a1) = step(qi, carry, True)
    o = jnp.where(lo, a0 / l0, a1 / l1)
    sq = o * o
    ms0 = jnp.sum(jnp.where(lo, sq, 0.0), axis=-1, keepdims=True) * (1.0 / FOX_HEAD_DIM)
    ms1 = jnp.sum(jnp.where(lo, 0.0, sq), axis=-1, keepdims=True) * (1.0 / FOX_HEAD_DIM)
    inv = jnp.where(lo, lax.rsqrt(ms0 + RMS_EPS), lax.rsqrt(ms1 + RMS_EPS))
    o_ref[0] = (o * inv * g_ref[...]).astype(BF16)


def _fox(a3, c4, gain, t):
    B, S, _ = a3.shape
    return pl.pallas_call(
        functools.partial(_fox_body, t=t),
        grid=(B, FOX_HEADS // 2, S // t),
        in_specs=[
            pl.BlockSpec((1, t, LANES), lambda b, hp, qi: (b, qi, QBLK0 + hp)),
            pl.BlockSpec((1, S, LANES), lambda b, hp, qi: (b, 0, KBLK0 + hp)),
            pl.BlockSpec((1, S, LANES), lambda b, hp, qi: (b, 0, VBLK0 + hp)),
            pl.BlockSpec((1, FOX_HEADS * (S // t), 1, t), lambda b, hp, qi: (b, 0, 0, 0)),
            pl.BlockSpec((1, LANES), lambda b, hp, qi: (0, hp)),
        ],
        out_specs=pl.BlockSpec((1, t, LANES), lambda b, hp, qi: (b, qi, hp)),
        out_shape=jax.ShapeDtypeStruct((B, S, FOX_WIDTH), BF16),
        compiler_params=_cparams(("parallel", "parallel", "arbitrary")),
        name="fox_attention",
    )(a3, a3, a3, c4, gain)


def _hgrn_body(q_ref, f_ref, v_ref, gg_ref, lb_ref, gain_ref, t64_ref, t16_ref, o_ref, st_ref, *, tc):
    @pl.when(pl.program_id(2) == 0)
    def _():
        st_ref[...] = jnp.zeros_like(st_ref)

    C, SB = HG_CHUNK, HG_SUB
    lane = lax.broadcasted_iota(I32, (1, LANES), 1)
    lo = lane < HG_DK
    lb = lb_ref[...]
    fl = f_ref[0]
    t2 = jnp.log1p(-lb) + _log_sigmoid(fl)
    t1 = jnp.where(lb > 0.0, jnp.log(jnp.maximum(lb, 1e-37)), NEG_BIG)
    logf = jnp.maximum(t1, t2) + jnp.log1p(jnp.exp(-jnp.abs(t1 - t2)))
    kk = (1.0 - lb) / (1.0 + jnp.exp(fl))
    q = q_ref[0].astype(F32)
    a = _tri_dot(t64_ref[...], logf)
    r = _tri_dot(t16_ref[...], logf)
    e0 = a - r
    qr = q * jnp.exp(r)
    qa = (q * jnp.exp(a)).astype(BF16)

    rowblk = lax.broadcasted_iota(I32, (C, 1), 0) // SB
    r64 = lax.broadcasted_iota(I32, (C, C), 0)
    c64 = lax.broadcasted_iota(I32, (C, C), 1)
    causal = c64 <= r64
    rr = lax.broadcasted_iota(I32, (LANES, LANES), 0)
    cc = lax.broadcasted_iota(I32, (LANES, LANES), 1)
    same_head = (rr < HG_DK) == (cc < HG_DK)
    nt = (((1,), (1,)), ((), ()))
    tn = (((0,), (0,)), ((), ()))

    st = st_ref[...]
    for ch in range(tc // C):
        sl = slice(ch * C, (ch + 1) * C)
        a_c, kk_c, v_c = a[sl], kk[sl], v_ref[0, sl, :]
        a_last = a_c[C - 1:C, :]
        rows0, rows1 = [], []
        for blk in range(C // SB):
            b0 = ch * C + blk * SB
            e_blk = e0[b0:b0 + 1, :]
            live = rowblk <= blk
            k_blk = jnp.where(live, kk_c * jnp.exp(jnp.where(live, e_blk - a_c, 0.0)), 0.0).astype(BF16)
            q_blk = qr[b0:b0 + SB, :]
            lhs = jnp.concatenate([jnp.where(lo, q_blk, 0.0), jnp.where(lo, 0.0, q_blk)], axis=0).astype(BF16)
            sc = lax.dot_general(lhs, k_blk, nt, preferred_element_type=F32)
            rows0.append(sc[0:SB])
            rows1.append(sc[SB:2 * SB])
        s0 = jnp.where(causal, jnp.concatenate(rows0, axis=0), 0.0).astype(BF16)
        s1 = jnp.where(causal, jnp.concatenate(rows1, axis=0), 0.0).astype(BF16)
        intra = jnp.where(lo, jnp.dot(s0, v_c, preferred_element_type=F32),
                          jnp.dot(s1, v_c, preferred_element_type=F32))
        inter = lax.dot_general(qa[sl], st.astype(BF16), nt, preferred_element_type=F32)
        o = inter + intra
        kd = (kk_c * jnp.exp(a_last - a_c)).astype(BF16)
        upd = lax.dot_general(v_c, kd, tn, preferred_element_type=F32)
        st = jnp.where(same_head, st * jnp.exp(a_last) + upd, 0.0)

        sq = o * o
        ms0 = jnp.sum(jnp.where(lo, sq, 0.0), axis=-1, keepdims=True) * (1.0 / HG_DK)
        ms1 = jnp.sum(jnp.where(lo, 0.0, sq), axis=-1, keepdims=True) * (1.0 / HG_DK)
        inv = jnp.where(lo, lax.rsqrt(ms0 + RMS_EPS), lax.rsqrt(ms1 + RMS_EPS))
        g = gg_ref[0, sl, :].astype(F32)
        silu = g / (1.0 + jnp.exp(-g))
        o_ref[0, sl, :] = (o * inv * gain_ref[...] * silu).astype(BF16)
    st_ref[...] = st


def _hgrn(a3, b3, lb, gain, t64, t16, tc):
    B, S, _ = a3.shape
    blk = lambda c0: pl.BlockSpec((1, tc, LANES), lambda b, hp, s: (b, s, c0 + hp))
    row = pl.BlockSpec((1, LANES), lambda b, hp, s: (0, hp))
    tri = pl.BlockSpec((tc, tc), lambda b, hp, s: (0, 0))
    return pl.pallas_call(
        functools.partial(_hgrn_body, tc=tc),
        grid=(B, HG_HEADS // 2, S // tc),
        in_specs=[blk(GQ0), blk(0), blk(GI0), blk(GG0), row, row, tri, tri],
        out_specs=pl.BlockSpec((1, tc, LANES), lambda b, hp, s: (b, s, hp)),
        out_shape=jax.ShapeDtypeStruct((B, S, HG_WIDTH), BF16),
        scratch_shapes=[pltpu.VMEM((LANES, LANES), F32)],
        compiler_params=_cparams(("parallel", "parallel", "arbitrary")),
        name="hgrn2",
    )(a3, b3, a3, a3, lb, gain, t64, t16)


def _pool_body(u_ref, w_ref, sc_ref, o_ref, ext_ref, *, tc):
    si = pl.program_id(1)

    @pl.when(si == 0)
    def _():
        ext_ref[0:POOL_HALO, :] = jnp.zeros((POOL_HALO, POOL_WIDTH), F32)

    u = u_ref[0].astype(F32)
    ext_ref[POOL_HALO:POOL_HALO + tc, :] = u
    acc = u
    sums = {}
    for j in range(1, max(POOL_WINDOWS)):
        acc = acc + ext_ref[POOL_HALO - j:POOL_HALO - j + tc, :]
        if j + 1 in POOL_WINDOWS:
            sums[j + 1] = acc
    lane = lax.broadcasted_iota(I32, (1, POOL_WIDTH), 1)
    grp = lane // POOL_GROUP_DIM
    win = sums[POOL_WINDOWS[-1]]
    wlen = jnp.full((1, POOL_WIDTH), float(POOL_WINDOWS[-1]), F32)
    for gi in range(len(POOL_WINDOWS) - 2, -1, -1):
        win = jnp.where(grp == gi, sums[POOL_WINDOWS[gi]], win)
        wlen = jnp.where(grp == gi, float(POOL_WINDOWS[gi]), wlen)
    pos = (si * tc + lax.broadcasted_iota(I32, (tc, 1), 0) + 1).astype(F32)
    pooled = win / jnp.minimum(pos, wlen) - u
    y = jnp.dot(pooled.astype(BF16), w_ref[...], preferred_element_type=F32) * sc_ref[...]
    o_ref[0] = y.astype(BF16)
    ext_ref[0:POOL_HALO, :] = u[tc - POOL_HALO:tc, :]


def _pool(a3, wbd, scale, tc):
    B, S, _ = a3.shape
    return pl.pallas_call(
        functools.partial(_pool_body, tc=tc),
        grid=(B, S // tc),
        in_specs=[
            pl.BlockSpec((1, tc, POOL_WIDTH), lambda b, s: (b, s, PU_BLK256)),
            pl.BlockSpec((POOL_WIDTH, POOL_WIDTH), lambda b, s: (0, 0)),
            pl.BlockSpec((1, POOL_WIDTH), lambda b, s: (0, 0)),
        ],
        out_specs=pl.BlockSpec((1, tc, POOL_WIDTH), lambda b, s: (b, s, 0)),
        out_shape=jax.ShapeDtypeStruct((B, S, POOL_WIDTH), BF16),
        scratch_shapes=[pltpu.VMEM((POOL_HALO + tc, POOL_WIDTH), F32)],
        compiler_params=_cparams(("parallel", "arbitrary")),
        name="pool_mixer",
    )(a3, wbd, scale)


def _router_body(of_ref, oh_ref, op_ref, h_ref, wo_ref, g_ref, wr_ref, br_ref, tri_ref,
                 hn_ref, ri_ref, rg_ref, cnt_ref, carry_ref, *, tm):
    @pl.when(pl.program_id(0) == 0)
    def _():
        carry_ref[...] = jnp.zeros_like(carry_ref)

    f0, f1 = FOX_WIDTH, FOX_WIDTH + HG_WIDTH
    mix = (jnp.dot(of_ref[...], wo_ref[0:f0, :], preferred_element_type=F32)
           + jnp.dot(oh_ref[...], wo_ref[f0:f1, :], preferred_element_type=F32)
           + jnp.dot(op_ref[...], wo_ref[f1:f1 + POOL_WIDTH, :], preferred_element_type=F32))
    hn = h_ref[...] + mix
    hn_ref[...] = hn
    ms = jnp.mean(hn * hn, axis=-1, keepdims=True)
    xn = hn * lax.rsqrt(ms + RMS_EPS) * g_ref[...]
    x1 = xn.astype(BF16)
    x2 = (xn - x1.astype(F32)).astype(BF16)
    d = lambda a, b: jnp.dot(a, b, preferred_element_type=F32)
    logits = d(x1, wr_ref[0]) + d(x1, wr_ref[1]) + d(x2, wr_ref[0]) + br_ref[...]

    lanef = lax.broadcasted_iota(I32, (tm, LANES), 1).astype(F32)
    l = logits
    vals, idxs, sels = [], [], []
    for _ in range(TOP_K):
        mk = jnp.max(l, axis=-1, keepdims=True)
        ik = jnp.min(jnp.where(l == mk, lanef, float(LANES)), axis=-1, keepdims=True)
        sk = lanef == ik
        vals.append(mk)
        idxs.append(ik)
        sels.append(sk)
        l = jnp.where(sk, -jnp.inf, l)
    es = [jnp.exp(v - vals[0]) for v in vals]
    den = es[0] + es[1] + es[2] + es[3]
    sel = jnp.zeros((tm, LANES), F32)
    for sk in sels:
        sel = jnp.where(sk, 1.0, sel)
    rank = jnp.dot(tri_ref[...], sel.astype(BF16), preferred_element_type=F32) + carry_ref[...]
    carry = carry_ref[...] + jnp.sum(sel, axis=0, keepdims=True)
    carry_ref[...] = carry
    cnt_ref[...] = carry
    ri = jnp.zeros((tm, LANES), F32)
    rg = jnp.zeros((tm, LANES), F32)
    for k in range(TOP_K):
        pos_k = jnp.sum(jnp.where(sels[k], rank, 0.0), axis=-1, keepdims=True)
        ri = jnp.where(lanef == float(k), idxs[k], ri)
        ri = jnp.where(lanef == float(TOP_K + k), pos_k, ri)
        rg = jnp.where(lanef == float(k), es[k] / den, rg)
    ri_ref[...] = ri.astype(I32)
    rg_ref[...] = rg


def _router(o_fox, o_hg, o_pool, h, wo, g, wr, br, tri, tm):
    T, D = h.shape
    rowblk = lambda w: pl.BlockSpec((tm, w), lambda i: (i, 0))
    const = lambda shp: pl.BlockSpec(shp, lambda i: tuple(0 for _ in shp))
    return pl.pallas_call(
        functools.partial(_router_body, tm=tm),
        grid=(T // tm,),
        in_specs=[rowblk(FOX_WIDTH), rowblk(HG_WIDTH), rowblk(POOL_WIDTH), rowblk(D),
                  const((D, D)), const((1, D)), const((2, D, LANES)), const((1, LANES)), const((tm, tm))],
        out_specs=[rowblk(D), rowblk(LANES), rowblk(LANES), const((1, LANES))],
        out_shape=[jax.ShapeDtypeStruct((T, D), F32), jax.ShapeDtypeStruct((T, LANES), I32),
                   jax.ShapeDtypeStruct((T, LANES), F32), jax.ShapeDtypeStruct((1, LANES), F32)],
        scratch_shapes=[pltpu.VMEM((1, LANES), F32)],
        compiler_params=_cparams(("arbitrary",)),
        name="outproj_router",
    )(o_fox, o_hg, o_pool, h, wo, g, wr, br, tri)


def _row_copy(src_ref, src_row, dst_ref, dst_row, sem):
    return pltpu.make_async_copy(src_ref.at[pl.ds(src_row, 1)], dst_ref.at[pl.ds(dst_row, 1)], sem)


def _dispatch_body(dst_ref, h_ref, g_ref, buf_in_ref, buf_ref, xn_ref, sem, *, tt):
    del buf_in_ref
    x = h_ref[...]
    ms = jnp.mean(x * x, axis=-1, keepdims=True)
    xn_ref[...] = x * lax.rsqrt(ms + RMS_EPS) * g_ref[...]

    def issue(r, _):
        for k in range(TOP_K):
            _row_copy(xn_ref, r, buf_ref, dst_ref[0, 0, r * TOP_K + k], sem).start()
        return 0

    def drain(r, _):
        for k in range(TOP_K):
            _row_copy(xn_ref, r, buf_ref, dst_ref[0, 0, r * TOP_K + k], sem).wait()
        return 0

    lax.fori_loop(0, tt, issue, 0, unroll=DMA_UNROLL)
    lax.fori_loop(0, tt, drain, 0, unroll=DMA_UNROLL)


def _dispatch(dst3, hn, g, buf0, tt):
    T, D = hn.shape
    return pl.pallas_call(
        functools.partial(_dispatch_body, tt=tt),
        grid=(T // tt,),
        in_specs=[
            pl.BlockSpec((1, 1, tt * TOP_K), lambda i: (i, 0, 0), memory_space=pltpu.SMEM),
            pl.BlockSpec((tt, D), lambda i: (i, 0)),
            pl.BlockSpec((1, D), lambda i: (0, 0)),
            pl.BlockSpec(memory_space=pl.ANY),
        ],
        out_specs=pl.BlockSpec(memory_space=pl.ANY),
        out_shape=jax.ShapeDtypeStruct(buf0.shape, F32),
        scratch_shapes=[pltpu.VMEM((tt, D), F32), pltpu.SemaphoreType.DMA],
        input_output_aliases={3: 0},
        compiler_params=_cparams(("arbitrary",)),
        name="moe_dispatch",
    )(dst3, hn, g, buf0)


def _expert_body(te_ref, nu_ref, x_ref, w1g_ref, w1l_ref, w2_ref, b1g_ref, b1l_ref, b2_ref, y_ref):
    del te_ref

    @pl.when(pl.program_id(0) < nu_ref[0])
    def _():
        x = x_ref[...].astype(BF16)
        glu = jnp.dot(x, w1g_ref[0], preferred_element_type=F32) + b1g_ref[0]
        lin = jnp.dot(x, w1l_ref[0], preferred_element_type=F32) + b1l_ref[0]
        glu = jnp.minimum(glu, SWIGLU_LIMIT)
        lin = jnp.clip(lin, -SWIGLU_LIMIT, SWIGLU_LIMIT)
        act = glu / (1.0 + jnp.exp(-SWIGLU_ALPHA * glu)) * (lin + 1.0)
        y_ref[...] = jnp.dot(act.astype(BF16), w2_ref[0], preferred_element_type=F32) + b2_ref[0]

    @pl.when(pl.program_id(0) >= nu_ref[0])
    def _():
        y_ref[...] = jnp.zeros_like(y_ref)


def _experts(tile_e, n_used, buf, w1g, w1l, w2, b1g, b1l, b2, bm):
    R, D = buf.shape
    F = w1g.shape[-1]
    rows = pl.BlockSpec((bm, D), lambda i, te, nu: (jnp.minimum(i, nu[0] - 1), 0))
    wspec = lambda a, b: pl.BlockSpec((1, a, b), lambda i, te, nu: (te[i], 0, 0))
    grid_spec = pltpu.PrefetchScalarGridSpec(
        num_scalar_prefetch=2,
        grid=(R // bm,),
        in_specs=[rows, wspec(D, F), wspec(D, F), wspec(F, D), wspec(1, F), wspec(1, F), wspec(1, D)],
        out_specs=pl.BlockSpec((bm, D), lambda i, te, nu: (i, 0)),
    )
    return pl.pallas_call(
        _expert_body,
        grid_spec=grid_spec,
        out_shape=jax.ShapeDtypeStruct((R, D), F32),
        compiler_params=_cparams(("arbitrary",)),
        name="moe_experts",
    )(tile_e, n_used, buf, w1g, w1l, w2, b1g, b1l, b2)


def _combine_body(dst_ref, h_ref, rg_ref, gf_ref, y_ref, o_ref, rows_ref, sem, *, tt, final_norm):
    def issue(r, _):
        for k in range(TOP_K):
            _row_copy(y_ref, dst_ref[0, 0, r * TOP_K + k], rows_ref.at[k], r, sem).start()
        return 0

    def drain(r, _):
        for k in range(TOP_K):
            _row_copy(y_ref, dst_ref[0, 0, r * TOP_K + k], rows_ref.at[k], r, sem).wait()
        return 0

    lax.fori_loop(0, tt, issue, 0, unroll=DMA_UNROLL)
    lax.fori_loop(0, tt, drain, 0, unroll=DMA_UNROLL)
    out = h_ref[...]
    gates = rg_ref[...]
    for k in range(TOP_K):
        out = out + gates[:, k:k + 1] * rows_ref[k]
    if final_norm:
        ms = jnp.mean(out * out, axis=-1, keepdims=True)
        out = out * lax.rsqrt(ms + RMS_EPS) * gf_ref[...]
    o_ref[...] = out


def _combine(dst3, hn, rg, g_final, ybuf, tt, final_norm):
    T, D = hn.shape
    return pl.pallas_call(
        functools.partial(_combine_body, tt=tt, final_norm=final_norm),
        grid=(T // tt,),
        in_specs=[
            pl.BlockSpec((1, 1, tt * TOP_K), lambda i: (i, 0, 0), memory_space=pltpu.SMEM),
            pl.BlockSpec((tt, D), lambda i: (i, 0)),
            pl.BlockSpec((tt, LANES), lambda i: (i, 0)),
            pl.BlockSpec((1, D), lambda i: (0, 0)),
            pl.BlockSpec(memory_space=pl.ANY),
        ],
        out_specs=pl.BlockSpec((tt, D), lambda i: (i, 0)),
        out_shape=jax.ShapeDtypeStruct((T, D), F32),
        scratch_shapes=[pltpu.VMEM((TOP_K, tt, D), F32), pltpu.SemaphoreType.DMA],
        compiler_params=_cparams(("arbitrary",)),
        name="moe_combine",
    )(dst3, hn, rg, g_final, ybuf)


MXU_DIM = 256


def _w1_split_body(w_ref, p_ref, g_ref, l_ref):
    half = MXU_DIM // 2
    for grp in range(w_ref.shape[-1] // MXU_DIM):
        w = w_ref[0, :, grp * MXU_DIM:(grp + 1) * MXU_DIM].astype(BF16)
        r = jnp.dot(w, p_ref[...], preferred_element_type=F32).astype(BF16)
        g_ref[0, :, grp * half:(grp + 1) * half] = r[:, :half]
        l_ref[0, :, grp * half:(grp + 1) * half] = r[:, half:]


def _w1_split(w1_all, rows):
    N, D, F2 = w1_all.shape
    i = jnp.arange(MXU_DIM)[:, None]
    j = jnp.arange(MXU_DIM)[None, :]
    half = MXU_DIM // 2
    perm = jnp.where(j < half, i == 2 * j, i == 2 * (j - half) + 1).astype(BF16)
    out = jax.ShapeDtypeStruct((N, D, F2 // 2), BF16)
    return pl.pallas_call(
        _w1_split_body,
        grid=(N, D // rows),
        in_specs=[pl.BlockSpec((1, rows, F2), lambda n, r: (n, r, 0)),
                  pl.BlockSpec((MXU_DIM, MXU_DIM), lambda n, r: (0, 0))],
        out_specs=[pl.BlockSpec((1, rows, F2 // 2), lambda n, r: (n, r, 0))] * 2,
        out_shape=[out, out],
        compiler_params=_cparams(("parallel", "parallel")),
        name="w1_split",
    )(w1_all, perm)


def _tiles(B, S):
    T = B * S
    ts = min(512, S)
    tm = min(512, T)
    tt = min(256, T)
    bm = min(512, T)
    return ts, tm, tt, bm


def _block_tri(n, blk, strict=False):
    r = jnp.arange(n)[:, None]
    c = jnp.arange(n)[None, :]
    keep = (c < r) if strict else (c <= r)
    return (keep & (r // blk == c // blk)).astype(BF16)


def kernel(x, norm_mix, w_in, b_fox_f, fox_gain, hgrn_lb, hgrn_gain, pool_w, pool_scale, w_out,
           norm_moe, w_router, b_router, w1, b1, w2, b2, norm_final):
    B, S, D = x.shape
    L = w_in.shape[0]
    T = B * S
    E = w1.shape[1]
    ts, tm, tt, bm = _tiles(B, S)

    offs = [0]
    for wdt in (FOX_WIDTH, FOX_WIDTH, FOX_WIDTH, FOX_HEADS, HG_WIDTH, HG_WIDTH, HG_WIDTH, HG_WIDTH, POOL_WIDTH):
        offs.append(offs[-1] + wdt)
    col = lambda i: w_in[:, :, offs[i]:offs[i + 1]]
    scale = FOX_HEAD_DIM ** -0.5
    wa = jnp.concatenate([col(0) * scale, col(1), col(2), col(4), col(6), col(7), col(8)], axis=-1).astype(BF16)
    wb = jnp.concatenate([col(5), col(3), jnp.zeros((L, D, LANES - FOX_HEADS), F32)], axis=-1).astype(BF16)
    bf_pad = jnp.pad(b_fox_f, ((0, 0), (0, LANES - FOX_HEADS)))
    p = jax.nn.softmax(hgrn_lb.astype(F32), axis=0)
    cs = jnp.cumsum(p, axis=0)
    lbs = cs - cs[0:1]
    eye = jnp.eye(len(POOL_WINDOWS), dtype=F32)
    pool_bd = jnp.einsum("lgcd,gh->lgchd", pool_w, eye).reshape(L, POOL_WIDTH, POOL_WIDTH).astype(BF16)
    wo = w_out.astype(BF16)
    wr_pad = jnp.pad(w_router, ((0, 0), (0, 0), (0, LANES - E)))
    wr_hi = wr_pad.astype(BF16)
    wr_lo = (wr_pad - wr_hi.astype(F32)).astype(BF16)
    wr = jnp.stack([wr_hi, wr_lo], axis=1)
    br_pad = jnp.pad(b_router, ((0, 0), (0, LANES - E)), constant_values=NEG_BIG)
    F2 = w1.shape[-1]
    w1g, w1l = _w1_split(w1.reshape(L * E, D, F2), min(256, D))
    w1g = w1g.reshape(L, E, D, F2 // 2)
    w1l = w1l.reshape(L, E, D, F2 // 2)
    w2b = w2.astype(BF16)
    b1g = b1[..., 0::2][:, :, None, :]
    b1l = b1[..., 1::2][:, :, None, :]
    b2r = b2[:, :, None, :]

    tri_seq = _block_tri(ts, ts)
    tri64 = _block_tri(ts, HG_CHUNK)
    tri16 = _block_tri(ts, HG_SUB)
    tri_tok = _block_tri(tm, tm, strict=True)

    M = T * TOP_K
    R = -(-M // bm) * bm + E * bm
    NB = R // bm

    h = x.reshape(T, D)
    for l in range(L):
        a2, b2d = _inproj(h, norm_mix[l][None], wa[l], wb[l], tm)
        a3 = a2.reshape(B, S, A_COLS)
        b3 = b2d.reshape(B, S, B_COLS)
        c = _foxprep(b3, bf_pad[l][None], tri_seq, ts)
        c4 = c.reshape(B, FOX_HEADS * (S // ts), 1, ts)
        o_fox = _fox(a3, c4, fox_gain[l][None], ts)
        o_hg = _hgrn(a3, b3, lbs[l][None], hgrn_gain[l][None], tri64, tri16, ts)
        o_pool = _pool(a3, pool_bd[l], pool_scale[l][None], ts)
        hn, ri, rg, cnt = _router(o_fox.reshape(T, FOX_WIDTH), o_hg.reshape(T, HG_WIDTH),
                                  o_pool.reshape(T, POOL_WIDTH), h, wo[l], norm_moe[l][None],
                                  wr[l], br_pad[l][None], tri_tok, tm)
        counts = cnt[0, :E].astype(I32)
        padded = (counts + bm - 1) // bm * bm
        pend = jnp.cumsum(padded)
        pstart = pend - padded
        dst = pstart[ri[:, 0:TOP_K]] + ri[:, TOP_K:2 * TOP_K]
        n_used = (pend[-1] // bm).astype(I32)
        tile_ids = jnp.minimum(jnp.arange(NB, dtype=I32), n_used - 1)
        tile_e = jnp.minimum(jnp.sum(pend[None, :] <= (tile_ids * bm)[:, None], axis=1), E - 1).astype(I32)
        dst3 = dst.reshape(T // tt, 1, tt * TOP_K)
        buf = _dispatch(dst3, hn, norm_moe[l][None], jnp.zeros((R, D), F32), tt)
        ybuf = _experts(tile_e, n_used.reshape(1), buf, w1g[l], w1l[l], w2b[l], b1g[l], b1l[l], b2r[l], bm)
        h = _combine(dst3, hn, rg, norm_final[None], ybuf, tt, final_norm=(l == L - 1))
    return h.reshape(B, S, D)
```

```python
import functools

import jax
import jax.numpy as jnp
from jax import lax
from jax.experimental import pallas as pl
from jax.experimental.pallas import tpu as pltpu

F32 = jnp.float32
BF16 = jnp.bfloat16
I32 = jnp.int32

LANES = 128
VMEM_LIMIT = 56 * 1024 * 1024

FOX_HEADS = 8
FOX_HEAD_DIM = 64
FOX_WIDTH = FOX_HEADS * FOX_HEAD_DIM
HG_HEADS = 4
HG_DK = 64
HG_WIDTH = HG_HEADS * HG_DK
HG_CHUNK = 64
HG_SUB = 16
POOL_WINDOWS = (2, 4, 8, 16)
POOL_GROUP_DIM = 64
POOL_WIDTH = len(POOL_WINDOWS) * POOL_GROUP_DIM
POOL_HALO = 16
N_EXPERTS = 32
TOP_K = 4
SWIGLU_LIMIT = 7.0
SWIGLU_ALPHA = 1.702
RMS_EPS = 1e-5
NEG_BIG = -1e30
DMA_UNROLL = 8

HEAD_BLK = LANES
Q_COL0 = 0
K_COL0 = FOX_HEADS * HEAD_BLK
V_COL0 = 2 * FOX_HEADS * HEAD_BLK
GQ0 = 3 * FOX_HEADS * HEAD_BLK // LANES
GI0, GG0 = GQ0 + 2, GQ0 + 4
PU_BLK256 = (GQ0 + 6) // 2
A_COLS = (GQ0 + 8) * LANES
BIAS_LANE0 = FOX_HEAD_DIM
BIAS_TERMS = 3
LOG2E = 1.4426950408889634
SKIP_LOG2 = 150.0
B_COLS = 384


def _cparams(sem, vmem=VMEM_LIMIT):
    return pltpu.CompilerParams(dimension_semantics=sem, vmem_limit_bytes=vmem)


def _split3(x):
    x1 = x.astype(BF16)
    r1 = x - x1.astype(F32)
    x2 = r1.astype(BF16)
    x3 = (r1 - x2.astype(F32)).astype(BF16)
    return x1, x2, x3


def _tri_dot(tri, x):
    x1, x2, x3 = _split3(x)
    d = lambda a: jnp.dot(tri, a, preferred_element_type=F32)
    return d(x1) + d(x2) + d(x3)


def _log_sigmoid(z):
    return jnp.minimum(z, 0.0) - jnp.log1p(jnp.exp(-jnp.abs(z)))


def _inproj_body(x_ref, g_ref, wa_ref, wb_ref, a_ref, b_ref):
    x = x_ref[...]
    ms = jnp.mean(x * x, axis=-1, keepdims=True)
    xn = (x * lax.rsqrt(ms + RMS_EPS) * g_ref[...]).astype(BF16)
    a_ref[...] = jnp.dot(xn, wa_ref[...], preferred_element_type=F32).astype(BF16)
    b_ref[...] = jnp.dot(xn, wb_ref[...], preferred_element_type=F32)


def _inproj(h, g, wa, wb, tm):
    T, D = h.shape
    return pl.pallas_call(
        _inproj_body,
        grid=(T // tm,),
        in_specs=[
            pl.BlockSpec((tm, D), lambda i: (i, 0)),
            pl.BlockSpec((1, D), lambda i: (0, 0)),
            pl.BlockSpec((D, A_COLS), lambda i: (0, 0)),
            pl.BlockSpec((D, B_COLS), lambda i: (0, 0)),
        ],
        out_specs=[
            pl.BlockSpec((tm, A_COLS), lambda i: (i, 0)),
            pl.BlockSpec((tm, B_COLS), lambda i: (i, 0)),
        ],
        out_shape=[jax.ShapeDtypeStruct((T, A_COLS), BF16), jax.ShapeDtypeStruct((T, B_COLS), F32)],
        compiler_params=_cparams(("parallel",)),
        name="inproj",
    )(h, g, wa, wb)


def _foxprep_body(ff_ref, bf_ref, tri_ref, q_ref, k_ref, ka_ref, st_ref, carry_ref, *, ts):
    @pl.when(pl.program_id(1) == 0)
    def _():
        carry_ref[...] = jnp.zeros_like(carry_ref)

    logf = _log_sigmoid(ff_ref[0] + bf_ref[...])
    c = _tri_dot(tri_ref[...], logf) + carry_ref[...]
    carry_ref[...] = c[ts - 1:ts, :]
    cl = c * LOG2E
    lane = lax.broadcasted_iota(I32, (1, LANES), 1)
    qn_row = jnp.zeros((1, LANES), F32)
    kn_row = jnp.zeros((1, LANES), F32)
    for h in range(FOX_HEADS):
        cols = slice(h * HEAD_BLK, (h + 1) * HEAD_BLK)
        kh = k_ref[0, :, cols]
        terms = _split3(jnp.broadcast_to(-cl[:, h:h + 1], (ts, LANES)))
        ka = kh
        for i, term in enumerate(terms):
            ka = jnp.where(lane == BIAS_LANE0 + i, term, ka)
        ka_ref[0, :, cols] = ka
        kf = kh.astype(F32)
        qf = q_ref[0, :, cols].astype(F32)
        norm = lambda x: jnp.sqrt(jnp.max(jnp.sum(x * x, axis=-1, keepdims=True), axis=0, keepdims=True))
        kn_row = jnp.where(lane == h, norm(kf), kn_row)
        qn_row = jnp.where(lane == h, norm(qf), qn_row)
    rowi = lax.broadcasted_iota(I32, (8, LANES), 0)
    st = jnp.where(rowi == 0, qn_row, jnp.where(rowi == 1, kn_row, jnp.where(rowi == 2, cl[0:1, :],
                   jnp.where(rowi == 3, cl[ts - 1:ts, :], 0.0))))
    st_ref[0, 0] = st


def _foxprep(a3, b3, bf_pad, tri, ts):
    B, S, _ = b3.shape
    hw = FOX_HEADS * HEAD_BLK
    return pl.pallas_call(
        functools.partial(_foxprep_body, ts=ts),
        grid=(B, S // ts),
        in_specs=[
            pl.BlockSpec((1, ts, LANES), lambda b, s: (b, s, 2)),
            pl.BlockSpec((1, LANES), lambda b, s: (0, 0)),
            pl.BlockSpec((ts, ts), lambda b, s: (0, 0)),
            pl.BlockSpec((1, ts, hw), lambda b, s: (b, s, Q_COL0 // hw)),
            pl.BlockSpec((1, ts, hw), lambda b, s: (b, s, K_COL0 // hw)),
        ],
        out_specs=[pl.BlockSpec((1, ts, hw), lambda b, s: (b, s, 0)),
                   pl.BlockSpec((1, 1, 8, LANES), lambda b, s: (b, s, 0, 0))],
        out_shape=[jax.ShapeDtypeStruct((B, S, hw), BF16),
                   jax.ShapeDtypeStruct((B, S // ts, 8, LANES), F32)],
        scratch_shapes=[pltpu.VMEM((1, LANES), F32)],
        compiler_params=_cparams(("parallel", "arbitrary")),
        name="foxprep",
    )(b3, bf_pad, tri, a3, a3)


def _fox_body(js_ref, q_ref, k_ref, v_ref, g_ref, o_ref, s_ref, p_ref, m_ref, acc_ref, *, t, rc):
    b = pl.program_id(0)
    hp = pl.program_id(1)
    qi = pl.program_id(2)
    nt = pl.num_programs(2)
    lane = lax.broadcasted_iota(I32, (1, LANES), 1)
    lo = lane < FOX_HEAD_DIM
    ones_l = (lane >= BIAS_LANE0) & (lane < BIAS_LANE0 + BIAS_TERMS)
    sum_l = (lane == FOX_HEAD_DIM, lane == 0)
    qs = []
    for hh in range(2):
        qh = q_ref[0, :, hh * HEAD_BLK:(hh + 1) * HEAD_BLK]
        qs.append(jnp.where(ones_l, jnp.ones_like(qh), qh))
    m_ref[...] = jnp.full(m_ref.shape, -jnp.inf, F32)
    acc_ref[...] = jnp.zeros(acc_ref.shape, F32)
    base = (b * FOX_HEADS + 2 * hp) * nt + qi
    j0 = jnp.minimum(js_ref[base], js_ref[base + nt])
    nt_dims = (((1,), (1,)), ((), ()))
    nblk = t // LANES

    def tile(j, masked):
        start = pl.multiple_of(j * t, t)
        for hh in range(2):
            kt = k_ref[0, pl.ds(start, t), hh * HEAD_BLK:(hh + 1) * HEAD_BLK]
            s_ref[hh] = lax.dot_general(qs[hh], kt, nt_dims, preferred_element_type=F32)
        for hh in range(2):
            for r in range(t // rc):
                rows = slice(r * rc, (r + 1) * rc)
                diag_blk = (r * rc) // LANES
                mx = None
                for cb in range(diag_blk + 1 if masked else nblk):
                    cols = slice(cb * LANES, (cb + 1) * LANES)
                    sc = s_ref[hh, rows, cols]
                    if masked and cb == diag_blk:
                        ri = lax.broadcasted_iota(I32, (rc, LANES), 0) + r * rc
                        ci = lax.broadcasted_iota(I32, (rc, LANES), 1) + cb * LANES
                        sc = jnp.where(ci <= ri, sc, -jnp.inf)
                        s_ref[hh, rows, cols] = sc
                    mx = sc if mx is None else jnp.maximum(mx, sc)
                m_old = m_ref[hh, rows, :]
                m_new = jnp.maximum(m_old, jnp.broadcast_to(jnp.max(mx, axis=-1, keepdims=True), (rc, LANES)))
                m_ref[hh, rows, :] = m_new
                acc_ref[hh, rows, :] = jnp.exp2(m_old - m_new) * acc_ref[hh, rows, :]
            for r in range(t // rc):
                rows = slice(r * rc, (r + 1) * rc)
                diag_blk = (r * rc) // LANES
                m_new = m_ref[hh, rows, :]
                for cb in range(nblk):
                    cols = slice(cb * LANES, (cb + 1) * LANES)
                    if masked and cb > diag_blk:
                        p_ref[hh, rows, cols] = jnp.zeros((rc, LANES), BF16)
                    else:
                        p_ref[hh, rows, cols] = jnp.exp2(s_ref[hh, rows, cols] - m_new).astype(BF16)
            vt = v_ref[0, pl.ds(start, t), hh * HEAD_BLK:(hh + 1) * HEAD_BLK]
            vt = jnp.where(sum_l[hh], jnp.ones_like(vt), vt)
            acc_ref[hh] += jnp.dot(p_ref[hh], vt, preferred_element_type=F32)

    def body(j, carry):
        tile(j, False)
        return carry

    lax.fori_loop(j0, qi, body, 0)
    tile(qi, True)
    outs = []
    for hh in range(2):
        acc = acc_ref[hh]
        val = lo if hh == 0 else jnp.logical_not(lo)
        w = jnp.where(val, 1.0 / FOX_HEAD_DIM, jnp.where(sum_l[hh], RMS_EPS, 0.0))
        outs.append(acc * lax.rsqrt(jnp.sum(acc * acc * w, axis=-1, keepdims=True)))
    o_ref[0] = (jnp.where(lo, outs[0], outs[1]) * g_ref[...]).astype(BF16)


def _fox_first_tiles(stats):
    qn, kn, cs, ce = (stats[:, :, i, :FOX_HEADS].transpose(0, 2, 1) for i in range(4))
    bound = cs[..., :, None] - ce[..., None, :] + qn[..., :, None] * (kn[..., None, :] + kn[..., :, None])
    nt = qn.shape[-1]
    ti = jnp.arange(nt)[:, None]
    tj = jnp.arange(nt)[None, :]
    need = (tj == ti) | ((tj < ti) & (bound >= -SKIP_LOG2))
    return jnp.argmax(need, axis=-1).astype(I32).reshape(-1)


def _fox(first, a3, ka, gain, t, rc):
    B, S, _ = a3.shape
    pair = 2 * HEAD_BLK
    grid_spec = pltpu.PrefetchScalarGridSpec(
        num_scalar_prefetch=1,
        grid=(B, FOX_HEADS // 2, S // t),
        in_specs=[
            pl.BlockSpec((1, t, pair), lambda b, hp, qi, js: (b, qi, Q_COL0 // pair + hp)),
            pl.BlockSpec((1, S, pair), lambda b, hp, qi, js: (b, 0, hp)),
            pl.BlockSpec((1, S, pair), lambda b, hp, qi, js: (b, 0, V_COL0 // pair + hp)),
            pl.BlockSpec((1, LANES), lambda b, hp, qi, js: (0, hp)),
        ],
        out_specs=pl.BlockSpec((1, t, LANES), lambda b, hp, qi, js: (b, qi, hp)),
        scratch_shapes=[pltpu.VMEM((2, t, t), F32), pltpu.VMEM((2, t, t), BF16),
                        pltpu.VMEM((2, t, LANES), F32), pltpu.VMEM((2, t, LANES), F32)],
    )
    return pl.pallas_call(
        functools.partial(_fox_body, t=t, rc=rc),
        grid_spec=grid_spec,
        out_shape=jax.ShapeDtypeStruct((B, S, FOX_WIDTH), BF16),
        compiler_params=_cparams(("parallel", "parallel", "arbitrary")),
        name="fox_attention",
    )(first, a3, ka, a3, gain)


def _hgrn_body(q_ref, f_ref, v_ref, gg_ref, lb_ref, gain_ref, t64_ref, t16_ref, o_ref, st_ref, *, tc):
    @pl.when(pl.program_id(2) == 0)
    def _():
        st_ref[...] = jnp.zeros_like(st_ref)

    C, SB = HG_CHUNK, HG_SUB
    lane = lax.broadcasted_iota(I32, (1, LANES), 1)
    lo = lane < HG_DK
    lb = lb_ref[...]
    fl = f_ref[0]
    t2 = jnp.log1p(-lb) + _log_sigmoid(fl)
    t1 = jnp.where(lb > 0.0, jnp.log(jnp.maximum(lb, 1e-37)), NEG_BIG)
    logf = jnp.maximum(t1, t2) + jnp.log1p(jnp.exp(-jnp.abs(t1 - t2)))
    kk = (1.0 - lb) / (1.0 + jnp.exp(fl))
    q = q_ref[0].astype(F32)
    a = _tri_dot(t64_ref[...], logf)
    r = _tri_dot(t16_ref[...], logf)
    e0 = a - r
    qr = q * jnp.exp(r)
    qa = (q * jnp.exp(a)).astype(BF16)

    rowblk = lax.broadcasted_iota(I32, (C, 1), 0) // SB
    r64 = lax.broadcasted_iota(I32, (C, C), 0)
    c64 = lax.broadcasted_iota(I32, (C, C), 1)
    causal = c64 <= r64
    rr = lax.broadcasted_iota(I32, (LANES, LANES), 0)
    cc = lax.broadcasted_iota(I32, (LANES, LANES), 1)
    same_head = (rr < HG_DK) == (cc < HG_DK)
    nt = (((1,), (1,)), ((), ()))
    tn = (((0,), (0,)), ((), ()))

    st = st_ref[...]
    for ch in range(tc // C):
        sl = slice(ch * C, (ch + 1) * C)
        a_c, kk_c, v_c = a[sl], kk[sl], v_ref[0, sl, :]
        a_last = a_c[C - 1:C, :]
        rows0, rows1 = [], []
        for blk in range(C // SB):
            b0 = ch * C + blk * SB
            e_blk = e0[b0:b0 + 1, :]
            live = rowblk <= blk
            k_blk = jnp.where(live, kk_c * jnp.exp(jnp.where(live, e_blk - a_c, 0.0)), 0.0).astype(BF16)
            q_blk = qr[b0:b0 + SB, :]
            lhs = jnp.concatenate([jnp.where(lo, q_blk, 0.0), jnp.where(lo, 0.0, q_blk)], axis=0).astype(BF16)
            sc = lax.dot_general(lhs, k_blk, nt, preferred_element_type=F32)
            rows0.append(sc[0:SB])
            rows1.append(sc[SB:2 * SB])
        s0 = jnp.where(causal, jnp.concatenate(rows0, axis=0), 0.0).astype(BF16)
        s1 = jnp.where(causal, jnp.concatenate(rows1, axis=0), 0.0).astype(BF16)
        intra = jnp.where(lo, jnp.dot(s0, v_c, preferred_element_type=F32),
                          jnp.dot(s1, v_c, preferred_element_type=F32))
        inter = lax.dot_general(qa[sl], st.astype(BF16), nt, preferred_element_type=F32)
        o = inter + intra
        kd = (kk_c * jnp.exp(a_last - a_c)).astype(BF16)
        upd = lax.dot_general(v_c, kd, tn, preferred_element_type=F32)
        st = jnp.where(same_head, st * jnp.exp(a_last) + upd, 0.0)

        sq = o * o
        ms0 = jnp.sum(jnp.where(lo, sq, 0.0), axis=-1, keepdims=True) * (1.0 / HG_DK)
        ms1 = jnp.sum(jnp.where(lo, 0.0, sq), axis=-1, keepdims=True) * (1.0 / HG_DK)
        inv = jnp.where(lo, lax.rsqrt(ms0 + RMS_EPS), lax.rsqrt(ms1 + RMS_EPS))
        g = gg_ref[0, sl, :].astype(F32)
        silu = g / (1.0 + jnp.exp(-g))
        o_ref[0, sl, :] = (o * inv * gain_ref[...] * silu).astype(BF16)
    st_ref[...] = st


def _hgrn(a3, b3, lb, gain, t64, t16, tc):
    B, S, _ = a3.shape
    blk = lambda c0: pl.BlockSpec((1, tc, LANES), lambda b, hp, s: (b, s, c0 + hp))
    row = pl.BlockSpec((1, LANES), lambda b, hp, s: (0, hp))
    tri = pl.BlockSpec((tc, tc), lambda b, hp, s: (0, 0))
    return pl.pallas_call(
        functools.partial(_hgrn_body, tc=tc),
        grid=(B, HG_HEADS // 2, S // tc),
        in_specs=[blk(GQ0), blk(0), blk(GI0), blk(GG0), row, row, tri, tri],
        out_specs=pl.BlockSpec((1, tc, LANES), lambda b, hp, s: (b, s, hp)),
        out_shape=jax.ShapeDtypeStruct((B, S, HG_WIDTH), BF16),
        scratch_shapes=[pltpu.VMEM((LANES, LANES), F32)],
        compiler_params=_cparams(("parallel", "parallel", "arbitrary")),
        name="hgrn2",
    )(a3, b3, a3, a3, lb, gain, t64, t16)


def _pool_body(u_ref, w_ref, sc_ref, o_ref, ext_ref, *, tc):
    si = pl.program_id(1)

    @pl.when(si == 0)
    def _():
        ext_ref[0:POOL_HALO, :] = jnp.zeros((POOL_HALO, POOL_WIDTH), F32)

    u = u_ref[0].astype(F32)
    ext_ref[POOL_HALO:POOL_HALO + tc, :] = u
    acc = u
    sums = {}
    for j in range(1, max(POOL_WINDOWS)):
        acc = acc + ext_ref[POOL_HALO - j:POOL_HALO - j + tc, :]
        if j + 1 in POOL_WINDOWS:
            sums[j + 1] = acc
    lane = lax.broadcasted_iota(I32, (1, POOL_WIDTH), 1)
    grp = lane // POOL_GROUP_DIM
    win = sums[POOL_WINDOWS[-1]]
    wlen = jnp.full((1, POOL_WIDTH), float(POOL_WINDOWS[-1]), F32)
    for gi in range(len(POOL_WINDOWS) - 2, -1, -1):
        win = jnp.where(grp == gi, sums[POOL_WINDOWS[gi]], win)
        wlen = jnp.where(grp == gi, float(POOL_WINDOWS[gi]), wlen)
    pos = (si * tc + lax.broadcasted_iota(I32, (tc, 1), 0) + 1).astype(F32)
    pooled = win / jnp.minimum(pos, wlen) - u
    y = jnp.dot(pooled.astype(BF16), w_ref[...], preferred_element_type=F32) * sc_ref[...]
    o_ref[0] = y.astype(BF16)
    ext_ref[0:POOL_HALO, :] = u[tc - POOL_HALO:tc, :]


def _pool(a3, wbd, scale, tc):
    B, S, _ = a3.shape
    return pl.pallas_call(
        functools.partial(_pool_body, tc=tc),
        grid=(B, S // tc),
        in_specs=[
            pl.BlockSpec((1, tc, POOL_WIDTH), lambda b, s: (b, s, PU_BLK256)),
            pl.BlockSpec((POOL_WIDTH, POOL_WIDTH), lambda b, s: (0, 0)),
            pl.BlockSpec((1, POOL_WIDTH), lambda b, s: (0, 0)),
        ],
        out_specs=pl.BlockSpec((1, tc, POOL_WIDTH), lambda b, s: (b, s, 0)),
        out_shape=jax.ShapeDtypeStruct((B, S, POOL_WIDTH), BF16),
        scratch_shapes=[pltpu.VMEM((POOL_HALO + tc, POOL_WIDTH), F32)],
        compiler_params=_cparams(("parallel", "arbitrary")),
        name="pool_mixer",
    )(a3, wbd, scale)


def _router_body(of_ref, oh_ref, op_ref, h_ref, wo_ref, g_ref, wr_ref, br_ref, tri_ref,
                 hn_ref, ri_ref, rg_ref, cnt_ref, carry_ref, *, tm):
    @pl.when(pl.program_id(0) == 0)
    def _():
        carry_ref[...] = jnp.zeros_like(carry_ref)

    f0, f1 = FOX_WIDTH, FOX_WIDTH + HG_WIDTH
    mix = (jnp.dot(of_ref[...], wo_ref[0:f0, :], preferred_element_type=F32)
           + jnp.dot(oh_ref[...], wo_ref[f0:f1, :], preferred_element_type=F32)
           + jnp.dot(op_ref[...], wo_ref[f1:f1 + POOL_WIDTH, :], preferred_element_type=F32))
    hn = h_ref[...] + mix
    hn_ref[...] = hn
    ms = jnp.mean(hn * hn, axis=-1, keepdims=True)
    xn = hn * lax.rsqrt(ms + RMS_EPS) * g_ref[...]
    x1 = xn.astype(BF16)
    x2 = (xn - x1.astype(F32)).astype(BF16)
    d = lambda a, b: jnp.dot(a, b, preferred_element_type=F32)
    logits = d(x1, wr_ref[0]) + d(x1, wr_ref[1]) + d(x2, wr_ref[0]) + br_ref[...]

    lanef = lax.broadcasted_iota(I32, (tm, LANES), 1).astype(F32)
    l = logits
    vals, idxs, sels = [], [], []
    for _ in range(TOP_K):
        mk = jnp.max(l, axis=-1, keepdims=True)
        ik = jnp.min(jnp.where(l == mk, lanef, float(LANES)), axis=-1, keepdims=True)
        sk = lanef == ik
        vals.append(mk)
        idxs.append(ik)
        sels.append(sk)
        l = jnp.where(sk, -jnp.inf, l)
    es = [jnp.exp(v - vals[0]) for v in vals]
    den = es[0] + es[1] + es[2] + es[3]
    sel = jnp.zeros((tm, LANES), F32)
    for sk in sels:
        sel = jnp.where(sk, 1.0, sel)
    rank = jnp.dot(tri_ref[...], sel.astype(BF16), preferred_element_type=F32) + carry_ref[...]
    carry = carry_ref[...] + jnp.sum(sel, axis=0, keepdims=True)
    carry_ref[...] = carry
    cnt_ref[...] = carry
    ri = jnp.zeros((tm, LANES), F32)
    rg = jnp.zeros((tm, LANES), F32)
    for k in range(TOP_K):
        pos_k = jnp.sum(jnp.where(sels[k], rank, 0.0), axis=-1, keepdims=True)
        ri = jnp.where(lanef == float(k), idxs[k], ri)
        ri = jnp.where(lanef == float(TOP_K + k), pos_k, ri)
        rg = jnp.where(lanef == float(k), es[k] / den, rg)
    ri_ref[...] = ri.astype(I32)
    rg_ref[...] = rg


def _router(o_fox, o_hg, o_pool, h, wo, g, wr, br, tri, tm):
    T, D = h.shape
    rowblk = lambda w: pl.BlockSpec((tm, w), lambda i: (i, 0))
    const = lambda shp: pl.BlockSpec(shp, lambda i: tuple(0 for _ in shp))
    return pl.pallas_call(
        functools.partial(_router_body, tm=tm),
        grid=(T // tm,),
        in_specs=[rowblk(FOX_WIDTH), rowblk(HG_WIDTH), rowblk(POOL_WIDTH), rowblk(D),
                  const((D, D)), const((1, D)), const((2, D, LANES)), const((1, LANES)), const((tm, tm))],
        out_specs=[rowblk(D), rowblk(LANES), rowblk(LANES), const((1, LANES))],
        out_shape=[jax.ShapeDtypeStruct((T, D), F32), jax.ShapeDtypeStruct((T, LANES), I32),
                   jax.ShapeDtypeStruct((T, LANES), F32), jax.ShapeDtypeStruct((1, LANES), F32)],
        scratch_shapes=[pltpu.VMEM((1, LANES), F32)],
        compiler_params=_cparams(("arbitrary",)),
        name="outproj_router",
    )(o_fox, o_hg, o_pool, h, wo, g, wr, br, tri)


def _row_copy(src_ref, src_row, dst_ref, dst_row, sem):
    return pltpu.make_async_copy(src_ref.at[pl.ds(src_row, 1)], dst_ref.at[pl.ds(dst_row, 1)], sem)


def _dispatch_body(dst_ref, h_ref, g_ref, buf_in_ref, buf_ref, xn_ref, sem, *, tt):
    del buf_in_ref
    x = h_ref[...]
    ms = jnp.mean(x * x, axis=-1, keepdims=True)
    xn_ref[...] = x * lax.rsqrt(ms + RMS_EPS) * g_ref[...]

    def issue(r, _):
        for k in range(TOP_K):
            _row_copy(xn_ref, r, buf_ref, dst_ref[0, 0, r * TOP_K + k], sem).start()
        return 0

    def drain(r, _):
        for k in range(TOP_K):
            _row_copy(xn_ref, r, buf_ref, dst_ref[0, 0, r * TOP_K + k], sem).wait()
        return 0

    lax.fori_loop(0, tt, issue, 0, unroll=DMA_UNROLL)
    lax.fori_loop(0, tt, drain, 0, unroll=DMA_UNROLL)


def _dispatch(dst3, hn, g, buf0, tt):
    T, D = hn.shape
    return pl.pallas_call(
        functools.partial(_dispatch_body, tt=tt),
        grid=(T // tt,),
        in_specs=[
            pl.BlockSpec((1, 1, tt * TOP_K), lambda i: (i, 0, 0), memory_space=pltpu.SMEM),
            pl.BlockSpec((tt, D), lambda i: (i, 0)),
            pl.BlockSpec((1, D), lambda i: (0, 0)),
            pl.BlockSpec(memory_space=pl.ANY),
        ],
        out_specs=pl.BlockSpec(memory_space=pl.ANY),
        out_shape=jax.ShapeDtypeStruct(buf0.shape, F32),
        scratch_shapes=[pltpu.VMEM((tt, D), F32), pltpu.SemaphoreType.DMA],
        input_output_aliases={3: 0},
        compiler_params=_cparams(("arbitrary",)),
        name="moe_dispatch",
    )(dst3, hn, g, buf0)


def _expert_body(te_ref, nu_ref, x_ref, w1g_ref, w1l_ref, w2_ref, b1g_ref, b1l_ref, b2_ref, y_ref):
    del te_ref

    @pl.when(pl.program_id(0) < nu_ref[0])
    def _():
        x = x_ref[...].astype(BF16)
        glu = jnp.dot(x, w1g_ref[0], preferred_element_type=F32) + b1g_ref[0]
        lin = jnp.dot(x, w1l_ref[0], preferred_element_type=F32) + b1l_ref[0]
        glu = jnp.minimum(glu, SWIGLU_LIMIT)
        lin = jnp.clip(lin, -SWIGLU_LIMIT, SWIGLU_LIMIT)
        act = glu / (1.0 + jnp.exp(-SWIGLU_ALPHA * glu)) * (lin + 1.0)
        y_ref[...] = jnp.dot(act.astype(BF16), w2_ref[0], preferred_element_type=F32) + b2_ref[0]

    @pl.when(pl.program_id(0) >= nu_ref[0])
    def _():
        y_ref[...] = jnp.zeros_like(y_ref)


def _experts(tile_e, n_used, buf, w1g, w1l, w2, b1g, b1l, b2, bm):
    R, D = buf.shape
    F = w1g.shape[-1]
    rows = pl.BlockSpec((bm, D), lambda i, te, nu: (jnp.minimum(i, nu[0] - 1), 0))
    wspec = lambda a, b: pl.BlockSpec((1, a, b), lambda i, te, nu: (te[i], 0, 0))
    grid_spec = pltpu.PrefetchScalarGridSpec(
        num_scalar_prefetch=2,
        grid=(R // bm,),
        in_specs=[rows, wspec(D, F), wspec(D, F), wspec(F, D), wspec(1, F), wspec(1, F), wspec(1, D)],
        out_specs=pl.BlockSpec((bm, D), lambda i, te, nu: (i, 0)),
    )
    return pl.pallas_call(
        _expert_body,
        grid_spec=grid_spec,
        out_shape=jax.ShapeDtypeStruct((R, D), F32),
        compiler_params=_cparams(("arbitrary",)),
        name="moe_experts",
    )(tile_e, n_used, buf, w1g, w1l, w2, b1g, b1l, b2)


def _combine_body(dst_ref, h_ref, rg_ref, gf_ref, y_ref, o_ref, rows_ref, sem, *, tt, final_norm):
    def issue(r, _):
        for k in range(TOP_K):
            _row_copy(y_ref, dst_ref[0, 0, r * TOP_K + k], rows_ref.at[k], r, sem).start()
        return 0

    def drain(r, _):
        for k in range(TOP_K):
            _row_copy(y_ref, dst_ref[0, 0, r * TOP_K + k], rows_ref.at[k], r, sem).wait()
        return 0

    lax.fori_loop(0, tt, issue, 0, unroll=DMA_UNROLL)
    lax.fori_loop(0, tt, drain, 0, unroll=DMA_UNROLL)
    out = h_ref[...]
    gates = rg_ref[...]
    for k in range(TOP_K):
        out = out + gates[:, k:k + 1] * rows_ref[k]
    if final_norm:
        ms = jnp.mean(out * out, axis=-1, keepdims=True)
        out = out * lax.rsqrt(ms + RMS_EPS) * gf_ref[...]
    o_ref[...] = out


def _combine(dst3, hn, rg, g_final, ybuf, tt, final_norm):
    T, D = hn.shape
    return pl.pallas_call(
        functools.partial(_combine_body, tt=tt, final_norm=final_norm),
        grid=(T // tt,),
        in_specs=[
            pl.BlockSpec((1, 1, tt * TOP_K), lambda i: (i, 0, 0), memory_space=pltpu.SMEM),
            pl.BlockSpec((tt, D), lambda i: (i, 0)),
            pl.BlockSpec((tt, LANES), lambda i: (i, 0)),
            pl.BlockSpec((1, D), lambda i: (0, 0)),
            pl.BlockSpec(memory_space=pl.ANY),
        ],
        out_specs=pl.BlockSpec((tt, D), lambda i: (i, 0)),
        out_shape=jax.ShapeDtypeStruct((T, D), F32),
        scratch_shapes=[pltpu.VMEM((TOP_K, tt, D), F32), pltpu.SemaphoreType.DMA],
        compiler_params=_cparams(("arbitrary",)),
        name="moe_combine",
    )(dst3, hn, rg, g_final, ybuf)


MXU_DIM = 256


def _w1_split_body(w_ref, p_ref, g_ref, l_ref):
    half = MXU_DIM // 2
    for grp in range(w_ref.shape[-1] // MXU_DIM):
        w = w_ref[0, :, grp * MXU_DIM:(grp + 1) * MXU_DIM].astype(BF16)
        r = jnp.dot(w, p_ref[...], preferred_element_type=F32).astype(BF16)
        g_ref[0, :, grp * half:(grp + 1) * half] = r[:, :half]
        l_ref[0, :, grp * half:(grp + 1) * half] = r[:, half:]


def _w1_split(w1_all, rows):
    N, D, F2 = w1_all.shape
    i = jnp.arange(MXU_DIM)[:, None]
    j = jnp.arange(MXU_DIM)[None, :]
    half = MXU_DIM // 2
    perm = jnp.where(j < half, i == 2 * j, i == 2 * (j - half) + 1).astype(BF16)
    out = jax.ShapeDtypeStruct((N, D, F2 // 2), BF16)
    return pl.pallas_call(
        _w1_split_body,
        grid=(N, D // rows),
        in_specs=[pl.BlockSpec((1, rows, F2), lambda n, r: (n, r, 0)),
                  pl.BlockSpec((MXU_DIM, MXU_DIM), lambda n, r: (0, 0))],
        out_specs=[pl.BlockSpec((1, rows, F2 // 2), lambda n, r: (n, r, 0))] * 2,
        out_shape=[out, out],
        compiler_params=_cparams(("parallel", "parallel")),
        name="w1_split",
    )(w1_all, perm)


def _tiles(B, S):
    T = B * S
    ts = min(512, S)
    tm = min(512, T)
    tt = min(256, T)
    bm = min(512, T)
    return ts, tm, tt, bm


def _block_tri(n, blk, strict=False):
    r = jnp.arange(n)[:, None]
    c = jnp.arange(n)[None, :]
    keep = (c < r) if strict else (c <= r)
    return (keep & (r // blk == c // blk)).astype(BF16)


def kernel(x, norm_mix, w_in, b_fox_f, fox_gain, hgrn_lb, hgrn_gain, pool_w, pool_scale, w_out,
           norm_moe, w_router, b_router, w1, b1, w2, b2, norm_final):
    B, S, D = x.shape
    L = w_in.shape[0]
    T = B * S
    E = w1.shape[1]
    ts, tm, tt, bm = _tiles(B, S)

    offs = [0]
    for wdt in (FOX_WIDTH, FOX_WIDTH, FOX_WIDTH, FOX_HEADS, HG_WIDTH, HG_WIDTH, HG_WIDTH, HG_WIDTH, POOL_WIDTH):
        offs.append(offs[-1] + wdt)
    col = lambda i: w_in[:, :, offs[i]:offs[i + 1]]
    def head_blocks(w):
        w = w.reshape(L, D, FOX_HEADS, FOX_HEAD_DIM)
        w = jnp.pad(w, ((0, 0), (0, 0), (0, 0), (0, HEAD_BLK - FOX_HEAD_DIM)))
        return w.reshape(L, D, FOX_HEADS * HEAD_BLK)

    def head_blocks_v(w):
        w = w.reshape(L, D, FOX_HEADS, FOX_HEAD_DIM)
        odd = (jnp.arange(FOX_HEADS) % 2 == 1)[:, None]
        w = jnp.concatenate([jnp.where(odd, 0.0, w), jnp.where(odd, w, 0.0)], axis=-1)
        return w.reshape(L, D, FOX_HEADS * HEAD_BLK)

    scale = FOX_HEAD_DIM ** -0.5 * LOG2E
    wa = jnp.concatenate([head_blocks(col(0) * scale), head_blocks(col(1)), head_blocks_v(col(2)),
                          col(4), col(6), col(7), col(8)],
                         axis=-1).astype(BF16)
    wb = jnp.concatenate([col(5), col(3), jnp.zeros((L, D, LANES - FOX_HEADS), F32)], axis=-1).astype(BF16)
    bf_pad = jnp.pad(b_fox_f, ((0, 0), (0, LANES - FOX_HEADS)))
    p = jax.nn.softmax(hgrn_lb.astype(F32), axis=0)
    cs = jnp.cumsum(p, axis=0)
    lbs = cs - cs[0:1]
    eye = jnp.eye(len(POOL_WINDOWS), dtype=F32)
    pool_bd = jnp.einsum("lgcd,gh->lgchd", pool_w, eye).reshape(L, POOL_WIDTH, POOL_WIDTH).astype(BF16)
    wo = w_out.astype(BF16)
    wr_pad = jnp.pad(w_router, ((0, 0), (0, 0), (0, LANES - E)))
    wr_hi = wr_pad.astype(BF16)
    wr_lo = (wr_pad - wr_hi.astype(F32)).astype(BF16)
    wr = jnp.stack([wr_hi, wr_lo], axis=1)
    br_pad = jnp.pad(b_router, ((0, 0), (0, LANES - E)), constant_values=NEG_BIG)
    F2 = w1.shape[-1]
    w1g, w1l = _w1_split(w1.reshape(L * E, D, F2), min(256, D))
    w1g = w1g.reshape(L, E, D, F2 // 2)
    w1l = w1l.reshape(L, E, D, F2 // 2)
    w2b = w2.astype(BF16)
    b1g = b1[..., 0::2][:, :, None, :]
    b1l = b1[..., 1::2][:, :, None, :]
    b2r = b2[:, :, None, :]

    tri_seq = _block_tri(ts, ts)
    tri64 = _block_tri(ts, HG_CHUNK)
    tri16 = _block_tri(ts, HG_SUB)
    tri_tok = _block_tri(tm, tm, strict=True)

    M = T * TOP_K
    R = -(-M // bm) * bm + E * bm
    NB = R // bm

    h = x.reshape(T, D)
    for l in range(L):
        a2, b2d = _inproj(h, norm_mix[l][None], wa[l], wb[l], tm)
        a3 = a2.reshape(B, S, A_COLS)
        b3 = b2d.reshape(B, S, B_COLS)
        ka, stats = _foxprep(a3, b3, bf_pad[l][None], tri_seq, ts)
        o_fox = _fox(_fox_first_tiles(stats), a3, ka, fox_gain[l][None], ts, min(32, ts))
        o_hg = _hgrn(a3, b3, lbs[l][None], hgrn_gain[l][None], tri64, tri16, ts)
        o_pool = _pool(a3, pool_bd[l], pool_scale[l][None], ts)
        hn, ri, rg, cnt = _router(o_fox.reshape(T, FOX_WIDTH), o_hg.reshape(T, HG_WIDTH),
                                  o_pool.reshape(T, POOL_WIDTH), h, wo[l], norm_moe[l][None],
                                  wr[l], br_pad[l][None], tri_tok, tm)
        counts = cnt[0, :E].astype(I32)
        padded = (counts + bm - 1) // bm * bm
        pend = jnp.cumsum(padded)
        pstart = pend - padded
        dst = pstart[ri[:, 0:TOP_K]] + ri[:, TOP_K:2 * TOP_K]
        n_used = (pend[-1] // bm).astype(I32)
        tile_ids = jnp.minimum(jnp.arange(NB, dtype=I32), n_used - 1)
        tile_e = jnp.minimum(jnp.sum(pend[None, :] <= (tile_ids * bm)[:, None], axis=1), E - 1).astype(I32)
        dst3 = dst.reshape(T // tt, 1, tt * TOP_K)
        buf = _dispatch(dst3, hn, norm_moe[l][None], jnp.zeros((R, D), F32), tt)
        ybuf = _experts(tile_e, n_used.reshape(1), buf, w1g[l], w1l[l], w2b[l], b1g[l], b1l[l], b2r[l], bm)
        h = _combine(dst3, hn, rg, norm_final[None], ybuf, tt, final_norm=(l == L - 1))
    return h.reshape(B, S, D)
```

```python
import functools

import jax
import jax.numpy as jnp
from jax import lax
from jax.experimental import pallas as pl
from jax.experimental.pallas import tpu as pltpu

F32 = jnp.float32
BF16 = jnp.bfloat16
I32 = jnp.int32

LANES = 128
SUBLANES = 8
VMEM_LIMIT = 56 * 1024 * 1024

FOX_HEADS = 8
FOX_HEAD_DIM = 64
FOX_WIDTH = FOX_HEADS * FOX_HEAD_DIM
HG_HEADS = 4
HG_DK = 64
HG_WIDTH = HG_HEADS * HG_DK
HG_CHUNK = 64
HG_SUB = 16
POOL_WINDOWS = (2, 4, 8, 16)
POOL_GROUP_DIM = 64
POOL_WIDTH = len(POOL_WINDOWS) * POOL_GROUP_DIM
POOL_HALO = 16
N_EXPERTS = 32
TOP_K = 4
SWIGLU_LIMIT = 7.0
SWIGLU_ALPHA = 1.702
RMS_EPS = 1e-5
NEG_BIG = -1e30
DMA_UNROLL = 8

HEAD_BLK = LANES
Q_COL0 = 0
K_COL0 = FOX_HEADS * HEAD_BLK
V_COL0 = 2 * FOX_HEADS * HEAD_BLK
GQ0 = 3 * FOX_HEADS * HEAD_BLK // LANES
GI0, GG0 = GQ0 + 2, GQ0 + 4
PU_BLK256 = (GQ0 + 6) // 2
A_COLS = (GQ0 + 8) * LANES
BIAS_LANE0 = FOX_HEAD_DIM
BIAS_TERMS = 3
LOG2E = 1.4426950408889634
SKIP_LOG2 = 150.0
B_COLS = 384


def _cparams(sem, vmem=VMEM_LIMIT):
    return pltpu.CompilerParams(dimension_semantics=sem, vmem_limit_bytes=vmem)


def _split3(x):
    x1 = x.astype(BF16)
    r1 = x - x1.astype(F32)
    x2 = r1.astype(BF16)
    x3 = (r1 - x2.astype(F32)).astype(BF16)
    return x1, x2, x3


def _tri_dot(tri, x):
    x1, x2, x3 = _split3(x)
    d = lambda a: jnp.dot(tri, a, preferred_element_type=F32)
    return d(x1) + d(x2) + d(x3)


def _log_sigmoid(z):
    return jnp.minimum(z, 0.0) - jnp.log1p(jnp.exp(-jnp.abs(z)))


def _inproj_body(x_ref, g_ref, wa_ref, wb_ref, a_ref, b_ref):
    x = x_ref[...]
    ms = jnp.mean(x * x, axis=-1, keepdims=True)
    xn = (x * lax.rsqrt(ms + RMS_EPS) * g_ref[...]).astype(BF16)
    a_ref[...] = jnp.dot(xn, wa_ref[...], preferred_element_type=F32).astype(BF16)
    b_ref[...] = jnp.dot(xn, wb_ref[...], preferred_element_type=F32)


def _inproj(h, g, wa, wb, tm):
    T, D = h.shape
    return pl.pallas_call(
        _inproj_body,
        grid=(T // tm,),
        in_specs=[
            pl.BlockSpec((tm, D), lambda i: (i, 0)),
            pl.BlockSpec((1, D), lambda i: (0, 0)),
            pl.BlockSpec((D, A_COLS), lambda i: (0, 0)),
            pl.BlockSpec((D, B_COLS), lambda i: (0, 0)),
        ],
        out_specs=[
            pl.BlockSpec((tm, A_COLS), lambda i: (i, 0)),
            pl.BlockSpec((tm, B_COLS), lambda i: (i, 0)),
        ],
        out_shape=[jax.ShapeDtypeStruct((T, A_COLS), BF16), jax.ShapeDtypeStruct((T, B_COLS), F32)],
        compiler_params=_cparams(("parallel",)),
        name="inproj",
    )(h, g, wa, wb)


def _foxprep_body(ff_ref, bf_ref, tri_ref, q_ref, k_ref, ka_ref, st_ref, carry_ref, *, ts):
    @pl.when(pl.program_id(1) == 0)
    def _():
        carry_ref[...] = jnp.zeros_like(carry_ref)

    logf = _log_sigmoid(ff_ref[0] + bf_ref[...])
    c = _tri_dot(tri_ref[...], logf) + carry_ref[...]
    carry_ref[...] = c[ts - 1:ts, :]
    cl = c * LOG2E
    lane = lax.broadcasted_iota(I32, (1, LANES), 1)
    qn_row = jnp.zeros((1, LANES), F32)
    kn_row = jnp.zeros((1, LANES), F32)
    for h in range(FOX_HEADS):
        cols = slice(h * HEAD_BLK, (h + 1) * HEAD_BLK)
        kh = k_ref[0, :, cols]
        terms = _split3(jnp.broadcast_to(-cl[:, h:h + 1], (ts, LANES)))
        ka = kh
        for i, term in enumerate(terms):
            ka = jnp.where(lane == BIAS_LANE0 + i, term, ka)
        ka_ref[0, :, cols] = ka
        kf = kh.astype(F32)
        qf = q_ref[0, :, cols].astype(F32)
        norm = lambda x: jnp.sqrt(jnp.max(jnp.sum(x * x, axis=-1, keepdims=True), axis=0, keepdims=True))
        kn_row = jnp.where(lane == h, norm(kf), kn_row)
        qn_row = jnp.where(lane == h, norm(qf), qn_row)
    rowi = lax.broadcasted_iota(I32, (8, LANES), 0)
    st = jnp.where(rowi == 0, qn_row, jnp.where(rowi == 1, kn_row, jnp.where(rowi == 2, cl[0:1, :],
                   jnp.where(rowi == 3, cl[ts - 1:ts, :], 0.0))))
    st_ref[0, 0] = st


def _foxprep(a3, b3, bf_pad, tri, ts):
    B, S, _ = b3.shape
    hw = FOX_HEADS * HEAD_BLK
    return pl.pallas_call(
        functools.partial(_foxprep_body, ts=ts),
        grid=(B, S // ts),
        in_specs=[
            pl.BlockSpec((1, ts, LANES), lambda b, s: (b, s, 2)),
            pl.BlockSpec((1, LANES), lambda b, s: (0, 0)),
            pl.BlockSpec((ts, ts), lambda b, s: (0, 0)),
            pl.BlockSpec((1, ts, hw), lambda b, s: (b, s, Q_COL0 // hw)),
            pl.BlockSpec((1, ts, hw), lambda b, s: (b, s, K_COL0 // hw)),
        ],
        out_specs=[pl.BlockSpec((1, ts, hw), lambda b, s: (b, s, 0)),
                   pl.BlockSpec((1, 1, 8, LANES), lambda b, s: (b, s, 0, 0))],
        out_shape=[jax.ShapeDtypeStruct((B, S, hw), BF16),
                   jax.ShapeDtypeStruct((B, S // ts, 8, LANES), F32)],
        scratch_shapes=[pltpu.VMEM((1, LANES), F32)],
        compiler_params=_cparams(("parallel", "arbitrary")),
        name="foxprep",
    )(b3, bf_pad, tri, a3, a3)


def _fox_body(js_ref, q_ref, k_ref, v_ref, g_ref, o_ref, s_ref, p_ref, m_ref, acc_ref, *, t, rc):
    b = pl.program_id(0)
    hp = pl.program_id(1)
    qi = pl.program_id(2)
    nt = pl.num_programs(2)
    lane = lax.broadcasted_iota(I32, (1, LANES), 1)
    lo = lane < FOX_HEAD_DIM
    ones_l = (lane >= BIAS_LANE0) & (lane < BIAS_LANE0 + BIAS_TERMS)
    sum_l = (lane == FOX_HEAD_DIM, lane == 0)
    qs = []
    for hh in range(2):
        qh = q_ref[0, :, hh * HEAD_BLK:(hh + 1) * HEAD_BLK]
        qs.append(jnp.where(ones_l, jnp.ones_like(qh), qh))
    m_ref[...] = jnp.full(m_ref.shape, -jnp.inf, F32)
    acc_ref[...] = jnp.zeros(acc_ref.shape, F32)
    base = (b * FOX_HEADS + 2 * hp) * nt + qi
    j0 = jnp.minimum(js_ref[base], js_ref[base + nt])
    nt_dims = (((1,), (1,)), ((), ()))
    nblk = t // LANES

    def tile(j, masked):
        start = pl.multiple_of(j * t, t)
        for hh in range(2):
            kt = k_ref[0, pl.ds(start, t), hh * HEAD_BLK:(hh + 1) * HEAD_BLK]
            s_ref[hh] = lax.dot_general(qs[hh], kt, nt_dims, preferred_element_type=F32)
        for hh in range(2):
            for r in range(t // rc):
                rows = slice(r * rc, (r + 1) * rc)
                diag_blk = (r * rc) // LANES
                mx = None
                for cb in range(diag_blk + 1 if masked else nblk):
                    cols = slice(cb * LANES, (cb + 1) * LANES)
                    sc = s_ref[hh, rows, cols]
                    if masked and cb == diag_blk:
                        ri = lax.broadcasted_iota(I32, (rc, LANES), 0) + r * rc
                        ci = lax.broadcasted_iota(I32, (rc, LANES), 1) + cb * LANES
                        sc = jnp.where(ci <= ri, sc, -jnp.inf)
                        s_ref[hh, rows, cols] = sc
                    mx = sc if mx is None else jnp.maximum(mx, sc)
                m_old = m_ref[hh, rows, :]
                m_new = jnp.maximum(m_old, jnp.broadcast_to(jnp.max(mx, axis=-1, keepdims=True), (rc, LANES)))
                m_ref[hh, rows, :] = m_new
                acc_ref[hh, rows, :] = jnp.exp2(m_old - m_new) * acc_ref[hh, rows, :]
            for r in range(t // rc):
                rows = slice(r * rc, (r + 1) * rc)
                diag_blk = (r * rc) // LANES
                m_new = m_ref[hh, rows, :]
                for cb in range(nblk):
                    cols = slice(cb * LANES, (cb + 1) * LANES)
                    if masked and cb > diag_blk:
                        p_ref[hh, rows, cols] = jnp.zeros((rc, LANES), BF16)
                    else:
                        p_ref[hh, rows, cols] = jnp.exp2(s_ref[hh, rows, cols] - m_new).astype(BF16)
            vt = v_ref[0, pl.ds(start, t), hh * HEAD_BLK:(hh + 1) * HEAD_BLK]
            vt = jnp.where(sum_l[hh], jnp.ones_like(vt), vt)
            acc_ref[hh] += jnp.dot(p_ref[hh], vt, preferred_element_type=F32)

    def body(j, carry):
        tile(j, False)
        return carry

    lax.fori_loop(j0, qi, body, 0)
    tile(qi, True)
    outs = []
    for hh in range(2):
        acc = acc_ref[hh]
        val = lo if hh == 0 else jnp.logical_not(lo)
        w = jnp.where(val, 1.0 / FOX_HEAD_DIM, jnp.where(sum_l[hh], RMS_EPS, 0.0))
        outs.append(acc * lax.rsqrt(jnp.sum(acc * acc * w, axis=-1, keepdims=True)))
    o_ref[0] = (jnp.where(lo, outs[0], outs[1]) * g_ref[...]).astype(BF16)


def _fox_first_tiles(stats):
    qn, kn, cs, ce = (stats[:, :, i, :FOX_HEADS].transpose(0, 2, 1) for i in range(4))
    bound = cs[..., :, None] - ce[..., None, :] + qn[..., :, None] * (kn[..., None, :] + kn[..., :, None])
    nt = qn.shape[-1]
    ti = jnp.arange(nt)[:, None]
    tj = jnp.arange(nt)[None, :]
    need = (tj == ti) | ((tj < ti) & (bound >= -SKIP_LOG2))
    return jnp.argmax(need, axis=-1).astype(I32).reshape(-1)


def _fox(first, a3, ka, gain, t, rc):
    B, S, _ = a3.shape
    pair = 2 * HEAD_BLK
    grid_spec = pltpu.PrefetchScalarGridSpec(
        num_scalar_prefetch=1,
        grid=(B, FOX_HEADS // 2, S // t),
        in_specs=[
            pl.BlockSpec((1, t, pair), lambda b, hp, qi, js: (b, qi, Q_COL0 // pair + hp)),
            pl.BlockSpec((1, S, pair), lambda b, hp, qi, js: (b, 0, hp)),
            pl.BlockSpec((1, S, pair), lambda b, hp, qi, js: (b, 0, V_COL0 // pair + hp)),
            pl.BlockSpec((1, LANES), lambda b, hp, qi, js: (0, hp)),
        ],
        out_specs=pl.BlockSpec((1, t, LANES), lambda b, hp, qi, js: (b, qi, hp)),
        scratch_shapes=[pltpu.VMEM((2, t, t), F32), pltpu.VMEM((2, t, t), BF16),
                        pltpu.VMEM((2, t, LANES), F32), pltpu.VMEM((2, t, LANES), F32)],
    )
    return pl.pallas_call(
        functools.partial(_fox_body, t=t, rc=rc),
        grid_spec=grid_spec,
        out_shape=jax.ShapeDtypeStruct((B, S, FOX_WIDTH), BF16),
        compiler_params=_cparams(("parallel", "parallel", "arbitrary")),
        name="fox_attention",
    )(first, a3, ka, a3, gain)


def _hgrn_body(q_ref, f_ref, v_ref, gg_ref, lb_ref, gain_ref, t64_ref, t16_ref, o_ref, st_ref, *, tc):
    @pl.when(pl.program_id(2) == 0)
    def _():
        st_ref[...] = jnp.zeros_like(st_ref)

    C, SB = HG_CHUNK, HG_SUB
    lane = lax.broadcasted_iota(I32, (1, LANES), 1)
    lo = lane < HG_DK
    lb = lb_ref[...]
    fl = f_ref[0]
    t2 = jnp.log1p(-lb) + _log_sigmoid(fl)
    t1 = jnp.where(lb > 0.0, jnp.log(jnp.maximum(lb, 1e-37)), NEG_BIG)
    logf = jnp.maximum(t1, t2) + jnp.log1p(jnp.exp(-jnp.abs(t1 - t2)))
    kk = (1.0 - lb) / (1.0 + jnp.exp(fl))
    q = q_ref[0].astype(F32)
    a = _tri_dot(t64_ref[...], logf)
    r = _tri_dot(t16_ref[...], logf)
    e0 = a - r
    qr = q * jnp.exp(r)
    qa = (q * jnp.exp(a)).astype(BF16)

    rowblk = lax.broadcasted_iota(I32, (C, 1), 0) // SB
    r64 = lax.broadcasted_iota(I32, (C, C), 0)
    c64 = lax.broadcasted_iota(I32, (C, C), 1)
    causal = c64 <= r64
    rr = lax.broadcasted_iota(I32, (LANES, LANES), 0)
    cc = lax.broadcasted_iota(I32, (LANES, LANES), 1)
    same_head = (rr < HG_DK) == (cc < HG_DK)
    nt = (((1,), (1,)), ((), ()))
    tn = (((0,), (0,)), ((), ()))

    st = st_ref[...]
    for ch in range(tc // C):
        sl = slice(ch * C, (ch + 1) * C)
        a_c, kk_c, v_c = a[sl], kk[sl], v_ref[0, sl, :]
        a_last = a_c[C - 1:C, :]
        rows0, rows1 = [], []
        for blk in range(C // SB):
            b0 = ch * C + blk * SB
            e_blk = e0[b0:b0 + 1, :]
            live = rowblk <= blk
            k_blk = jnp.where(live, kk_c * jnp.exp(jnp.where(live, e_blk - a_c, 0.0)), 0.0).astype(BF16)
            q_blk = qr[b0:b0 + SB, :]
            lhs = jnp.concatenate([jnp.where(lo, q_blk, 0.0), jnp.where(lo, 0.0, q_blk)], axis=0).astype(BF16)
            sc = lax.dot_general(lhs, k_blk, nt, preferred_element_type=F32)
            rows0.append(sc[0:SB])
            rows1.append(sc[SB:2 * SB])
        s0 = jnp.where(causal, jnp.concatenate(rows0, axis=0), 0.0).astype(BF16)
        s1 = jnp.where(causal, jnp.concatenate(rows1, axis=0), 0.0).astype(BF16)
        intra = jnp.where(lo, jnp.dot(s0, v_c, preferred_element_type=F32),
                          jnp.dot(s1, v_c, preferred_element_type=F32))
        inter = lax.dot_general(qa[sl], st.astype(BF16), nt, preferred_element_type=F32)
        o = inter + intra
        kd = (kk_c * jnp.exp(a_last - a_c)).astype(BF16)
        upd = lax.dot_general(v_c, kd, tn, preferred_element_type=F32)
        st = jnp.where(same_head, st * jnp.exp(a_last) + upd, 0.0)

        sq = o * o
        ms0 = jnp.sum(jnp.where(lo, sq, 0.0), axis=-1, keepdims=True) * (1.0 / HG_DK)
        ms1 = jnp.sum(jnp.where(lo, 0.0, sq), axis=-1, keepdims=True) * (1.0 / HG_DK)
        inv = jnp.where(lo, lax.rsqrt(ms0 + RMS_EPS), lax.rsqrt(ms1 + RMS_EPS))
        g = gg_ref[0, sl, :].astype(F32)
        silu = g / (1.0 + jnp.exp(-g))
        o_ref[0, sl, :] = (o * inv * gain_ref[...] * silu).astype(BF16)
    st_ref[...] = st


def _hgrn(a3, b3, lb, gain, t64, t16, tc):
    B, S, _ = a3.shape
    blk = lambda c0: pl.BlockSpec((1, tc, LANES), lambda b, hp, s: (b, s, c0 + hp))
    row = pl.BlockSpec((1, LANES), lambda b, hp, s: (0, hp))
    tri = pl.BlockSpec((tc, tc), lambda b, hp, s: (0, 0))
    return pl.pallas_call(
        functools.partial(_hgrn_body, tc=tc),
        grid=(B, HG_HEADS // 2, S // tc),
        in_specs=[blk(GQ0), blk(0), blk(GI0), blk(GG0), row, row, tri, tri],
        out_specs=pl.BlockSpec((1, tc, LANES), lambda b, hp, s: (b, s, hp)),
        out_shape=jax.ShapeDtypeStruct((B, S, HG_WIDTH), BF16),
        scratch_shapes=[pltpu.VMEM((LANES, LANES), F32)],
        compiler_params=_cparams(("parallel", "parallel", "arbitrary")),
        name="hgrn2",
    )(a3, b3, a3, a3, lb, gain, t64, t16)


def _pool_body(u_ref, w_ref, sc_ref, o_ref, ext_ref, *, tc):
    si = pl.program_id(1)

    @pl.when(si == 0)
    def _():
        ext_ref[0:POOL_HALO, :] = jnp.zeros((POOL_HALO, POOL_WIDTH), F32)

    u = u_ref[0].astype(F32)
    ext_ref[POOL_HALO:POOL_HALO + tc, :] = u
    acc = u
    sums = {}
    for j in range(1, max(POOL_WINDOWS)):
        acc = acc + ext_ref[POOL_HALO - j:POOL_HALO - j + tc, :]
        if j + 1 in POOL_WINDOWS:
            sums[j + 1] = acc
    lane = lax.broadcasted_iota(I32, (1, POOL_WIDTH), 1)
    grp = lane // POOL_GROUP_DIM
    win = sums[POOL_WINDOWS[-1]]
    wlen = jnp.full((1, POOL_WIDTH), float(POOL_WINDOWS[-1]), F32)
    for gi in range(len(POOL_WINDOWS) - 2, -1, -1):
        win = jnp.where(grp == gi, sums[POOL_WINDOWS[gi]], win)
        wlen = jnp.where(grp == gi, float(POOL_WINDOWS[gi]), wlen)
    pos = (si * tc + lax.broadcasted_iota(I32, (tc, 1), 0) + 1).astype(F32)
    pooled = win / jnp.minimum(pos, wlen) - u
    y = jnp.dot(pooled.astype(BF16), w_ref[...], preferred_element_type=F32) * sc_ref[...]
    o_ref[0] = y.astype(BF16)
    ext_ref[0:POOL_HALO, :] = u[tc - POOL_HALO:tc, :]


def _pool(a3, wbd, scale, tc):
    B, S, _ = a3.shape
    return pl.pallas_call(
        functools.partial(_pool_body, tc=tc),
        grid=(B, S // tc),
        in_specs=[
            pl.BlockSpec((1, tc, POOL_WIDTH), lambda b, s: (b, s, PU_BLK256)),
            pl.BlockSpec((POOL_WIDTH, POOL_WIDTH), lambda b, s: (0, 0)),
            pl.BlockSpec((1, POOL_WIDTH), lambda b, s: (0, 0)),
        ],
        out_specs=pl.BlockSpec((1, tc, POOL_WIDTH), lambda b, s: (b, s, 0)),
        out_shape=jax.ShapeDtypeStruct((B, S, POOL_WIDTH), BF16),
        scratch_shapes=[pltpu.VMEM((POOL_HALO + tc, POOL_WIDTH), F32)],
        compiler_params=_cparams(("parallel", "arbitrary")),
        name="pool_mixer",
    )(a3, wbd, scale)


def _router_body(of_ref, oh_ref, op_ref, h_ref, wo_ref, g_ref, wr_ref, br_ref, tri_ref,
                 hn_ref, ri_ref, rg_ref, cnt_ref, carry_ref, *, tm):
    @pl.when(pl.program_id(0) == 0)
    def _():
        carry_ref[...] = jnp.zeros_like(carry_ref)

    f0, f1 = FOX_WIDTH, FOX_WIDTH + HG_WIDTH
    mix = (jnp.dot(of_ref[...], wo_ref[0:f0, :], preferred_element_type=F32)
           + jnp.dot(oh_ref[...], wo_ref[f0:f1, :], preferred_element_type=F32)
           + jnp.dot(op_ref[...], wo_ref[f1:f1 + POOL_WIDTH, :], preferred_element_type=F32))
    hn = h_ref[...] + mix
    hn_ref[...] = hn
    ms = jnp.mean(hn * hn, axis=-1, keepdims=True)
    xn = hn * lax.rsqrt(ms + RMS_EPS) * g_ref[...]
    x1 = xn.astype(BF16)
    x2 = (xn - x1.astype(F32)).astype(BF16)
    d = lambda a, b: jnp.dot(a, b, preferred_element_type=F32)
    logits = d(x1, wr_ref[0]) + d(x1, wr_ref[1]) + d(x2, wr_ref[0]) + br_ref[...]

    lanef = lax.broadcasted_iota(I32, (tm, LANES), 1).astype(F32)
    l = logits
    vals, idxs, sels = [], [], []
    for _ in range(TOP_K):
        mk = jnp.max(l, axis=-1, keepdims=True)
        ik = jnp.min(jnp.where(l == mk, lanef, float(LANES)), axis=-1, keepdims=True)
        sk = lanef == ik
        vals.append(mk)
        idxs.append(ik)
        sels.append(sk)
        l = jnp.where(sk, -jnp.inf, l)
    es = [jnp.exp(v - vals[0]) for v in vals]
    den = es[0] + es[1] + es[2] + es[3]
    sel = jnp.zeros((tm, LANES), F32)
    for sk in sels:
        sel = jnp.where(sk, 1.0, sel)
    rank = jnp.dot(tri_ref[...], sel.astype(BF16), preferred_element_type=F32) + carry_ref[...]
    carry = carry_ref[...] + jnp.sum(sel, axis=0, keepdims=True)
    carry_ref[...] = carry
    cnt_ref[...] = carry
    ri = jnp.zeros((tm, LANES), F32)
    rg = jnp.zeros((tm, LANES), F32)
    for k in range(TOP_K):
        pos_k = jnp.sum(jnp.where(sels[k], rank, 0.0), axis=-1, keepdims=True)
        ri = jnp.where(lanef == float(k), idxs[k], ri)
        ri = jnp.where(lanef == float(TOP_K + k), pos_k, ri)
        rg = jnp.where(lanef == float(k), es[k] / den, rg)
    ri_ref[...] = ri.astype(I32)
    rg_ref[...] = rg


def _router(o_fox, o_hg, o_pool, h, wo, g, wr, br, tri, tm):
    T, D = h.shape
    rowblk = lambda w: pl.BlockSpec((tm, w), lambda i: (i, 0))
    const = lambda shp: pl.BlockSpec(shp, lambda i: tuple(0 for _ in shp))
    return pl.pallas_call(
        functools.partial(_router_body, tm=tm),
        grid=(T // tm,),
        in_specs=[rowblk(FOX_WIDTH), rowblk(HG_WIDTH), rowblk(POOL_WIDTH), rowblk(D),
                  const((D, D)), const((1, D)), const((2, D, LANES)), const((1, LANES)), const((tm, tm))],
        out_specs=[rowblk(D), rowblk(LANES), rowblk(LANES), const((1, LANES))],
        out_shape=[jax.ShapeDtypeStruct((T, D), F32), jax.ShapeDtypeStruct((T, LANES), I32),
                   jax.ShapeDtypeStruct((T, LANES), F32), jax.ShapeDtypeStruct((1, LANES), F32)],
        scratch_shapes=[pltpu.VMEM((1, LANES), F32)],
        compiler_params=_cparams(("arbitrary",)),
        name="outproj_router",
    )(o_fox, o_hg, o_pool, h, wo, g, wr, br, tri)


def _row_copy(src_ref, src_row, dst_ref, dst_row, sem):
    return pltpu.make_async_copy(src_ref.at[pl.ds(src_row, 1)], dst_ref.at[pl.ds(dst_row, 1)], sem)


def _wait_rows(tile_ref, sem, times):
    for _ in range(times):
        pltpu.make_async_copy(tile_ref, tile_ref, sem).wait()


def _dispatch_body(pad0_ref, padn_ref, dst_ref, h_ref, g_ref, buf_ref, xn_ref, zero_ref, sem, zsem, *, tt, bm):
    i = pl.program_id(0)
    n = pl.num_programs(0)
    slot = i % 2
    stage = xn_ref.at[slot]

    @pl.when(i >= 2)
    def _():
        _wait_rows(stage, sem.at[slot], TOP_K)

    x = h_ref[...]
    ms = jnp.mean(x * x, axis=-1, keepdims=True)
    stage[...] = x * lax.rsqrt(ms + RMS_EPS) * g_ref[...]

    def issue(r, _):
        for k in range(TOP_K):
            _row_copy(stage, r, buf_ref, dst_ref[0, 0, r * TOP_K + k], sem.at[slot]).start(priority=k % 2)
        return 0

    lax.fori_loop(0, tt, issue, 0, unroll=DMA_UNROLL)

    @pl.when(i == 0)
    def _():
        zero_ref[...] = jnp.zeros_like(zero_ref)
        half = bm // 2
        n_exp = pad0_ref.shape[0] - 1

        def run(cp, pred, wait):
            @pl.when(pred)
            def _():
                cp.wait() if wait else cp.start()

        def pad_copies(e, wait):
            off, cnt = pad0_ref[e], padn_ref[e]
            for r in range(SUBLANES - 1):
                run(_row_copy(zero_ref, 0, buf_ref, off + r, zsem), r < (cnt & (SUBLANES - 1)), wait)
            off = off + (cnt & (SUBLANES - 1))
            sz = SUBLANES
            while sz <= half:
                take = (cnt & sz) != 0
                dst = buf_ref.at[pl.ds(pl.multiple_of(off, SUBLANES), sz)]
                run(pltpu.make_async_copy(zero_ref.at[pl.ds(0, sz)], dst, zsem), take, wait)
                off = off + jnp.where(take, sz, 0)
                sz *= 2
            return 0

        def tail_copy(c, wait):
            dst = buf_ref.at[pl.ds(pl.multiple_of(pad0_ref[n_exp] + c * half, SUBLANES), half)]
            cp = pltpu.make_async_copy(zero_ref, dst, zsem)
            cp.wait() if wait else cp.start()
            return 0

        n_tail = padn_ref[n_exp] // half
        for wait in (False, True):
            lax.fori_loop(0, n_exp, lambda e, c, w=wait: pad_copies(e, w), 0)
            lax.fori_loop(0, n_tail, lambda c, _, w=wait: tail_copy(c, w), 0)

    @pl.when(i == n - 1)
    def _():
        _wait_rows(stage, sem.at[slot], TOP_K)

    @pl.when((i == n - 1) & (n >= 2))
    def _():
        _wait_rows(xn_ref.at[1 - slot], sem.at[1 - slot], TOP_K)


def _dispatch(pad0, padn, dst3, hn, g, R, tt, bm):
    T, D = hn.shape
    grid_spec = pltpu.PrefetchScalarGridSpec(
        num_scalar_prefetch=2,
        grid=(T // tt,),
        in_specs=[
            pl.BlockSpec((1, 1, tt * TOP_K), lambda i, p0, pn: (i, 0, 0), memory_space=pltpu.SMEM),
            pl.BlockSpec((tt, D), lambda i, p0, pn: (i, 0)),
            pl.BlockSpec((1, D), lambda i, p0, pn: (0, 0)),
        ],
        out_specs=pl.BlockSpec(memory_space=pl.ANY),
        scratch_shapes=[pltpu.VMEM((2, tt, D), F32), pltpu.VMEM((bm // 2, D), F32),
                        pltpu.SemaphoreType.DMA((2,)), pltpu.SemaphoreType.DMA],
    )
    return pl.pallas_call(
        functools.partial(_dispatch_body, tt=tt, bm=bm),
        grid_spec=grid_spec,
        out_shape=jax.ShapeDtypeStruct((R, D), F32),
        compiler_params=_cparams(("arbitrary",)),
        name="moe_dispatch",
    )(pad0, padn, dst3, hn, g)


def _expert_body(te_ref, nu_ref, x_ref, w1g_ref, w1l_ref, w2_ref, b1g_ref, b1l_ref, b2_ref, y_ref):
    del te_ref

    @pl.when(pl.program_id(0) < nu_ref[0])
    def _():
        x = x_ref[...].astype(BF16)
        glu = jnp.dot(x, w1g_ref[0], preferred_element_type=F32) + b1g_ref[0]
        lin = jnp.dot(x, w1l_ref[0], preferred_element_type=F32) + b1l_ref[0]
        glu = jnp.minimum(glu, SWIGLU_LIMIT)
        lin = jnp.clip(lin, -SWIGLU_LIMIT, SWIGLU_LIMIT)
        act = glu / (1.0 + jnp.exp(-SWIGLU_ALPHA * glu)) * (lin + 1.0)
        y_ref[...] = jnp.dot(act.astype(BF16), w2_ref[0], preferred_element_type=F32) + b2_ref[0]

    @pl.when(pl.program_id(0) >= nu_ref[0])
    def _():
        y_ref[...] = jnp.zeros_like(y_ref)


def _experts(tile_e, n_used, buf, w1g, w1l, w2, b1g, b1l, b2, bm):
    R, D = buf.shape
    F = w1g.shape[-1]
    rows = pl.BlockSpec((bm, D), lambda i, te, nu: (jnp.minimum(i, nu[0] - 1), 0))
    wspec = lambda a, b: pl.BlockSpec((1, a, b), lambda i, te, nu: (te[i], 0, 0))
    grid_spec = pltpu.PrefetchScalarGridSpec(
        num_scalar_prefetch=2,
        grid=(R // bm,),
        in_specs=[rows, wspec(D, F), wspec(D, F), wspec(F, D), wspec(1, F), wspec(1, F), wspec(1, D)],
        out_specs=pl.BlockSpec((bm, D), lambda i, te, nu: (i, 0)),
    )
    return pl.pallas_call(
        _expert_body,
        grid_spec=grid_spec,
        out_shape=jax.ShapeDtypeStruct((R, D), F32),
        compiler_params=_cparams(("arbitrary",)),
        name="moe_experts",
    )(tile_e, n_used, buf, w1g, w1l, w2, b1g, b1l, b2)


def _combine_body(dst_ref, h_ref, rg_ref, gf_ref, y_ref, o_ref, rows_ref, sem, *, tt, final_norm):
    s = pl.program_id(0)
    n_tiles = pl.num_programs(0) - 1

    @pl.when(s < n_tiles)
    def _():
        slot = s % 2

        def issue(r, _):
            for k in range(TOP_K):
                _row_copy(y_ref, dst_ref[0, 0, r * TOP_K + k], rows_ref.at[slot, k], r,
                          sem.at[slot]).start(priority=k % 2)
            return 0

        lax.fori_loop(0, tt, issue, 0, unroll=DMA_UNROLL)

    @pl.when(s >= 1)
    def _():
        slot = (s - 1) % 2
        _wait_rows(rows_ref.at[slot, 0], sem.at[slot], TOP_K)
        out = h_ref[...]
        gates = rg_ref[...]
        for k in range(TOP_K):
            out = out + gates[:, k:k + 1] * rows_ref[slot, k]
        if final_norm:
            ms = jnp.mean(out * out, axis=-1, keepdims=True)
            out = out * lax.rsqrt(ms + RMS_EPS) * gf_ref[...]
        o_ref[...] = out


def _combine(dst3, hn, rg, g_final, ybuf, tt, final_norm):
    T, D = hn.shape
    n_tiles = T // tt
    prev = lambda s: (jnp.maximum(s - 1, 0), 0)
    return pl.pallas_call(
        functools.partial(_combine_body, tt=tt, final_norm=final_norm),
        grid=(n_tiles + 1,),
        in_specs=[
            pl.BlockSpec((1, 1, tt * TOP_K), lambda s: (jnp.minimum(s, n_tiles - 1), 0, 0), memory_space=pltpu.SMEM),
            pl.BlockSpec((tt, D), prev),
            pl.BlockSpec((tt, LANES), prev),
            pl.BlockSpec((1, D), lambda s: (0, 0)),
            pl.BlockSpec(memory_space=pl.ANY),
        ],
        out_specs=pl.BlockSpec((tt, D), prev),
        out_shape=jax.ShapeDtypeStruct((T, D), F32),
        scratch_shapes=[pltpu.VMEM((2, TOP_K, tt, D), F32), pltpu.SemaphoreType.DMA((2,))],
        compiler_params=_cparams(("arbitrary",)),
        name="moe_combine",
    )(dst3, hn, rg, g_final, ybuf)


MXU_DIM = 256


def _w1_split_body(w_ref, p_ref, g_ref, l_ref):
    half = MXU_DIM // 2
    for grp in range(w_ref.shape[-1] // MXU_DIM):
        w = w_ref[0, :, grp * MXU_DIM:(grp + 1) * MXU_DIM].astype(BF16)
        r = jnp.dot(w, p_ref[...], preferred_element_type=F32).astype(BF16)
        g_ref[0, :, grp * half:(grp + 1) * half] = r[:, :half]
        l_ref[0, :, grp * half:(grp + 1) * half] = r[:, half:]


def _w1_split(w1_all, rows):
    N, D, F2 = w1_all.shape
    i = jnp.arange(MXU_DIM)[:, None]
    j = jnp.arange(MXU_DIM)[None, :]
    half = MXU_DIM // 2
    perm = jnp.where(j < half, i == 2 * j, i == 2 * (j - half) + 1).astype(BF16)
    out = jax.ShapeDtypeStruct((N, D, F2 // 2), BF16)
    return pl.pallas_call(
        _w1_split_body,
        grid=(N, D // rows),
        in_specs=[pl.BlockSpec((1, rows, F2), lambda n, r: (n, r, 0)),
                  pl.BlockSpec((MXU_DIM, MXU_DIM), lambda n, r: (0, 0))],
        out_specs=[pl.BlockSpec((1, rows, F2 // 2), lambda n, r: (n, r, 0))] * 2,
        out_shape=[out, out],
        compiler_params=_cparams(("parallel", "parallel")),
        name="w1_split",
    )(w1_all, perm)


def _tiles(B, S):
    T = B * S
    ts = min(512, S)
    tm = min(512, T)
    tt = min(256, T)
    bm = min(512, T)
    return ts, tm, tt, bm


def _block_tri(n, blk, strict=False):
    r = jnp.arange(n)[:, None]
    c = jnp.arange(n)[None, :]
    keep = (c < r) if strict else (c <= r)
    return (keep & (r // blk == c // blk)).astype(BF16)


def kernel(x, norm_mix, w_in, b_fox_f, fox_gain, hgrn_lb, hgrn_gain, pool_w, pool_scale, w_out,
           norm_moe, w_router, b_router, w1, b1, w2, b2, norm_final):
    B, S, D = x.shape
    L = w_in.shape[0]
    T = B * S
    E = w1.shape[1]
    ts, tm, tt, bm = _tiles(B, S)

    offs = [0]
    for wdt in (FOX_WIDTH, FOX_WIDTH, FOX_WIDTH, FOX_HEADS, HG_WIDTH, HG_WIDTH, HG_WIDTH, HG_WIDTH, POOL_WIDTH):
        offs.append(offs[-1] + wdt)
    col = lambda i: w_in[:, :, offs[i]:offs[i + 1]]
    def head_blocks(w):
        w = w.reshape(L, D, FOX_HEADS, FOX_HEAD_DIM)
        w = jnp.pad(w, ((0, 0), (0, 0), (0, 0), (0, HEAD_BLK - FOX_HEAD_DIM)))
        return w.reshape(L, D, FOX_HEADS * HEAD_BLK)

    def head_blocks_v(w):
        w = w.reshape(L, D, FOX_HEADS, FOX_HEAD_DIM)
        odd = (jnp.arange(FOX_HEADS) % 2 == 1)[:, None]
        w = jnp.concatenate([jnp.where(odd, 0.0, w), jnp.where(odd, w, 0.0)], axis=-1)
        return w.reshape(L, D, FOX_HEADS * HEAD_BLK)

    scale = FOX_HEAD_DIM ** -0.5 * LOG2E
    wa = jnp.concatenate([head_blocks(col(0) * scale), head_blocks(col(1)), head_blocks_v(col(2)),
                          col(4), col(6), col(7), col(8)],
                         axis=-1).astype(BF16)
    wb = jnp.concatenate([col(5), col(3), jnp.zeros((L, D, LANES - FOX_HEADS), F32)], axis=-1).astype(BF16)
    bf_pad = jnp.pad(b_fox_f, ((0, 0), (0, LANES - FOX_HEADS)))
    p = jax.nn.softmax(hgrn_lb.astype(F32), axis=0)
    cs = jnp.cumsum(p, axis=0)
    lbs = cs - cs[0:1]
    eye = jnp.eye(len(POOL_WINDOWS), dtype=F32)
    pool_bd = jnp.einsum("lgcd,gh->lgchd", pool_w, eye).reshape(L, POOL_WIDTH, POOL_WIDTH).astype(BF16)
    wo = w_out.astype(BF16)
    wr_pad = jnp.pad(w_router, ((0, 0), (0, 0), (0, LANES - E)))
    wr_hi = wr_pad.astype(BF16)
    wr_lo = (wr_pad - wr_hi.astype(F32)).astype(BF16)
    wr = jnp.stack([wr_hi, wr_lo], axis=1)
    br_pad = jnp.pad(b_router, ((0, 0), (0, LANES - E)), constant_values=NEG_BIG)
    F2 = w1.shape[-1]
    w1g, w1l = _w1_split(w1.reshape(L * E, D, F2), min(256, D))
    w1g = w1g.reshape(L, E, D, F2 // 2)
    w1l = w1l.reshape(L, E, D, F2 // 2)
    w2b = w2.astype(BF16)
    b1g = b1[..., 0::2][:, :, None, :]
    b1l = b1[..., 1::2][:, :, None, :]
    b2r = b2[:, :, None, :]

    tri_seq = _block_tri(ts, ts)
    tri64 = _block_tri(ts, HG_CHUNK)
    tri16 = _block_tri(ts, HG_SUB)
    tri_tok = _block_tri(tm, tm, strict=True)

    M = T * TOP_K
    R = -(-M // bm) * bm + E * bm
    NB = R // bm

    h = x.reshape(T, D)
    for l in range(L):
        a2, b2d = _inproj(h, norm_mix[l][None], wa[l], wb[l], tm)
        a3 = a2.reshape(B, S, A_COLS)
        b3 = b2d.reshape(B, S, B_COLS)
        ka, stats = _foxprep(a3, b3, bf_pad[l][None], tri_seq, ts)
        o_fox = _fox(_fox_first_tiles(stats), a3, ka, fox_gain[l][None], ts, min(32, ts))
        o_hg = _hgrn(a3, b3, lbs[l][None], hgrn_gain[l][None], tri64, tri16, ts)
        o_pool = _pool(a3, pool_bd[l], pool_scale[l][None], ts)
        hn, ri, rg, cnt = _router(o_fox.reshape(T, FOX_WIDTH), o_hg.reshape(T, HG_WIDTH),
                                  o_pool.reshape(T, POOL_WIDTH), h, wo[l], norm_moe[l][None],
                                  wr[l], br_pad[l][None], tri_tok, tm)
        counts = cnt[0, :E].astype(I32)
        padded = (counts + bm - 1) // bm * bm
        pend = jnp.cumsum(padded)
        pstart = pend - padded
        dst = pstart[ri[:, 0:TOP_K]] + ri[:, TOP_K:2 * TOP_K]
        n_used = (pend[-1] // bm).astype(I32)
        tile_ids = jnp.minimum(jnp.arange(NB, dtype=I32), n_used - 1)
        tile_e = jnp.minimum(jnp.sum(pend[None, :] <= (tile_ids * bm)[:, None], axis=1), E - 1).astype(I32)
        dst3 = dst.reshape(T // tt, 1, tt * TOP_K)
        pad0 = jnp.concatenate([pstart + counts, pend[-1:]]).astype(I32)
        padn = jnp.concatenate([padded - counts, R - pend[-1:]]).astype(I32)
        buf = _dispatch(pad0, padn, dst3, hn, norm_moe[l][None], R, tt, bm)
        ybuf = _experts(tile_e, n_used.reshape(1), buf, w1g[l], w1l[l], w2b[l], b1g[l], b1l[l], b2r[l], bm)
        h = _combine(dst3, hn, rg, norm_final[None], ybuf, tt, final_norm=(l == L - 1))
    return h.reshape(B, S, D)
```

```python
import functools

import jax
import jax.numpy as jnp
from jax import lax
from jax.experimental import pallas as pl
from jax.experimental.pallas import tpu as pltpu

F32 = jnp.float32
BF16 = jnp.bfloat16
I32 = jnp.int32

LANES = 128
SUBLANES = 8
VMEM_LIMIT = 56 * 1024 * 1024

FOX_HEADS = 8
FOX_HEAD_DIM = 64
FOX_WIDTH = FOX_HEADS * FOX_HEAD_DIM
HG_HEADS = 4
HG_DK = 64
HG_WIDTH = HG_HEADS * HG_DK
HG_CHUNK = 64
HG_SUB = 16
POOL_WINDOWS = (2, 4, 8, 16)
POOL_GROUP_DIM = 64
POOL_WIDTH = len(POOL_WINDOWS) * POOL_GROUP_DIM
POOL_HALO = 16
N_EXPERTS = 32
TOP_K = 4
SWIGLU_LIMIT = 7.0
SWIGLU_ALPHA = 1.702
RMS_EPS = 1e-5
NEG_BIG = -1e30
DMA_UNROLL = 8

HEAD_BLK = LANES
Q_COL0 = 0
K_COL0 = FOX_HEADS * HEAD_BLK
V_COL0 = 2 * FOX_HEADS * HEAD_BLK
GQ0 = 3 * FOX_HEADS * HEAD_BLK // LANES
GI0, GG0 = GQ0 + 2, GQ0 + 4
PU_BLK256 = (GQ0 + 6) // 2
A_COLS = (GQ0 + 8) * LANES
BIAS_LANE0 = FOX_HEAD_DIM
BIAS_TERMS = 3
LOG2E = 1.4426950408889634
SKIP_LOG2 = 150.0
B_COLS = 384


def _cparams(sem, vmem=VMEM_LIMIT):
    return pltpu.CompilerParams(dimension_semantics=sem, vmem_limit_bytes=vmem)


def _split3(x):
    x1 = x.astype(BF16)
    r1 = x - x1.astype(F32)
    x2 = r1.astype(BF16)
    x3 = (r1 - x2.astype(F32)).astype(BF16)
    return x1, x2, x3


def _tri_dot(tri, x):
    x1, x2, x3 = _split3(x)
    d = lambda a: jnp.dot(tri, a, preferred_element_type=F32)
    return d(x1) + d(x2) + d(x3)


def _log_sigmoid(z):
    return jnp.minimum(z, 0.0) - jnp.log1p(jnp.exp(-jnp.abs(z)))


def _inproj_body(x_ref, g_ref, wa_ref, wb_ref, a_ref, b_ref):
    x = x_ref[...]
    ms = jnp.mean(x * x, axis=-1, keepdims=True)
    xn = (x * lax.rsqrt(ms + RMS_EPS) * g_ref[...]).astype(BF16)
    a_ref[...] = jnp.dot(xn, wa_ref[...], preferred_element_type=F32).astype(BF16)
    b_ref[...] = jnp.dot(xn, wb_ref[...], preferred_element_type=F32)


def _inproj(h, g, wa, wb, tm):
    T, D = h.shape
    return pl.pallas_call(
        _inproj_body,
        grid=(T // tm,),
        in_specs=[
            pl.BlockSpec((tm, D), lambda i: (i, 0)),
            pl.BlockSpec((1, D), lambda i: (0, 0)),
            pl.BlockSpec((D, A_COLS), lambda i: (0, 0)),
            pl.BlockSpec((D, B_COLS), lambda i: (0, 0)),
        ],
        out_specs=[
            pl.BlockSpec((tm, A_COLS), lambda i: (i, 0)),
            pl.BlockSpec((tm, B_COLS), lambda i: (i, 0)),
        ],
        out_shape=[jax.ShapeDtypeStruct((T, A_COLS), BF16), jax.ShapeDtypeStruct((T, B_COLS), F32)],
        compiler_params=_cparams(("parallel",)),
        name="inproj",
    )(h, g, wa, wb)


def _foxprep_body(ff_ref, bf_ref, tri_ref, q_ref, k_ref, ka_ref, st_ref, carry_ref, *, ts):
    @pl.when(pl.program_id(1) == 0)
    def _():
        carry_ref[...] = jnp.zeros_like(carry_ref)

    logf = _log_sigmoid(ff_ref[0] + bf_ref[...])
    c = _tri_dot(tri_ref[...], logf) + carry_ref[...]
    carry_ref[...] = c[ts - 1:ts, :]
    cl = c * LOG2E
    lane = lax.broadcasted_iota(I32, (1, LANES), 1)
    qn_row = jnp.zeros((1, LANES), F32)
    kn_row = jnp.zeros((1, LANES), F32)
    for h in range(FOX_HEADS):
        cols = slice(h * HEAD_BLK, (h + 1) * HEAD_BLK)
        kh = k_ref[0, :, cols]
        terms = _split3(jnp.broadcast_to(-cl[:, h:h + 1], (ts, LANES)))
        ka = kh
        for i, term in enumerate(terms):
            ka = jnp.where(lane == BIAS_LANE0 + i, term, ka)
        ka_ref[0, :, cols] = ka
        kf = kh.astype(F32)
        qf = q_ref[0, :, cols].astype(F32)
        norm = lambda x: jnp.sqrt(jnp.max(jnp.sum(x * x, axis=-1, keepdims=True), axis=0, keepdims=True))
        kn_row = jnp.where(lane == h, norm(kf), kn_row)
        qn_row = jnp.where(lane == h, norm(qf), qn_row)
    rowi = lax.broadcasted_iota(I32, (8, LANES), 0)
    st = jnp.where(rowi == 0, qn_row, jnp.where(rowi == 1, kn_row, jnp.where(rowi == 2, cl[0:1, :],
                   jnp.where(rowi == 3, cl[ts - 1:ts, :], 0.0))))
    st_ref[0, 0] = st


def _foxprep(a3, b3, bf_pad, tri, ts):
    B, S, _ = b3.shape
    hw = FOX_HEADS * HEAD_BLK
    return pl.pallas_call(
        functools.partial(_foxprep_body, ts=ts),
        grid=(B, S // ts),
        in_specs=[
            pl.BlockSpec((1, ts, LANES), lambda b, s: (b, s, 2)),
            pl.BlockSpec((1, LANES), lambda b, s: (0, 0)),
            pl.BlockSpec((ts, ts), lambda b, s: (0, 0)),
            pl.BlockSpec((1, ts, hw), lambda b, s: (b, s, Q_COL0 // hw)),
            pl.BlockSpec((1, ts, hw), lambda b, s: (b, s, K_COL0 // hw)),
        ],
        out_specs=[pl.BlockSpec((1, ts, hw), lambda b, s: (b, s, 0)),
                   pl.BlockSpec((1, 1, 8, LANES), lambda b, s: (b, s, 0, 0))],
        out_shape=[jax.ShapeDtypeStruct((B, S, hw), BF16),
                   jax.ShapeDtypeStruct((B, S // ts, 8, LANES), F32)],
        scratch_shapes=[pltpu.VMEM((1, LANES), F32)],
        compiler_params=_cparams(("parallel", "arbitrary")),
        name="foxprep",
    )(b3, bf_pad, tri, a3, a3)


def _fox_body(js_ref, q_ref, k_ref, v_ref, g_ref, o_ref, s_ref, p_ref, m_ref, acc_ref, *, t, rc):
    b = pl.program_id(0)
    hp = pl.program_id(1)
    qi = pl.program_id(2)
    nt = pl.num_programs(2)
    lane = lax.broadcasted_iota(I32, (1, LANES), 1)
    lo = lane < FOX_HEAD_DIM
    ones_l = (lane >= BIAS_LANE0) & (lane < BIAS_LANE0 + BIAS_TERMS)
    sum_l = (lane == FOX_HEAD_DIM, lane == 0)
    qs = []
    for hh in range(2):
        qh = q_ref[0, :, hh * HEAD_BLK:(hh + 1) * HEAD_BLK]
        qs.append(jnp.where(ones_l, jnp.ones_like(qh), qh))
    m_ref[...] = jnp.full(m_ref.shape, -jnp.inf, F32)
    acc_ref[...] = jnp.zeros(acc_ref.shape, F32)
    base = (b * FOX_HEADS + 2 * hp) * nt + qi
    j0 = jnp.minimum(js_ref[base], js_ref[base + nt])
    nt_dims = (((1,), (1,)), ((), ()))
    nblk = t // LANES

    def tile(j, masked):
        start = pl.multiple_of(j * t, t)
        for hh in range(2):
            kt = k_ref[0, pl.ds(start, t), hh * HEAD_BLK:(hh + 1) * HEAD_BLK]
            s_ref[hh] = lax.dot_general(qs[hh], kt, nt_dims, preferred_element_type=F32)
        for hh in range(2):
            for r in range(t // rc):
                rows = slice(r * rc, (r + 1) * rc)
                diag_blk = (r * rc) // LANES
                mx = None
                for cb in range(diag_blk + 1 if masked else nblk):
                    cols = slice(cb * LANES, (cb + 1) * LANES)
                    sc = s_ref[hh, rows, cols]
                    if masked and cb == diag_blk:
                        ri = lax.broadcasted_iota(I32, (rc, LANES), 0) + r * rc
                        ci = lax.broadcasted_iota(I32, (rc, LANES), 1) + cb * LANES
                        sc = jnp.where(ci <= ri, sc, -jnp.inf)
                        s_ref[hh, rows, cols] = sc
                    mx = sc if mx is None else jnp.maximum(mx, sc)
                m_old = m_ref[hh, rows, :]
                m_new = jnp.maximum(m_old, jnp.broadcast_to(jnp.max(mx, axis=-1, keepdims=True), (rc, LANES)))
                m_ref[hh, rows, :] = m_new
                acc_ref[hh, rows, :] = jnp.exp2(m_old - m_new) * acc_ref[hh, rows, :]
            for r in range(t // rc):
                rows = slice(r * rc, (r + 1) * rc)
                diag_blk = (r * rc) // LANES
                m_new = m_ref[hh, rows, :]
                for cb in range(nblk):
                    cols = slice(cb * LANES, (cb + 1) * LANES)
                    if masked and cb > diag_blk:
                        p_ref[hh, rows, cols] = jnp.zeros((rc, LANES), BF16)
                    else:
                        p_ref[hh, rows, cols] = jnp.exp2(s_ref[hh, rows, cols] - m_new).astype(BF16)
            vt = v_ref[0, pl.ds(start, t), hh * HEAD_BLK:(hh + 1) * HEAD_BLK]
            vt = jnp.where(sum_l[hh], jnp.ones_like(vt), vt)
            acc_ref[hh] += jnp.dot(p_ref[hh], vt, preferred_element_type=F32)

    def body(j, carry):
        tile(j, False)
        return carry

    lax.fori_loop(j0, qi, body, 0)
    tile(qi, True)
    outs = []
    for hh in range(2):
        acc = acc_ref[hh]
        val = lo if hh == 0 else jnp.logical_not(lo)
        w = jnp.where(val, 1.0 / FOX_HEAD_DIM, jnp.where(sum_l[hh], RMS_EPS, 0.0))
        outs.append(acc * lax.rsqrt(jnp.sum(acc * acc * w, axis=-1, keepdims=True)))
    o_ref[0] = (jnp.where(lo, outs[0], outs[1]) * g_ref[...]).astype(BF16)


def _fox_first_tiles(stats):
    qn, kn, cs, ce = (stats[:, :, i, :FOX_HEADS].transpose(0, 2, 1) for i in range(4))
    bound = cs[..., :, None] - ce[..., None, :] + qn[..., :, None] * (kn[..., None, :] + kn[..., :, None])
    nt = qn.shape[-1]
    ti = jnp.arange(nt)[:, None]
    tj = jnp.arange(nt)[None, :]
    need = (tj == ti) | ((tj < ti) & (bound >= -SKIP_LOG2))
    return jnp.argmax(need, axis=-1).astype(I32).reshape(-1)


def _fox(first, a3, ka, gain, t, rc):
    B, S, _ = a3.shape
    pair = 2 * HEAD_BLK
    grid_spec = pltpu.PrefetchScalarGridSpec(
        num_scalar_prefetch=1,
        grid=(B, FOX_HEADS // 2, S // t),
        in_specs=[
            pl.BlockSpec((1, t, pair), lambda b, hp, qi, js: (b, qi, Q_COL0 // pair + hp)),
            pl.BlockSpec((1, S, pair), lambda b, hp, qi, js: (b, 0, hp)),
            pl.BlockSpec((1, S, pair), lambda b, hp, qi, js: (b, 0, V_COL0 // pair + hp)),
            pl.BlockSpec((1, LANES), lambda b, hp, qi, js: (0, hp)),
        ],
        out_specs=pl.BlockSpec((1, t, LANES), lambda b, hp, qi, js: (b, qi, hp)),
        scratch_shapes=[pltpu.VMEM((2, t, t), F32), pltpu.VMEM((2, t, t), BF16),
                        pltpu.VMEM((2, t, LANES), F32), pltpu.VMEM((2, t, LANES), F32)],
    )
    return pl.pallas_call(
        functools.partial(_fox_body, t=t, rc=rc),
        grid_spec=grid_spec,
        out_shape=jax.ShapeDtypeStruct((B, S, FOX_WIDTH), BF16),
        compiler_params=_cparams(("parallel", "parallel", "arbitrary")),
        name="fox_attention",
    )(first, a3, ka, a3, gain)


def _hgrn_body(q_ref, f_ref, v_ref, gg_ref, lb_ref, gain_ref, t64_ref, hm_ref, o_ref, st_ref, o_acc, *, tc):
    @pl.when(pl.program_id(2) == 0)
    def _():
        st_ref[...] = jnp.zeros_like(st_ref)

    C, SB = HG_CHUNK, HG_SUB
    lane = lax.broadcasted_iota(I32, (1, LANES), 1)
    lo = lane < HG_DK
    lb = lb_ref[...]
    fl = f_ref[0]
    t2 = jnp.log1p(-lb) + _log_sigmoid(fl)
    t1 = jnp.where(lb > 0.0, jnp.log(jnp.maximum(lb, 1e-37)), NEG_BIG)
    logf = jnp.maximum(t1, t2) + jnp.log1p(jnp.exp(-jnp.abs(t1 - t2)))
    kk = (1.0 - lb) / (1.0 + jnp.exp(fl))
    q = q_ref[0].astype(F32)
    nch = tc // C
    wide = jnp.concatenate([logf[ch * C:(ch + 1) * C] for ch in range(nch)], axis=1)
    a_wide = _tri_dot(t64_ref[...], wide)
    a = jnp.concatenate([a_wide[:, ch * LANES:(ch + 1) * LANES] for ch in range(nch)], axis=0)
    qa = (q * jnp.exp(a)).astype(BF16)

    rowblk = lax.broadcasted_iota(I32, (C, 1), 0) // SB
    r64 = lax.broadcasted_iota(I32, (C, C), 0)
    c64 = lax.broadcasted_iota(I32, (C, C), 1)
    causal = c64 <= r64
    rr = lax.broadcasted_iota(I32, (LANES, LANES), 0)
    cc = lax.broadcasted_iota(I32, (LANES, LANES), 1)
    same_head = (rr < HG_DK) == (cc < HG_DK)
    nt = (((1,), (1,)), ((), ()))
    tn = (((0,), (0,)), ((), ()))

    st = st_ref[...]
    for ch in range(tc // C):
        sl = slice(ch * C, (ch + 1) * C)
        a_c, kk_c, v_c = a[sl], kk[sl], v_ref[0, sl, :]
        a_last = a_c[C - 1:C, :]
        rows0, rows1 = [], []
        for blk in range(C // SB):
            b0 = ch * C + blk * SB
            e_blk = a_c[blk * SB - 1:blk * SB, :] if blk else jnp.zeros((1, LANES), F32)
            live = rowblk <= blk
            k_blk = jnp.where(live, kk_c * jnp.exp(jnp.where(live, e_blk - a_c, 0.0)), 0.0).astype(BF16)
            q_blk = q[b0:b0 + SB, :] * jnp.exp(a_c[blk * SB:(blk + 1) * SB, :] - e_blk)
            lhs = jnp.concatenate([jnp.where(lo, q_blk, 0.0), jnp.where(lo, 0.0, q_blk)], axis=0).astype(BF16)
            sc = lax.dot_general(lhs, k_blk, nt, preferred_element_type=F32)
            rows0.append(sc[0:SB])
            rows1.append(sc[SB:2 * SB])
        s0 = jnp.where(causal, jnp.concatenate(rows0, axis=0), 0.0).astype(BF16)
        s1 = jnp.where(causal, jnp.concatenate(rows1, axis=0), 0.0).astype(BF16)
        intra = jnp.where(lo, jnp.dot(s0, v_c, preferred_element_type=F32),
                          jnp.dot(s1, v_c, preferred_element_type=F32))
        inter = lax.dot_general(qa[sl], st.astype(BF16), nt, preferred_element_type=F32)
        o = inter + intra
        kd = (kk_c * jnp.exp(a_last - a_c)).astype(BF16)
        upd = lax.dot_general(v_c, kd, tn, preferred_element_type=F32)
        st = jnp.where(same_head, st * jnp.exp(a_last) + upd, 0.0)
        o_acc[sl, :] = o
    st_ref[...] = st

    o = o_acc[...]
    sq = o * o
    hi = sq.astype(BF16)
    lo_t = (sq - hi.astype(F32)).astype(BF16)
    ms = (jnp.dot(hi, hm_ref[...], preferred_element_type=F32)
          + jnp.dot(lo_t, hm_ref[...], preferred_element_type=F32))
    g = gg_ref[0].astype(F32)
    silu = g / (1.0 + jnp.exp(-g))
    o_ref[0] = (o * lax.rsqrt(ms + RMS_EPS) * gain_ref[...] * silu).astype(BF16)


def _hgrn(a3, b3, lb, gain, tri, head_mean, tc):
    B, S, _ = a3.shape
    blk = lambda c0: pl.BlockSpec((1, tc, LANES), lambda b, hp, s: (b, s, c0 + hp))
    row = pl.BlockSpec((1, LANES), lambda b, hp, s: (0, hp))
    const = lambda n: pl.BlockSpec((n, n), lambda b, hp, s: (0, 0))
    return pl.pallas_call(
        functools.partial(_hgrn_body, tc=tc),
        grid=(B, HG_HEADS // 2, S // tc),
        in_specs=[blk(GQ0), blk(0), blk(GI0), blk(GG0), row, row, const(HG_CHUNK), const(LANES)],
        out_specs=pl.BlockSpec((1, tc, LANES), lambda b, hp, s: (b, s, hp)),
        out_shape=jax.ShapeDtypeStruct((B, S, HG_WIDTH), BF16),
        scratch_shapes=[pltpu.VMEM((LANES, LANES), F32), pltpu.VMEM((tc, LANES), F32)],
        compiler_params=_cparams(("parallel", "parallel", "arbitrary")),
        name="hgrn2",
    )(a3, b3, a3, a3, lb, gain, tri, head_mean)


def _pool_body(u_ref, w_ref, sc_ref, o_ref, ext_ref, *, tc):
    si = pl.program_id(1)

    @pl.when(si == 0)
    def _():
        ext_ref[0:POOL_HALO, :] = jnp.zeros((POOL_HALO, POOL_WIDTH), F32)

    u = u_ref[0].astype(F32)
    ext_ref[POOL_HALO:POOL_HALO + tc, :] = u
    acc = u
    sums = {}
    for j in range(1, max(POOL_WINDOWS)):
        acc = acc + ext_ref[POOL_HALO - j:POOL_HALO - j + tc, :]
        if j + 1 in POOL_WINDOWS:
            sums[j + 1] = acc
    lane = lax.broadcasted_iota(I32, (1, POOL_WIDTH), 1)
    grp = lane // POOL_GROUP_DIM
    win = sums[POOL_WINDOWS[-1]]
    wlen = jnp.full((1, POOL_WIDTH), float(POOL_WINDOWS[-1]), F32)
    for gi in range(len(POOL_WINDOWS) - 2, -1, -1):
        win = jnp.where(grp == gi, sums[POOL_WINDOWS[gi]], win)
        wlen = jnp.where(grp == gi, float(POOL_WINDOWS[gi]), wlen)
    pos = (si * tc + lax.broadcasted_iota(I32, (tc, 1), 0) + 1).astype(F32)
    pooled = win / jnp.minimum(pos, wlen) - u
    y = jnp.dot(pooled.astype(BF16), w_ref[...], preferred_element_type=F32) * sc_ref[...]
    o_ref[0] = y.astype(BF16)
    ext_ref[0:POOL_HALO, :] = u[tc - POOL_HALO:tc, :]


def _pool(a3, wbd, scale, tc):
    B, S, _ = a3.shape
    return pl.pallas_call(
        functools.partial(_pool_body, tc=tc),
        grid=(B, S // tc),
        in_specs=[
            pl.BlockSpec((1, tc, POOL_WIDTH), lambda b, s: (b, s, PU_BLK256)),
            pl.BlockSpec((POOL_WIDTH, POOL_WIDTH), lambda b, s: (0, 0)),
            pl.BlockSpec((1, POOL_WIDTH), lambda b, s: (0, 0)),
        ],
        out_specs=pl.BlockSpec((1, tc, POOL_WIDTH), lambda b, s: (b, s, 0)),
        out_shape=jax.ShapeDtypeStruct((B, S, POOL_WIDTH), BF16),
        scratch_shapes=[pltpu.VMEM((POOL_HALO + tc, POOL_WIDTH), F32)],
        compiler_params=_cparams(("parallel", "arbitrary")),
        name="pool_mixer",
    )(a3, wbd, scale)


def _router_body(of_ref, oh_ref, op_ref, h_ref, wo_ref, g_ref, wr_ref, br_ref, tri_ref,
                 hn_ref, ri_ref, rg_ref, cnt_ref, carry_ref, *, tm):
    @pl.when(pl.program_id(0) == 0)
    def _():
        carry_ref[...] = jnp.zeros_like(carry_ref)

    f0, f1 = FOX_WIDTH, FOX_WIDTH + HG_WIDTH
    mix = (jnp.dot(of_ref[...], wo_ref[0:f0, :], preferred_element_type=F32)
           + jnp.dot(oh_ref[...], wo_ref[f0:f1, :], preferred_element_type=F32)
           + jnp.dot(op_ref[...], wo_ref[f1:f1 + POOL_WIDTH, :], preferred_element_type=F32))
    hn = h_ref[...] + mix
    hn_ref[...] = hn
    ms = jnp.mean(hn * hn, axis=-1, keepdims=True)
    xn = hn * lax.rsqrt(ms + RMS_EPS) * g_ref[...]
    x1 = xn.astype(BF16)
    x2 = (xn - x1.astype(F32)).astype(BF16)
    d = lambda a, b: jnp.dot(a, b, preferred_element_type=F32)
    logits = d(x1, wr_ref[0]) + d(x1, wr_ref[1]) + d(x2, wr_ref[0]) + br_ref[...]

    lanef = lax.broadcasted_iota(I32, (tm, LANES), 1).astype(F32)
    l = logits
    vals, idxs, sels = [], [], []
    for _ in range(TOP_K):
        mk = jnp.max(l, axis=-1, keepdims=True)
        ik = jnp.min(jnp.where(l == mk, lanef, float(LANES)), axis=-1, keepdims=True)
        sk = lanef == ik
        vals.append(mk)
        idxs.append(ik)
        sels.append(sk)
        l = jnp.where(sk, -jnp.inf, l)
    es = [jnp.exp(v - vals[0]) for v in vals]
    den = es[0] + es[1] + es[2] + es[3]
    sel = jnp.zeros((tm, LANES), F32)
    for sk in sels:
        sel = jnp.where(sk, 1.0, sel)
    rank = jnp.dot(tri_ref[...], sel.astype(BF16), preferred_element_type=F32) + carry_ref[...]
    carry = carry_ref[...] + jnp.sum(sel, axis=0, keepdims=True)
    carry_ref[...] = carry
    cnt_ref[...] = carry
    ri = jnp.zeros((tm, LANES), F32)
    rg = jnp.zeros((tm, LANES), F32)
    for k in range(TOP_K):
        pos_k = jnp.sum(jnp.where(sels[k], rank, 0.0), axis=-1, keepdims=True)
        ri = jnp.where(lanef == float(k), idxs[k], ri)
        ri = jnp.where(lanef == float(TOP_K + k), pos_k, ri)
        rg = jnp.where(lanef == float(k), es[k] / den, rg)
    ri_ref[...] = ri.astype(I32)
    rg_ref[...] = rg


def _router(o_fox, o_hg, o_pool, h, wo, g, wr, br, tri, tm):
    T, D = h.shape
    rowblk = lambda w: pl.BlockSpec((tm, w), lambda i: (i, 0))
    const = lambda shp: pl.BlockSpec(shp, lambda i: tuple(0 for _ in shp))
    return pl.pallas_call(
        functools.partial(_router_body, tm=tm),
        grid=(T // tm,),
        in_specs=[rowblk(FOX_WIDTH), rowblk(HG_WIDTH), rowblk(POOL_WIDTH), rowblk(D),
                  const((D, D)), const((1, D)), const((2, D, LANES)), const((1, LANES)), const((tm, tm))],
        out_specs=[rowblk(D), rowblk(LANES), rowblk(LANES), const((1, LANES))],
        out_shape=[jax.ShapeDtypeStruct((T, D), F32), jax.ShapeDtypeStruct((T, LANES), I32),
                   jax.ShapeDtypeStruct((T, LANES), F32), jax.ShapeDtypeStruct((1, LANES), F32)],
        scratch_shapes=[pltpu.VMEM((1, LANES), F32)],
        compiler_params=_cparams(("arbitrary",)),
        name="outproj_router",
    )(o_fox, o_hg, o_pool, h, wo, g, wr, br, tri)


def _row_copy(src_ref, src_row, dst_ref, dst_row, sem):
    return pltpu.make_async_copy(src_ref.at[pl.ds(src_row, 1)], dst_ref.at[pl.ds(dst_row, 1)], sem)


def _wait_rows(tile_ref, sem, times):
    for _ in range(times):
        pltpu.make_async_copy(tile_ref, tile_ref, sem).wait()


def _dispatch_body(pad0_ref, padn_ref, dst_ref, h_ref, g_ref, buf_ref, xn_ref, zero_ref, sem, zsem, *, tt, bm):
    i = pl.program_id(0)
    n = pl.num_programs(0)
    slot = i % 2
    stage = xn_ref.at[slot]

    @pl.when(i >= 2)
    def _():
        _wait_rows(stage, sem.at[slot], TOP_K)

    x = h_ref[...]
    ms = jnp.mean(x * x, axis=-1, keepdims=True)
    stage[...] = x * lax.rsqrt(ms + RMS_EPS) * g_ref[...]

    def issue(r, _):
        for k in range(TOP_K):
            _row_copy(stage, r, buf_ref, dst_ref[0, 0, r * TOP_K + k], sem.at[slot]).start(priority=k % 2)
        return 0

    lax.fori_loop(0, tt, issue, 0, unroll=DMA_UNROLL)

    @pl.when(i == 0)
    def _():
        zero_ref[...] = jnp.zeros_like(zero_ref)
        half = bm // 2
        n_exp = pad0_ref.shape[0] - 1

        def run(cp, pred, wait):
            @pl.when(pred)
            def _():
                cp.wait() if wait else cp.start()

        def pad_copies(e, wait):
            off, cnt = pad0_ref[e], padn_ref[e]
            for r in range(SUBLANES - 1):
                run(_row_copy(zero_ref, 0, buf_ref, off + r, zsem), r < (cnt & (SUBLANES - 1)), wait)
            off = off + (cnt & (SUBLANES - 1))
            sz = SUBLANES
            while sz <= half:
                take = (cnt & sz) != 0
                dst = buf_ref.at[pl.ds(pl.multiple_of(off, SUBLANES), sz)]
                run(pltpu.make_async_copy(zero_ref.at[pl.ds(0, sz)], dst, zsem), take, wait)
                off = off + jnp.where(take, sz, 0)
                sz *= 2
            return 0

        def tail_copy(c, wait):
            dst = buf_ref.at[pl.ds(pl.multiple_of(pad0_ref[n_exp] + c * half, SUBLANES), half)]
            cp = pltpu.make_async_copy(zero_ref, dst, zsem)
            cp.wait() if wait else cp.start()
            return 0

        n_tail = padn_ref[n_exp] // half
        for wait in (False, True):
            lax.fori_loop(0, n_exp, lambda e, c, w=wait: pad_copies(e, w), 0)
            lax.fori_loop(0, n_tail, lambda c, _, w=wait: tail_copy(c, w), 0)

    @pl.when(i == n - 1)
    def _():
        _wait_rows(stage, sem.at[slot], TOP_K)

    @pl.when((i == n - 1) & (n >= 2))
    def _():
        _wait_rows(xn_ref.at[1 - slot], sem.at[1 - slot], TOP_K)


def _dispatch(pad0, padn, dst3, hn, g, R, tt, bm):
    T, D = hn.shape
    grid_spec = pltpu.PrefetchScalarGridSpec(
        num_scalar_prefetch=2,
        grid=(T // tt,),
        in_specs=[
            pl.BlockSpec((1, 1, tt * TOP_K), lambda i, p0, pn: (i, 0, 0), memory_space=pltpu.SMEM),
            pl.BlockSpec((tt, D), lambda i, p0, pn: (i, 0)),
            pl.BlockSpec((1, D), lambda i, p0, pn: (0, 0)),
        ],
        out_specs=pl.BlockSpec(memory_space=pl.ANY),
        scratch_shapes=[pltpu.VMEM((2, tt, D), F32), pltpu.VMEM((bm // 2, D), F32),
                        pltpu.SemaphoreType.DMA((2,)), pltpu.SemaphoreType.DMA],
    )
    return pl.pallas_call(
        functools.partial(_dispatch_body, tt=tt, bm=bm),
        grid_spec=grid_spec,
        out_shape=jax.ShapeDtypeStruct((R, D), F32),
        compiler_params=_cparams(("arbitrary",)),
        name="moe_dispatch",
    )(pad0, padn, dst3, hn, g)


def _expert_body(te_ref, nu_ref, x_ref, w1g_ref, w1l_ref, w2_ref, b1g_ref, b1l_ref, b2_ref, y_ref):
    del te_ref

    @pl.when(pl.program_id(0) < nu_ref[0])
    def _():
        x = x_ref[...].astype(BF16)
        glu = jnp.dot(x, w1g_ref[0], preferred_element_type=F32) + b1g_ref[0]
        lin = jnp.dot(x, w1l_ref[0], preferred_element_type=F32) + b1l_ref[0]
        glu = jnp.minimum(glu, SWIGLU_LIMIT)
        lin = jnp.clip(lin, -SWIGLU_LIMIT, SWIGLU_LIMIT)
        act = glu / (1.0 + jnp.exp(-SWIGLU_ALPHA * glu)) * (lin + 1.0)
        y_ref[...] = jnp.dot(act.astype(BF16), w2_ref[0], preferred_element_type=F32) + b2_ref[0]

    @pl.when(pl.program_id(0) >= nu_ref[0])
    def _():
        y_ref[...] = jnp.zeros_like(y_ref)


def _experts(tile_e, n_used, buf, w1g, w1l, w2, b1g, b1l, b2, bm):
    R, D = buf.shape
    F = w1g.shape[-1]
    rows = pl.BlockSpec((bm, D), lambda i, te, nu: (jnp.minimum(i, nu[0] - 1), 0))
    wspec = lambda a, b: pl.BlockSpec((1, a, b), lambda i, te, nu: (te[i], 0, 0))
    grid_spec = pltpu.PrefetchScalarGridSpec(
        num_scalar_prefetch=2,
        grid=(R // bm,),
        in_specs=[rows, wspec(D, F), wspec(D, F), wspec(F, D), wspec(1, F), wspec(1, F), wspec(1, D)],
        out_specs=pl.BlockSpec((bm, D), lambda i, te, nu: (i, 0)),
    )
    return pl.pallas_call(
        _expert_body,
        grid_spec=grid_spec,
        out_shape=jax.ShapeDtypeStruct((R, D), F32),
        compiler_params=_cparams(("arbitrary",)),
        name="moe_experts",
    )(tile_e, n_used, buf, w1g, w1l, w2, b1g, b1l, b2)


def _combine_body(dst_ref, h_ref, rg_ref, gf_ref, y_ref, o_ref, rows_ref, sem, *, tt, final_norm):
    s = pl.program_id(0)
    n_tiles = pl.num_programs(0) - 1

    @pl.when(s < n_tiles)
    def _():
        slot = s % 2

        def issue(r, _):
            for k in range(TOP_K):
                _row_copy(y_ref, dst_ref[0, 0, r * TOP_K + k], rows_ref.at[slot, k], r,
                          sem.at[slot]).start(priority=k % 2)
            return 0

        lax.fori_loop(0, tt, issue, 0, unroll=DMA_UNROLL)

    @pl.when(s >= 1)
    def _():
        slot = (s - 1) % 2
        _wait_rows(rows_ref.at[slot, 0], sem.at[slot], TOP_K)
        out = h_ref[...]
        gates = rg_ref[...]
        for k in range(TOP_K):
            out = out + gates[:, k:k + 1] * rows_ref[slot, k]
        if final_norm:
            ms = jnp.mean(out * out, axis=-1, keepdims=True)
            out = out * lax.rsqrt(ms + RMS_EPS) * gf_ref[...]
        o_ref[...] = out


def _combine(dst3, hn, rg, g_final, ybuf, tt, final_norm):
    T, D = hn.shape
    n_tiles = T // tt
    prev = lambda s: (jnp.maximum(s - 1, 0), 0)
    return pl.pallas_call(
        functools.partial(_combine_body, tt=tt, final_norm=final_norm),
        grid=(n_tiles + 1,),
        in_specs=[
            pl.BlockSpec((1, 1, tt * TOP_K), lambda s: (jnp.minimum(s, n_tiles - 1), 0, 0), memory_space=pltpu.SMEM),
            pl.BlockSpec((tt, D), prev),
            pl.BlockSpec((tt, LANES), prev),
            pl.BlockSpec((1, D), lambda s: (0, 0)),
            pl.BlockSpec(memory_space=pl.ANY),
        ],
        out_specs=pl.BlockSpec((tt, D), prev),
        out_shape=jax.ShapeDtypeStruct((T, D), F32),
        scratch_shapes=[pltpu.VMEM((2, TOP_K, tt, D), F32), pltpu.SemaphoreType.DMA((2,))],
        compiler_params=_cparams(("arbitrary",)),
        name="moe_combine",
    )(dst3, hn, rg, g_final, ybuf)


MXU_DIM = 256


def _w1_split_body(w_ref, p_ref, g_ref, l_ref):
    half = MXU_DIM // 2
    for grp in range(w_ref.shape[-1] // MXU_DIM):
        w = w_ref[0, :, grp * MXU_DIM:(grp + 1) * MXU_DIM].astype(BF16)
        r = jnp.dot(w, p_ref[...], preferred_element_type=F32).astype(BF16)
        g_ref[0, :, grp * half:(grp + 1) * half] = r[:, :half]
        l_ref[0, :, grp * half:(grp + 1) * half] = r[:, half:]


def _w1_split(w1_all, rows):
    N, D, F2 = w1_all.shape
    i = jnp.arange(MXU_DIM)[:, None]
    j = jnp.arange(MXU_DIM)[None, :]
    half = MXU_DIM // 2
    perm = jnp.where(j < half, i == 2 * j, i == 2 * (j - half) + 1).astype(BF16)
    out = jax.ShapeDtypeStruct((N, D, F2 // 2), BF16)
    return pl.pallas_call(
        _w1_split_body,
        grid=(N, D // rows),
        in_specs=[pl.BlockSpec((1, rows, F2), lambda n, r: (n, r, 0)),
                  pl.BlockSpec((MXU_DIM, MXU_DIM), lambda n, r: (0, 0))],
        out_specs=[pl.BlockSpec((1, rows, F2 // 2), lambda n, r: (n, r, 0))] * 2,
        out_shape=[out, out],
        compiler_params=_cparams(("parallel", "parallel")),
        name="w1_split",
    )(w1_all, perm)


def _tiles(B, S):
    T = B * S
    ts = min(512, S)
    tm = min(512, T)
    tt = min(256, T)
    bm = min(512, T)
    return ts, tm, tt, bm


def _block_tri(n, blk, strict=False):
    r = jnp.arange(n)[:, None]
    c = jnp.arange(n)[None, :]
    keep = (c < r) if strict else (c <= r)
    return (keep & (r // blk == c // blk)).astype(BF16)


def kernel(x, norm_mix, w_in, b_fox_f, fox_gain, hgrn_lb, hgrn_gain, pool_w, pool_scale, w_out,
           norm_moe, w_router, b_router, w1, b1, w2, b2, norm_final):
    B, S, D = x.shape
    L = w_in.shape[0]
    T = B * S
    E = w1.shape[1]
    ts, tm, tt, bm = _tiles(B, S)

    offs = [0]
    for wdt in (FOX_WIDTH, FOX_WIDTH, FOX_WIDTH, FOX_HEADS, HG_WIDTH, HG_WIDTH, HG_WIDTH, HG_WIDTH, POOL_WIDTH):
        offs.append(offs[-1] + wdt)
    col = lambda i: w_in[:, :, offs[i]:offs[i + 1]]
    def head_blocks(w):
        w = w.reshape(L, D, FOX_HEADS, FOX_HEAD_DIM)
        w = jnp.pad(w, ((0, 0), (0, 0), (0, 0), (0, HEAD_BLK - FOX_HEAD_DIM)))
        return w.reshape(L, D, FOX_HEADS * HEAD_BLK)

    def head_blocks_v(w):
        w = w.reshape(L, D, FOX_HEADS, FOX_HEAD_DIM)
        odd = (jnp.arange(FOX_HEADS) % 2 == 1)[:, None]
        w = jnp.concatenate([jnp.where(odd, 0.0, w), jnp.where(odd, w, 0.0)], axis=-1)
        return w.reshape(L, D, FOX_HEADS * HEAD_BLK)

    scale = FOX_HEAD_DIM ** -0.5 * LOG2E
    wa = jnp.concatenate([head_blocks(col(0) * scale), head_blocks(col(1)), head_blocks_v(col(2)),
                          col(4), col(6), col(7), col(8)],
                         axis=-1).astype(BF16)
    wb = jnp.concatenate([col(5), col(3), jnp.zeros((L, D, LANES - FOX_HEADS), F32)], axis=-1).astype(BF16)
    bf_pad = jnp.pad(b_fox_f, ((0, 0), (0, LANES - FOX_HEADS)))
    p = jax.nn.softmax(hgrn_lb.astype(F32), axis=0)
    cs = jnp.cumsum(p, axis=0)
    lbs = cs - cs[0:1]
    eye = jnp.eye(len(POOL_WINDOWS), dtype=F32)
    pool_bd = jnp.einsum("lgcd,gh->lgchd", pool_w, eye).reshape(L, POOL_WIDTH, POOL_WIDTH).astype(BF16)
    wo = w_out.astype(BF16)
    wr_pad = jnp.pad(w_router, ((0, 0), (0, 0), (0, LANES - E)))
    wr_hi = wr_pad.astype(BF16)
    wr_lo = (wr_pad - wr_hi.astype(F32)).astype(BF16)
    wr = jnp.stack([wr_hi, wr_lo], axis=1)
    br_pad = jnp.pad(b_router, ((0, 0), (0, LANES - E)), constant_values=NEG_BIG)
    F2 = w1.shape[-1]
    w1g, w1l = _w1_split(w1.reshape(L * E, D, F2), min(256, D))
    w1g = w1g.reshape(L, E, D, F2 // 2)
    w1l = w1l.reshape(L, E, D, F2 // 2)
    w2b = w2.astype(BF16)
    b1g = b1[..., 0::2][:, :, None, :]
    b1l = b1[..., 1::2][:, :, None, :]
    b2r = b2[:, :, None, :]

    tri_seq = _block_tri(ts, ts)
    tri64 = _block_tri(HG_CHUNK, HG_CHUNK)
    li = jnp.arange(LANES)
    head_mean = ((li[:, None] // HG_DK == li[None, :] // HG_DK) / HG_DK).astype(BF16)
    tri_tok = _block_tri(tm, tm, strict=True)

    M = T * TOP_K
    R = -(-M // bm) * bm + E * bm
    NB = R // bm

    h = x.reshape(T, D)
    for l in range(L):
        a2, b2d = _inproj(h, norm_mix[l][None], wa[l], wb[l], tm)
        a3 = a2.reshape(B, S, A_COLS)
        b3 = b2d.reshape(B, S, B_COLS)
        ka, stats = _foxprep(a3, b3, bf_pad[l][None], tri_seq, ts)
        o_fox = _fox(_fox_first_tiles(stats), a3, ka, fox_gain[l][None], ts, min(32, ts))
        o_hg = _hgrn(a3, b3, lbs[l][None], hgrn_gain[l][None], tri64, head_mean, ts)
        o_pool = _pool(a3, pool_bd[l], pool_scale[l][None], ts)
        hn, ri, rg, cnt = _router(o_fox.reshape(T, FOX_WIDTH), o_hg.reshape(T, HG_WIDTH),
                                  o_pool.reshape(T, POOL_WIDTH), h, wo[l], norm_moe[l][None],
                                  wr[l], br_pad[l][None], tri_tok, tm)
        counts = cnt[0, :E].astype(I32)
        padded = (counts + bm - 1) // bm * bm
        pend = jnp.cumsum(padded)
        pstart = pend - padded
        first_row = jnp.sum(jnp.where(ri[:, 0:TOP_K, None] == jnp.arange(E, dtype=I32), pstart.astype(I32), 0), axis=-1)
        dst = first_row + ri[:, TOP_K:2 * TOP_K]
        n_used = (pend[-1] // bm).astype(I32)
        inv = jnp.full((R,), -1, I32).at[dst.reshape(-1)].set(jnp.arange(M, dtype=I32), unique_indices=True)
        n_used = n_used + jnp.minimum(jnp.max(inv), 0)
        tile_ids = jnp.minimum(jnp.arange(NB, dtype=I32), n_used - 1)
        tile_e = jnp.minimum(jnp.sum(pend[None, :] <= (tile_ids * bm)[:, None], axis=1), E - 1).astype(I32)
        dst3 = dst.reshape(T // tt, 1, tt * TOP_K)
        pad0 = jnp.concatenate([pstart + counts, pend[-1:]]).astype(I32)
        padn = jnp.concatenate([padded - counts, R - pend[-1:]]).astype(I32)
        buf = _dispatch(pad0, padn, dst3, hn, norm_moe[l][None], R, tt, bm)
        ybuf = _experts(tile_e, n_used.reshape(1), buf, w1g[l], w1l[l], w2b[l], b1g[l], b1l[l], b2r[l], bm)
        h = _combine(dst3, hn, rg, norm_final[None], ybuf, tt, final_norm=(l == L - 1))
    return h.reshape(B, S, D)
```

```python
import functools

import jax
import jax.numpy as jnp
from jax import lax
from jax.experimental import pallas as pl
from jax.experimental.pallas import tpu as pltpu

F32 = jnp.float32
BF16 = jnp.bfloat16
I32 = jnp.int32

LANES = 128
SUBLANES = 8
VMEM_LIMIT = 56 * 1024 * 1024
EXPERT_VMEM_LIMIT = 60 * 1024 * 1024

FOX_HEADS = 8
FOX_HEAD_DIM = 64
FOX_WIDTH = FOX_HEADS * FOX_HEAD_DIM
HG_HEADS = 4
HG_DK = 64
HG_WIDTH = HG_HEADS * HG_DK
HG_CHUNK = 64
HG_SUB = 16
POOL_WINDOWS = (2, 4, 8, 16)
POOL_GROUP_DIM = 64
POOL_WIDTH = len(POOL_WINDOWS) * POOL_GROUP_DIM
POOL_HALO = 16
N_EXPERTS = 32
TOP_K = 4
SWIGLU_LIMIT = 7.0
SWIGLU_ALPHA = 1.702
RMS_EPS = 1e-5
NEG_BIG = -1e30
DMA_UNROLL = 8

HEAD_BLK = LANES
Q_COL0 = 0
K_COL0 = FOX_HEADS * HEAD_BLK
V_COL0 = 2 * FOX_HEADS * HEAD_BLK
GQ0 = 3 * FOX_HEADS * HEAD_BLK // LANES
GI0, GG0 = GQ0 + 2, GQ0 + 4
PU_BLK256 = (GQ0 + 6) // 2
A_COLS = (GQ0 + 8) * LANES
BIAS_LANE0 = FOX_HEAD_DIM
BIAS_TERMS = 3
LOG2E = 1.4426950408889634
SKIP_LOG2 = 150.0
NORM_SLACK = 1.01
B_COLS = 384


def _cparams(sem, vmem=VMEM_LIMIT):
    return pltpu.CompilerParams(dimension_semantics=sem, vmem_limit_bytes=vmem)


def _split3(x):
    x1 = x.astype(BF16)
    r1 = x - x1.astype(F32)
    x2 = r1.astype(BF16)
    x3 = (r1 - x2.astype(F32)).astype(BF16)
    return x1, x2, x3


def _tri_dot(tri, x):
    x1, x2, x3 = _split3(x)
    d = lambda a: jnp.dot(tri, a, preferred_element_type=F32)
    return d(x1) + d(x2) + d(x3)


def _log_sigmoid(z):
    return jnp.minimum(z, 0.0) - jnp.log1p(jnp.exp(-jnp.abs(z)))


def _inproj_body(x_ref, g_ref, wa_ref, wb_ref, a_ref, b_ref):
    x = x_ref[...]
    ms = jnp.mean(x * x, axis=-1, keepdims=True)
    xn = (x * lax.rsqrt(ms + RMS_EPS) * g_ref[...]).astype(BF16)
    a_ref[...] = jnp.dot(xn, wa_ref[...], preferred_element_type=F32).astype(BF16)
    b_ref[...] = jnp.dot(xn, wb_ref[...], preferred_element_type=F32)


def _inproj(h, g, wa, wb, tm):
    T, D = h.shape
    return pl.pallas_call(
        _inproj_body,
        grid=(T // tm,),
        in_specs=[
            pl.BlockSpec((tm, D), lambda i: (i, 0)),
            pl.BlockSpec((1, D), lambda i: (0, 0)),
            pl.BlockSpec((D, A_COLS), lambda i: (0, 0)),
            pl.BlockSpec((D, B_COLS), lambda i: (0, 0)),
        ],
        out_specs=[
            pl.BlockSpec((tm, A_COLS), lambda i: (i, 0)),
            pl.BlockSpec((tm, B_COLS), lambda i: (i, 0)),
        ],
        out_shape=[jax.ShapeDtypeStruct((T, A_COLS), BF16), jax.ShapeDtypeStruct((T, B_COLS), F32)],
        compiler_params=_cparams(("parallel",)),
        name="inproj",
    )(h, g, wa, wb)


def _foxprep_body(ff_ref, bf_ref, tri_ref, hsel_ref, q_ref, k_ref, ka_ref, st_ref, carry_ref, *, ts):
    @pl.when(pl.program_id(1) == 0)
    def _():
        carry_ref[...] = jnp.zeros_like(carry_ref)

    logf = _log_sigmoid(ff_ref[0] + bf_ref[...])
    c = _tri_dot(tri_ref[...], logf) + carry_ref[...]
    carry_ref[...] = c[ts - 1:ts, :]
    cl = c * LOG2E
    lane = lax.broadcasted_iota(I32, (1, LANES), 1)
    for h in range(FOX_HEADS):
        cols = slice(h * HEAD_BLK, (h + 1) * HEAD_BLK)
        terms = _split3(jnp.broadcast_to(-cl[:, h:h + 1], (ts, LANES)))
        ka = k_ref[0, :, cols]
        for i, term in enumerate(terms):
            ka = jnp.where(lane == BIAS_LANE0 + i, term, ka)
        ka_ref[0, :, cols] = ka

    def max_norm(ref):
        x = ref[0].astype(F32)
        n2 = jnp.dot((x * x).astype(BF16), hsel_ref[...], preferred_element_type=F32)
        return jnp.sqrt(jnp.max(n2, axis=0, keepdims=True)) * NORM_SLACK

    qn_row = max_norm(q_ref)
    kn_row = max_norm(k_ref)
    rowi = lax.broadcasted_iota(I32, (8, LANES), 0)
    st = jnp.where(rowi == 0, qn_row, jnp.where(rowi == 1, kn_row, jnp.where(rowi == 2, cl[0:1, :],
                   jnp.where(rowi == 3, cl[ts - 1:ts, :], 0.0))))
    st_ref[0, 0] = st


def _foxprep(a3, b3, bf_pad, tri, ts):
    B, S, _ = b3.shape
    hw = FOX_HEADS * HEAD_BLK
    head_sel = (jnp.arange(hw)[:, None] // HEAD_BLK == jnp.arange(LANES)[None, :]).astype(BF16)
    return pl.pallas_call(
        functools.partial(_foxprep_body, ts=ts),
        grid=(B, S // ts),
        in_specs=[
            pl.BlockSpec((1, ts, LANES), lambda b, s: (b, s, 2)),
            pl.BlockSpec((1, LANES), lambda b, s: (0, 0)),
            pl.BlockSpec((ts, ts), lambda b, s: (0, 0)),
            pl.BlockSpec((hw, LANES), lambda b, s: (0, 0)),
            pl.BlockSpec((1, ts, hw), lambda b, s: (b, s, Q_COL0 // hw)),
            pl.BlockSpec((1, ts, hw), lambda b, s: (b, s, K_COL0 // hw)),
        ],
        out_specs=[pl.BlockSpec((1, ts, hw), lambda b, s: (b, s, 0)),
                   pl.BlockSpec((1, 1, 8, LANES), lambda b, s: (b, s, 0, 0))],
        out_shape=[jax.ShapeDtypeStruct((B, S, hw), BF16),
                   jax.ShapeDtypeStruct((B, S // ts, 8, LANES), F32)],
        scratch_shapes=[pltpu.VMEM((1, LANES), F32)],
        compiler_params=_cparams(("parallel", "arbitrary")),
        name="foxprep",
    )(b3, bf_pad, tri, head_sel, a3, a3)


def _fox_body(js_ref, q_ref, k_ref, v_ref, g_ref, o_ref, s_ref, p_ref, m_ref, acc_ref, *, t, rc):
    b = pl.program_id(0)
    hp = pl.program_id(1)
    qi = pl.program_id(2)
    nt = pl.num_programs(2)
    lane = lax.broadcasted_iota(I32, (1, LANES), 1)
    lo = lane < FOX_HEAD_DIM
    ones_l = (lane >= BIAS_LANE0) & (lane < BIAS_LANE0 + BIAS_TERMS)
    sum_l = (lane == FOX_HEAD_DIM, lane == 0)
    qs = []
    for hh in range(2):
        qh = q_ref[0, :, hh * HEAD_BLK:(hh + 1) * HEAD_BLK]
        qs.append(jnp.where(ones_l, jnp.ones_like(qh), qh))
    m_ref[...] = jnp.full(m_ref.shape, -jnp.inf, F32)
    acc_ref[...] = jnp.zeros(acc_ref.shape, F32)
    base = (b * FOX_HEADS + 2 * hp) * nt + qi
    j0 = jnp.minimum(js_ref[base], js_ref[base + nt])
    nt_dims = (((1,), (1,)), ((), ()))
    nblk = t // LANES

    def tile(j, masked):
        start = pl.multiple_of(j * t, t)
        for hh in range(2):
            kt = k_ref[0, pl.ds(start, t), hh * HEAD_BLK:(hh + 1) * HEAD_BLK]
            s_ref[hh] = lax.dot_general(qs[hh], kt, nt_dims, preferred_element_type=F32)
        for hh in range(2):
            for r in range(t // rc):
                rows = slice(r * rc, (r + 1) * rc)
                diag_blk = (r * rc) // LANES
                mx = None
                for cb in range(diag_blk + 1 if masked else nblk):
                    cols = slice(cb * LANES, (cb + 1) * LANES)
                    sc = s_ref[hh, rows, cols]
                    if masked and cb == diag_blk:
                        ri = lax.broadcasted_iota(I32, (rc, LANES), 0) + r * rc
                        ci = lax.broadcasted_iota(I32, (rc, LANES), 1) + cb * LANES
                        sc = jnp.where(ci <= ri, sc, -jnp.inf)
                        s_ref[hh, rows, cols] = sc
                    mx = sc if mx is None else jnp.maximum(mx, sc)
                m_old = m_ref[hh, rows, :]
                m_new = jnp.maximum(m_old, jnp.broadcast_to(jnp.max(mx, axis=-1, keepdims=True), (rc, LANES)))
                m_ref[hh, rows, :] = m_new
                acc_ref[hh, rows, :] = jnp.exp2(m_old - m_new) * acc_ref[hh, rows, :]
            for r in range(t // rc):
                rows = slice(r * rc, (r + 1) * rc)
                diag_blk = (r * rc) // LANES
                m_new = m_ref[hh, rows, :]
                for cb in range(nblk):
                    cols = slice(cb * LANES, (cb + 1) * LANES)
                    if masked and cb > diag_blk:
                        p_ref[hh, rows, cols] = jnp.zeros((rc, LANES), BF16)
                    else:
                        p_ref[hh, rows, cols] = jnp.exp2(s_ref[hh, rows, cols] - m_new).astype(BF16)
            vt = v_ref[0, pl.ds(start, t), hh * HEAD_BLK:(hh + 1) * HEAD_BLK]
            vt = jnp.where(sum_l[hh], jnp.ones_like(vt), vt)
            acc_ref[hh] += jnp.dot(p_ref[hh], vt, preferred_element_type=F32)

    def body(j, carry):
        tile(j, False)
        return carry

    lax.fori_loop(j0, qi, body, 0)
    tile(qi, True)
    outs = []
    for hh in range(2):
        acc = acc_ref[hh]
        val = lo if hh == 0 else jnp.logical_not(lo)
        w = jnp.where(val, 1.0 / FOX_HEAD_DIM, jnp.where(sum_l[hh], RMS_EPS, 0.0))
        outs.append(acc * lax.rsqrt(jnp.sum(acc * acc * w, axis=-1, keepdims=True)))
    o_ref[0] = (jnp.where(lo, outs[0], outs[1]) * g_ref[...]).astype(BF16)


def _fox_first_tiles(stats):
    qn, kn, cs, ce = (stats[:, :, i, :FOX_HEADS].transpose(0, 2, 1) for i in range(4))
    bound = cs[..., :, None] - ce[..., None, :] + qn[..., :, None] * (kn[..., None, :] + kn[..., :, None])
    nt = qn.shape[-1]
    ti = jnp.arange(nt)[:, None]
    tj = jnp.arange(nt)[None, :]
    need = (tj == ti) | ((tj < ti) & (bound >= -SKIP_LOG2))
    return jnp.argmax(need, axis=-1).astype(I32).reshape(-1)


def _fox(first, a3, ka, gain, t, rc):
    B, S, _ = a3.shape
    pair = 2 * HEAD_BLK
    grid_spec = pltpu.PrefetchScalarGridSpec(
        num_scalar_prefetch=1,
        grid=(B, FOX_HEADS // 2, S // t),
        in_specs=[
            pl.BlockSpec((1, t, pair), lambda b, hp, qi, js: (b, qi, Q_COL0 // pair + hp)),
            pl.BlockSpec((1, S, pair), lambda b, hp, qi, js: (b, 0, hp)),
            pl.BlockSpec((1, S, pair), lambda b, hp, qi, js: (b, 0, V_COL0 // pair + hp)),
            pl.BlockSpec((1, LANES), lambda b, hp, qi, js: (0, hp)),
        ],
        out_specs=pl.BlockSpec((1, t, LANES), lambda b, hp, qi, js: (b, qi, hp)),
        scratch_shapes=[pltpu.VMEM((2, t, t), F32), pltpu.VMEM((2, t, t), BF16),
                        pltpu.VMEM((2, t, LANES), F32), pltpu.VMEM((2, t, LANES), F32)],
    )
    return pl.pallas_call(
        functools.partial(_fox_body, t=t, rc=rc),
        grid_spec=grid_spec,
        out_shape=jax.ShapeDtypeStruct((B, S, FOX_WIDTH), BF16),
        compiler_params=_cparams(("parallel", "parallel", "arbitrary")),
        name="fox_attention",
    )(first, a3, ka, a3, gain)


def _hgrn_body(q_ref, f_ref, v_ref, gg_ref, lb_ref, gain_ref, t64_ref, hm_ref, o_ref, st_ref, o_acc, *, tc):
    @pl.when(pl.program_id(2) == 0)
    def _():
        st_ref[...] = jnp.zeros_like(st_ref)

    C, SB = HG_CHUNK, HG_SUB
    lane = lax.broadcasted_iota(I32, (1, LANES), 1)
    lo = lane < HG_DK
    lb = lb_ref[...]
    fl = f_ref[0]
    t2 = jnp.log1p(-lb) + _log_sigmoid(fl)
    t1 = jnp.where(lb > 0.0, jnp.log(jnp.maximum(lb, 1e-37)), NEG_BIG)
    logf = jnp.maximum(t1, t2) + jnp.log1p(jnp.exp(-jnp.abs(t1 - t2)))
    kk = (1.0 - lb) / (1.0 + jnp.exp(fl))
    q = q_ref[0].astype(F32)
    nch = tc // C
    wide = jnp.concatenate([logf[ch * C:(ch + 1) * C] for ch in range(nch)], axis=1)
    a_wide = _tri_dot(t64_ref[...], wide)
    a = jnp.concatenate([a_wide[:, ch * LANES:(ch + 1) * LANES] for ch in range(nch)], axis=0)
    qa = (q * jnp.exp(a)).astype(BF16)

    rowblk = lax.broadcasted_iota(I32, (C, 1), 0) // SB
    r64 = lax.broadcasted_iota(I32, (C, C), 0)
    c64 = lax.broadcasted_iota(I32, (C, C), 1)
    causal = c64 <= r64
    rr = lax.broadcasted_iota(I32, (LANES, LANES), 0)
    cc = lax.broadcasted_iota(I32, (LANES, LANES), 1)
    same_head = (rr < HG_DK) == (cc < HG_DK)
    nt = (((1,), (1,)), ((), ()))
    tn = (((0,), (0,)), ((), ()))

    st = st_ref[...]
    for ch in range(tc // C):
        sl = slice(ch * C, (ch + 1) * C)
        a_c, kk_c, v_c = a[sl], kk[sl], v_ref[0, sl, :]
        a_last = a_c[C - 1:C, :]
        rows0, rows1 = [], []
        for blk in range(C // SB):
            b0 = ch * C + blk * SB
            e_blk = a_c[blk * SB - 1:blk * SB, :] if blk else jnp.zeros((1, LANES), F32)
            live = rowblk <= blk
            k_blk = jnp.where(live, kk_c * jnp.exp(jnp.where(live, e_blk - a_c, 0.0)), 0.0).astype(BF16)
            q_blk = q[b0:b0 + SB, :] * jnp.exp(a_c[blk * SB:(blk + 1) * SB, :] - e_blk)
            lhs = jnp.concatenate([jnp.where(lo, q_blk, 0.0), jnp.where(lo, 0.0, q_blk)], axis=0).astype(BF16)
            sc = lax.dot_general(lhs, k_blk, nt, preferred_element_type=F32)
            rows0.append(sc[0:SB])
            rows1.append(sc[SB:2 * SB])
        s0 = jnp.where(causal, jnp.concatenate(rows0, axis=0), 0.0).astype(BF16)
        s1 = jnp.where(causal, jnp.concatenate(rows1, axis=0), 0.0).astype(BF16)
        intra = jnp.where(lo, jnp.dot(s0, v_c, preferred_element_type=F32),
                          jnp.dot(s1, v_c, preferred_element_type=F32))
        inter = lax.dot_general(qa[sl], st.astype(BF16), nt, preferred_element_type=F32)
        o = inter + intra
        kd = (kk_c * jnp.exp(a_last - a_c)).astype(BF16)
        upd = lax.dot_general(v_c, kd, tn, preferred_element_type=F32)
        st = jnp.where(same_head, st * jnp.exp(a_last) + upd, 0.0)
        o_acc[sl, :] = o
    st_ref[...] = st

    o = o_acc[...]
    sq = o * o
    hi = sq.astype(BF16)
    lo_t = (sq - hi.astype(F32)).astype(BF16)
    ms = (jnp.dot(hi, hm_ref[...], preferred_element_type=F32)
          + jnp.dot(lo_t, hm_ref[...], preferred_element_type=F32))
    g = gg_ref[0].astype(F32)
    silu = g / (1.0 + jnp.exp(-g))
    o_ref[0] = (o * lax.rsqrt(ms + RMS_EPS) * gain_ref[...] * silu).astype(BF16)


def _hgrn(a3, b3, lb, gain, tri, head_mean, tc):
    B, S, _ = a3.shape
    blk = lambda c0: pl.BlockSpec((1, tc, LANES), lambda b, hp, s: (b, s, c0 + hp))
    row = pl.BlockSpec((1, LANES), lambda b, hp, s: (0, hp))
    const = lambda n: pl.BlockSpec((n, n), lambda b, hp, s: (0, 0))
    return pl.pallas_call(
        functools.partial(_hgrn_body, tc=tc),
        grid=(B, HG_HEADS // 2, S // tc),
        in_specs=[blk(GQ0), blk(0), blk(GI0), blk(GG0), row, row, const(HG_CHUNK), const(LANES)],
        out_specs=pl.BlockSpec((1, tc, LANES), lambda b, hp, s: (b, s, hp)),
        out_shape=jax.ShapeDtypeStruct((B, S, HG_WIDTH), BF16),
        scratch_shapes=[pltpu.VMEM((LANES, LANES), F32), pltpu.VMEM((tc, LANES), F32)],
        compiler_params=_cparams(("parallel", "parallel", "arbitrary")),
        name="hgrn2",
    )(a3, b3, a3, a3, lb, gain, tri, head_mean)


def _pool_body(u_ref, w_ref, sc_ref, o_ref, ext_ref, *, tc):
    si = pl.program_id(1)

    @pl.when(si == 0)
    def _():
        ext_ref[0:POOL_HALO, :] = jnp.zeros((POOL_HALO, POOL_WIDTH), F32)

    u = u_ref[0].astype(F32)
    ext_ref[POOL_HALO:POOL_HALO + tc, :] = u
    acc = u
    sums = {}
    for j in range(1, max(POOL_WINDOWS)):
        acc = acc + ext_ref[POOL_HALO - j:POOL_HALO - j + tc, :]
        if j + 1 in POOL_WINDOWS:
            sums[j + 1] = acc
    lane = lax.broadcasted_iota(I32, (1, POOL_WIDTH), 1)
    grp = lane // POOL_GROUP_DIM
    win = sums[POOL_WINDOWS[-1]]
    wlen = jnp.full((1, POOL_WIDTH), float(POOL_WINDOWS[-1]), F32)
    for gi in range(len(POOL_WINDOWS) - 2, -1, -1):
        win = jnp.where(grp == gi, sums[POOL_WINDOWS[gi]], win)
        wlen = jnp.where(grp == gi, float(POOL_WINDOWS[gi]), wlen)
    pos = (si * tc + lax.broadcasted_iota(I32, (tc, 1), 0) + 1).astype(F32)
    pooled = win / jnp.minimum(pos, wlen) - u
    y = jnp.dot(pooled.astype(BF16), w_ref[...], preferred_element_type=F32) * sc_ref[...]
    o_ref[0] = y.astype(BF16)
    ext_ref[0:POOL_HALO, :] = u[tc - POOL_HALO:tc, :]


def _pool(a3, wbd, scale, tc):
    B, S, _ = a3.shape
    return pl.pallas_call(
        functools.partial(_pool_body, tc=tc),
        grid=(B, S // tc),
        in_specs=[
            pl.BlockSpec((1, tc, POOL_WIDTH), lambda b, s: (b, s, PU_BLK256)),
            pl.BlockSpec((POOL_WIDTH, POOL_WIDTH), lambda b, s: (0, 0)),
            pl.BlockSpec((1, POOL_WIDTH), lambda b, s: (0, 0)),
        ],
        out_specs=pl.BlockSpec((1, tc, POOL_WIDTH), lambda b, s: (b, s, 0)),
        out_shape=jax.ShapeDtypeStruct((B, S, POOL_WIDTH), BF16),
        scratch_shapes=[pltpu.VMEM((POOL_HALO + tc, POOL_WIDTH), F32)],
        compiler_params=_cparams(("parallel", "arbitrary")),
        name="pool_mixer",
    )(a3, wbd, scale)


def _router_body(of_ref, oh_ref, op_ref, h_ref, wo_ref, g_ref, wr_ref, br_ref, tri_ref,
                 hn_ref, ri_ref, rg_ref, cnt_ref, carry_ref, *, tm):
    @pl.when(pl.program_id(0) == 0)
    def _():
        carry_ref[...] = jnp.zeros_like(carry_ref)

    f0, f1 = FOX_WIDTH, FOX_WIDTH + HG_WIDTH
    mix = (jnp.dot(of_ref[...], wo_ref[0:f0, :], preferred_element_type=F32)
           + jnp.dot(oh_ref[...], wo_ref[f0:f1, :], preferred_element_type=F32)
           + jnp.dot(op_ref[...], wo_ref[f1:f1 + POOL_WIDTH, :], preferred_element_type=F32))
    hn = h_ref[...] + mix
    hn_ref[...] = hn
    ms = jnp.mean(hn * hn, axis=-1, keepdims=True)
    xn = hn * lax.rsqrt(ms + RMS_EPS) * g_ref[...]
    x1 = xn.astype(BF16)
    x2 = (xn - x1.astype(F32)).astype(BF16)
    d = lambda a, b: jnp.dot(a, b, preferred_element_type=F32)
    logits = d(x1, wr_ref[0]) + d(x1, wr_ref[1]) + d(x2, wr_ref[0]) + br_ref[...]

    lanef = lax.broadcasted_iota(I32, (tm, LANES), 1).astype(F32)
    l = logits
    vals, idxs, sels = [], [], []
    for _ in range(TOP_K):
        mk = jnp.max(l, axis=-1, keepdims=True)
        ik = jnp.min(jnp.where(l == mk, lanef, float(LANES)), axis=-1, keepdims=True)
        sk = lanef == ik
        vals.append(mk)
        idxs.append(ik)
        sels.append(sk)
        l = jnp.where(sk, -jnp.inf, l)
    es = [jnp.exp(v - vals[0]) for v in vals]
    den = es[0] + es[1] + es[2] + es[3]
    sel = jnp.zeros((tm, LANES), F32)
    for sk in sels:
        sel = jnp.where(sk, 1.0, sel)
    rank = jnp.dot(tri_ref[...], sel.astype(BF16), preferred_element_type=F32) + carry_ref[...]
    carry = carry_ref[...] + jnp.sum(sel, axis=0, keepdims=True)
    carry_ref[...] = carry
    cnt_ref[...] = carry
    ri = jnp.zeros((tm, LANES), F32)
    rg = jnp.zeros((tm, LANES), F32)
    for k in range(TOP_K):
        pos_k = jnp.sum(jnp.where(sels[k], rank, 0.0), axis=-1, keepdims=True)
        ri = jnp.where(lanef == float(k), idxs[k], ri)
        ri = jnp.where(lanef == float(TOP_K + k), pos_k, ri)
        rg = jnp.where(lanef == float(k), es[k] / den, rg)
    ri_ref[...] = ri.astype(I32)
    rg_ref[...] = rg


def _router(o_fox, o_hg, o_pool, h, wo, g, wr, br, tri, tm):
    T, D = h.shape
    rowblk = lambda w: pl.BlockSpec((tm, w), lambda i: (i, 0))
    const = lambda shp: pl.BlockSpec(shp, lambda i: tuple(0 for _ in shp))
    return pl.pallas_call(
        functools.partial(_router_body, tm=tm),
        grid=(T // tm,),
        in_specs=[rowblk(FOX_WIDTH), rowblk(HG_WIDTH), rowblk(POOL_WIDTH), rowblk(D),
                  const((D, D)), const((1, D)), const((2, D, LANES)), const((1, LANES)), const((tm, tm))],
        out_specs=[rowblk(D), rowblk(LANES), rowblk(LANES), const((1, LANES))],
        out_shape=[jax.ShapeDtypeStruct((T, D), F32), jax.ShapeDtypeStruct((T, LANES), I32),
                   jax.ShapeDtypeStruct((T, LANES), F32), jax.ShapeDtypeStruct((1, LANES), F32)],
        scratch_shapes=[pltpu.VMEM((1, LANES), F32)],
        compiler_params=_cparams(("arbitrary",)),
        name="outproj_router",
    )(o_fox, o_hg, o_pool, h, wo, g, wr, br, tri)


def _row_copy(src_ref, src_row, dst_ref, dst_row, sem):
    return pltpu.make_async_copy(src_ref.at[pl.ds(src_row, 1)], dst_ref.at[pl.ds(dst_row, 1)], sem)


def _wait_rows(tile_ref, sem, times):
    for _ in range(times):
        pltpu.make_async_copy(tile_ref, tile_ref, sem).wait()


def _dispatch_body(pad0_ref, padn_ref, dst_ref, h_ref, g_ref, buf_ref, xn_ref, zero_ref, sem, zsem, *, tt, bm):
    i = pl.program_id(0)
    n = pl.num_programs(0)
    slot = i % 2
    stage = xn_ref.at[slot]

    @pl.when(i >= 2)
    def _():
        _wait_rows(stage, sem.at[slot], TOP_K)

    x = h_ref[...]
    ms = jnp.mean(x * x, axis=-1, keepdims=True)
    stage[...] = x * lax.rsqrt(ms + RMS_EPS) * g_ref[...]

    def issue(r, _):
        for k in range(TOP_K):
            _row_copy(stage, r, buf_ref, dst_ref[0, 0, r * TOP_K + k], sem.at[slot]).start(priority=k % 2)
        return 0

    lax.fori_loop(0, tt, issue, 0, unroll=DMA_UNROLL)

    @pl.when(i == 0)
    def _():
        zero_ref[...] = jnp.zeros_like(zero_ref)
        half = bm // 2
        n_exp = pad0_ref.shape[0] - 1

        def run(cp, pred, wait):
            @pl.when(pred)
            def _():
                cp.wait() if wait else cp.start()

        def pad_copies(e, wait):
            off, cnt = pad0_ref[e], padn_ref[e]
            for r in range(SUBLANES - 1):
                run(_row_copy(zero_ref, 0, buf_ref, off + r, zsem), r < (cnt & (SUBLANES - 1)), wait)
            off = off + (cnt & (SUBLANES - 1))
            sz = SUBLANES
            while sz <= half:
                take = (cnt & sz) != 0
                dst = buf_ref.at[pl.ds(pl.multiple_of(off, SUBLANES), sz)]
                run(pltpu.make_async_copy(zero_ref.at[pl.ds(0, sz)], dst, zsem), take, wait)
                off = off + jnp.where(take, sz, 0)
                sz *= 2
            return 0

        def tail_copy(c, wait):
            dst = buf_ref.at[pl.ds(pl.multiple_of(pad0_ref[n_exp] + c * half, SUBLANES), half)]
            cp = pltpu.make_async_copy(zero_ref, dst, zsem)
            cp.wait() if wait else cp.start()
            return 0

        n_tail = padn_ref[n_exp] // half
        for wait in (False, True):
            lax.fori_loop(0, n_exp, lambda e, c, w=wait: pad_copies(e, w), 0)
            lax.fori_loop(0, n_tail, lambda c, _, w=wait: tail_copy(c, w), 0)

    @pl.when(i == n - 1)
    def _():
        _wait_rows(stage, sem.at[slot], TOP_K)

    @pl.when((i == n - 1) & (n >= 2))
    def _():
        _wait_rows(xn_ref.at[1 - slot], sem.at[1 - slot], TOP_K)


def _dispatch(pad0, padn, dst3, hn, g, R, tt, bm):
    T, D = hn.shape
    grid_spec = pltpu.PrefetchScalarGridSpec(
        num_scalar_prefetch=2,
        grid=(T // tt,),
        in_specs=[
            pl.BlockSpec((1, 1, tt * TOP_K), lambda i, p0, pn: (i, 0, 0), memory_space=pltpu.SMEM),
            pl.BlockSpec((tt, D), lambda i, p0, pn: (i, 0)),
            pl.BlockSpec((1, D), lambda i, p0, pn: (0, 0)),
        ],
        out_specs=pl.BlockSpec(memory_space=pl.ANY),
        scratch_shapes=[pltpu.VMEM((2, tt, D), F32), pltpu.VMEM((bm // 2, D), F32),
                        pltpu.SemaphoreType.DMA((2,)), pltpu.SemaphoreType.DMA],
    )
    return pl.pallas_call(
        functools.partial(_dispatch_body, tt=tt, bm=bm),
        grid_spec=grid_spec,
        out_shape=jax.ShapeDtypeStruct((R, D), F32),
        compiler_params=_cparams(("arbitrary",)),
        name="moe_dispatch",
    )(pad0, padn, dst3, hn, g)


MXU_DIM = 256


def _expert_body(te_ref, nu_ref, x_ref, w1_ref, w2_ref, perm_ref, b1g_ref, b1l_ref, b2_ref, y_ref,
                 w1g_scr, w1l_scr, w2_scr):
    i = pl.program_id(0)
    live = i < nu_ref[0]
    prev = te_ref[jnp.maximum(i - 1, 0)]

    @pl.when(live & ((i == 0) | (te_ref[i] != prev)))
    def _():
        half = MXU_DIM // 2
        for grp in range(w1_ref.shape[-1] // MXU_DIM):
            w = w1_ref[0, :, grp * MXU_DIM:(grp + 1) * MXU_DIM].astype(BF16)
            r = jnp.dot(w, perm_ref[...], preferred_element_type=F32).astype(BF16)
            w1g_scr[:, grp * half:(grp + 1) * half] = r[:, :half]
            w1l_scr[:, grp * half:(grp + 1) * half] = r[:, half:]
        w2_scr[...] = w2_ref[0].astype(BF16)

    @pl.when(live)
    def _():
        x = x_ref[...].astype(BF16)
        glu = jnp.dot(x, w1g_scr[...], preferred_element_type=F32) + b1g_ref[0]
        lin = jnp.dot(x, w1l_scr[...], preferred_element_type=F32) + b1l_ref[0]
        glu = jnp.minimum(glu, SWIGLU_LIMIT)
        lin = jnp.clip(lin, -SWIGLU_LIMIT, SWIGLU_LIMIT)
        act = glu / (1.0 + jnp.exp(-SWIGLU_ALPHA * glu)) * (lin + 1.0)
        y_ref[...] = jnp.dot(act.astype(BF16), w2_scr[...], preferred_element_type=F32) + b2_ref[0]

    @pl.when(jnp.logical_not(live))
    def _():
        y_ref[...] = jnp.zeros_like(y_ref)


def _experts(tile_e, n_used, buf, w1, w2, b1g, b1l, b2, bm):
    R, D = buf.shape
    F2 = w1.shape[-1]
    F = F2 // 2
    half = MXU_DIM // 2
    pi = jnp.arange(MXU_DIM)[:, None]
    pj = jnp.arange(MXU_DIM)[None, :]
    perm = jnp.where(pj < half, pi == 2 * pj, pi == 2 * (pj - half) + 1).astype(BF16)
    rows = pl.BlockSpec((bm, D), lambda i, te, nu: (jnp.minimum(i, nu[0] - 1), 0))
    wspec = lambda a, b: pl.BlockSpec((1, a, b), lambda i, te, nu: (te[i], 0, 0))
    grid_spec = pltpu.PrefetchScalarGridSpec(
        num_scalar_prefetch=2,
        grid=(R // bm,),
        in_specs=[rows, wspec(D, F2), wspec(F, D), pl.BlockSpec((MXU_DIM, MXU_DIM), lambda i, te, nu: (0, 0)),
                  wspec(1, F), wspec(1, F), wspec(1, D)],
        out_specs=pl.BlockSpec((bm, D), lambda i, te, nu: (i, 0)),
        scratch_shapes=[pltpu.VMEM((D, F), BF16), pltpu.VMEM((D, F), BF16), pltpu.VMEM((F, D), BF16)],
    )
    return pl.pallas_call(
        _expert_body,
        grid_spec=grid_spec,
        out_shape=jax.ShapeDtypeStruct((R, D), F32),
        compiler_params=_cparams(("arbitrary",), vmem=EXPERT_VMEM_LIMIT),
        name="moe_experts",
    )(tile_e, n_used, buf, w1, w2, perm, b1g, b1l, b2)


def _combine_body(dst_ref, h_ref, rg_ref, gf_ref, y_ref, o_ref, rows_ref, sem, *, tt, final_norm):
    s = pl.program_id(0)
    n_tiles = pl.num_programs(0) - 1

    @pl.when(s < n_tiles)
    def _():
        slot = s % 2

        def issue(r, _):
            for k in range(TOP_K):
                _row_copy(y_ref, dst_ref[0, 0, r * TOP_K + k], rows_ref.at[slot, k], r,
                          sem.at[slot]).start(priority=k % 2)
            return 0

        lax.fori_loop(0, tt, issue, 0, unroll=DMA_UNROLL)

    @pl.when(s >= 1)
    def _():
        slot = (s - 1) % 2
        _wait_rows(rows_ref.at[slot, 0], sem.at[slot], TOP_K)
        out = h_ref[...]
        gates = rg_ref[...]
        for k in range(TOP_K):
            out = out + gates[:, k:k + 1] * rows_ref[slot, k]
        if final_norm:
            ms = jnp.mean(out * out, axis=-1, keepdims=True)
            out = out * lax.rsqrt(ms + RMS_EPS) * gf_ref[...]
        o_ref[...] = out


def _combine(dst3, hn, rg, g_final, ybuf, tt, final_norm):
    T, D = hn.shape
    n_tiles = T // tt
    prev = lambda s: (jnp.maximum(s - 1, 0), 0)
    return pl.pallas_call(
        functools.partial(_combine_body, tt=tt, final_norm=final_norm),
        grid=(n_tiles + 1,),
        in_specs=[
            pl.BlockSpec((1, 1, tt * TOP_K), lambda s: (jnp.minimum(s, n_tiles - 1), 0, 0), memory_space=pltpu.SMEM),
            pl.BlockSpec((tt, D), prev),
            pl.BlockSpec((tt, LANES), prev),
            pl.BlockSpec((1, D), lambda s: (0, 0)),
            pl.BlockSpec(memory_space=pl.ANY),
        ],
        out_specs=pl.BlockSpec((tt, D), prev),
        out_shape=jax.ShapeDtypeStruct((T, D), F32),
        scratch_shapes=[pltpu.VMEM((2, TOP_K, tt, D), F32), pltpu.SemaphoreType.DMA((2,))],
        compiler_params=_cparams(("arbitrary",)),
        name="moe_combine",
    )(dst3, hn, rg, g_final, ybuf)


def _tiles(B, S):
    T = B * S
    ts = min(512, S)
    tm = min(512, T)
    tt = min(256, T)
    bm = min(512, T)
    return ts, tm, tt, bm


def _block_tri(n, blk, strict=False):
    r = jnp.arange(n)[:, None]
    c = jnp.arange(n)[None, :]
    keep = (c < r) if strict else (c <= r)
    return (keep & (r // blk == c // blk)).astype(BF16)


def kernel(x, norm_mix, w_in, b_fox_f, fox_gain, hgrn_lb, hgrn_gain, pool_w, pool_scale, w_out,
           norm_moe, w_router, b_router, w1, b1, w2, b2, norm_final):
    B, S, D = x.shape
    L = w_in.shape[0]
    T = B * S
    E = w1.shape[1]
    ts, tm, tt, bm = _tiles(B, S)

    offs = [0]
    for wdt in (FOX_WIDTH, FOX_WIDTH, FOX_WIDTH, FOX_HEADS, HG_WIDTH, HG_WIDTH, HG_WIDTH, HG_WIDTH, POOL_WIDTH):
        offs.append(offs[-1] + wdt)
    col = lambda i: w_in[:, :, offs[i]:offs[i + 1]]
    def head_blocks(w):
        w = w.reshape(L, D, FOX_HEADS, FOX_HEAD_DIM)
        w = jnp.pad(w, ((0, 0), (0, 0), (0, 0), (0, HEAD_BLK - FOX_HEAD_DIM)))
        return w.reshape(L, D, FOX_HEADS * HEAD_BLK)

    def head_blocks_v(w):
        w = w.reshape(L, D, FOX_HEADS, FOX_HEAD_DIM)
        odd = (jnp.arange(FOX_HEADS) % 2 == 1)[:, None]
        w = jnp.concatenate([jnp.where(odd, 0.0, w), jnp.where(odd, w, 0.0)], axis=-1)
        return w.reshape(L, D, FOX_HEADS * HEAD_BLK)

    scale = FOX_HEAD_DIM ** -0.5 * LOG2E
    wa = jnp.concatenate([head_blocks(col(0) * scale), head_blocks(col(1)), head_blocks_v(col(2)),
                          col(4), col(6), col(7), col(8)],
                         axis=-1).astype(BF16)
    wb = jnp.concatenate([col(5), col(3), jnp.zeros((L, D, LANES - FOX_HEADS), F32)], axis=-1).astype(BF16)
    bf_pad = jnp.pad(b_fox_f, ((0, 0), (0, LANES - FOX_HEADS)))
    p = jax.nn.softmax(hgrn_lb.astype(F32), axis=0)
    cs = jnp.cumsum(p, axis=0)
    lbs = cs - cs[0:1]
    eye = jnp.eye(len(POOL_WINDOWS), dtype=F32)
    pool_bd = jnp.einsum("lgcd,gh->lgchd", pool_w, eye).reshape(L, POOL_WIDTH, POOL_WIDTH).astype(BF16)
    wo = w_out.astype(BF16)
    wr_pad = jnp.pad(w_router, ((0, 0), (0, 0), (0, LANES - E)))
    wr_hi = wr_pad.astype(BF16)
    wr_lo = (wr_pad - wr_hi.astype(F32)).astype(BF16)
    wr = jnp.stack([wr_hi, wr_lo], axis=1)
    br_pad = jnp.pad(b_router, ((0, 0), (0, LANES - E)), constant_values=NEG_BIG)
    b1g = b1[..., 0::2][:, :, None, :]
    b1l = b1[..., 1::2][:, :, None, :]
    b2r = b2[:, :, None, :]

    tri_seq = _block_tri(ts, ts)
    tri64 = _block_tri(HG_CHUNK, HG_CHUNK)
    li = jnp.arange(LANES)
    head_mean = ((li[:, None] // HG_DK == li[None, :] // HG_DK) / HG_DK).astype(BF16)
    tri_tok = _block_tri(tm, tm, strict=True)

    M = T * TOP_K
    R = -(-M // bm) * bm + E * bm
    NB = R // bm

    h = x.reshape(T, D)
    for l in range(L):
        a2, b2d = _inproj(h, norm_mix[l][None], wa[l], wb[l], tm)
        a3 = a2.reshape(B, S, A_COLS)
        b3 = b2d.reshape(B, S, B_COLS)
        ka, stats = _foxprep(a3, b3, bf_pad[l][None], tri_seq, ts)
        o_fox = _fox(_fox_first_tiles(stats), a3, ka, fox_gain[l][None], ts, min(32, ts))
        o_hg = _hgrn(a3, b3, lbs[l][None], hgrn_gain[l][None], tri64, head_mean, ts)
        o_pool = _pool(a3, pool_bd[l], pool_scale[l][None], ts)
        hn, ri, rg, cnt = _router(o_fox.reshape(T, FOX_WIDTH), o_hg.reshape(T, HG_WIDTH),
                                  o_pool.reshape(T, POOL_WIDTH), h, wo[l], norm_moe[l][None],
                                  wr[l], br_pad[l][None], tri_tok, tm)
        counts = cnt[0, :E].astype(I32)
        padded = (counts + bm - 1) // bm * bm
        pend = jnp.cumsum(padded)
        pstart = pend - padded
        first_row = jnp.sum(jnp.where(ri[:, 0:TOP_K, None] == jnp.arange(E, dtype=I32), pstart.astype(I32), 0), axis=-1)
        dst = first_row + ri[:, TOP_K:2 * TOP_K]
        n_used = (pend[-1] // bm).astype(I32)
        tile_ids = jnp.minimum(jnp.arange(NB, dtype=I32), n_used - 1)
        tile_e = jnp.minimum(jnp.sum(pend[None, :] <= (tile_ids * bm)[:, None], axis=1), E - 1).astype(I32)
        dst3 = dst.reshape(T // tt, 1, tt * TOP_K)
        pad0 = jnp.concatenate([pstart + counts, pend[-1:]]).astype(I32)
        padn = jnp.concatenate([padded - counts, R - pend[-1:]]).astype(I32)
        buf = _dispatch(pad0, padn, dst3, hn, norm_moe[l][None], R, tt, bm)
        ybuf = _experts(tile_e, n_used.reshape(1), buf, w1[l], w2[l], b1g[l], b1l[l], b2r[l], bm)
        h = _combine(dst3, hn, rg, norm_final[None], ybuf, tt, final_norm=(l == L - 1))
    return h.reshape(B, S, D)
```

```python
import functools

import jax
import jax.numpy as jnp
from jax import lax
from jax.experimental import pallas as pl
from jax.experimental.pallas import tpu as pltpu

F32 = jnp.float32
BF16 = jnp.bfloat16
I32 = jnp.int32

LANES = 128
SUBLANES = 8
VMEM_LIMIT = 56 * 1024 * 1024
EXPERT_VMEM_LIMIT = 60 * 1024 * 1024

FOX_HEADS = 8
FOX_HEAD_DIM = 64
FOX_WIDTH = FOX_HEADS * FOX_HEAD_DIM
HG_HEADS = 4
HG_DK = 64
HG_WIDTH = HG_HEADS * HG_DK
HG_CHUNK = 64
HG_SUB = 16
POOL_WINDOWS = (2, 4, 8, 16)
POOL_GROUP_DIM = 64
POOL_WIDTH = len(POOL_WINDOWS) * POOL_GROUP_DIM
POOL_HALO = 16
N_EXPERTS = 32
TOP_K = 4
SWIGLU_LIMIT = 7.0
SWIGLU_ALPHA = 1.702
RMS_EPS = 1e-5
NEG_BIG = -1e30
DMA_UNROLL = 8

HEAD_BLK = LANES
Q_COL0 = 0
K_COL0 = FOX_HEADS * HEAD_BLK
V_COL0 = 2 * FOX_HEADS * HEAD_BLK
GQ0 = 3 * FOX_HEADS * HEAD_BLK // LANES
GI0, GG0 = GQ0 + 2, GQ0 + 4
PU_BLK256 = (GQ0 + 6) // 2
A_COLS = (GQ0 + 8) * LANES
BIAS_LANE0 = FOX_HEAD_DIM
BIAS_TERMS = 3
LOG2E = 1.4426950408889634
SKIP_LOG2 = 150.0
NORM_SLACK = 1.01
B_COLS = 384


def _cparams(sem, vmem=VMEM_LIMIT):
    return pltpu.CompilerParams(dimension_semantics=sem, vmem_limit_bytes=vmem)


def _split3(x):
    x1 = x.astype(BF16)
    r1 = x - x1.astype(F32)
    x2 = r1.astype(BF16)
    x3 = (r1 - x2.astype(F32)).astype(BF16)
    return x1, x2, x3


def _tri_dot(tri, x):
    x1, x2, x3 = _split3(x)
    d = lambda a: jnp.dot(tri, a, preferred_element_type=F32)
    return d(x1) + d(x2) + d(x3)


def _log_sigmoid(z):
    return jnp.minimum(z, 0.0) - jnp.log1p(jnp.exp(-jnp.abs(z)))


def _inproj_body(x_ref, g_ref, wa_ref, wb_ref, a_ref, b_ref):
    x = x_ref[...]
    ms = jnp.mean(x * x, axis=-1, keepdims=True)
    xn = (x * lax.rsqrt(ms + RMS_EPS) * g_ref[...]).astype(BF16)
    a_ref[...] = jnp.dot(xn, wa_ref[...], preferred_element_type=F32).astype(BF16)
    b_ref[...] = jnp.dot(xn, wb_ref[...], preferred_element_type=F32)


def _inproj(h, g, wa, wb, tm):
    T, D = h.shape
    return pl.pallas_call(
        _inproj_body,
        grid=(T // tm,),
        in_specs=[
            pl.BlockSpec((tm, D), lambda i: (i, 0)),
            pl.BlockSpec((1, D), lambda i: (0, 0)),
            pl.BlockSpec((D, A_COLS), lambda i: (0, 0)),
            pl.BlockSpec((D, B_COLS), lambda i: (0, 0)),
        ],
        out_specs=[
            pl.BlockSpec((tm, A_COLS), lambda i: (i, 0)),
            pl.BlockSpec((tm, B_COLS), lambda i: (i, 0)),
        ],
        out_shape=[jax.ShapeDtypeStruct((T, A_COLS), BF16), jax.ShapeDtypeStruct((T, B_COLS), F32)],
        compiler_params=_cparams(("parallel",)),
        name="inproj",
    )(h, g, wa, wb)


def _foxprep_body(ff_ref, bf_ref, tri_ref, hsel_ref, q_ref, k_ref, ka_ref, st_ref, carry_ref, *, ts):
    @pl.when(pl.program_id(1) == 0)
    def _():
        carry_ref[...] = jnp.zeros_like(carry_ref)

    logf = _log_sigmoid(ff_ref[0] + bf_ref[...])
    c = _tri_dot(tri_ref[...], logf) + carry_ref[...]
    carry_ref[...] = c[ts - 1:ts, :]
    cl = c * LOG2E
    lane = lax.broadcasted_iota(I32, (1, LANES), 1)
    for h in range(FOX_HEADS):
        cols = slice(h * HEAD_BLK, (h + 1) * HEAD_BLK)
        terms = _split3(jnp.broadcast_to(-cl[:, h:h + 1], (ts, LANES)))
        ka = k_ref[0, :, cols]
        for i, term in enumerate(terms):
            ka = jnp.where(lane == BIAS_LANE0 + i, term, ka)
        ka_ref[0, :, cols] = ka

    def max_norm(ref):
        x = ref[0].astype(F32)
        n2 = jnp.dot((x * x).astype(BF16), hsel_ref[...], preferred_element_type=F32)
        return jnp.sqrt(jnp.max(n2, axis=0, keepdims=True)) * NORM_SLACK

    qn_row = max_norm(q_ref)
    kn_row = max_norm(k_ref)
    rowi = lax.broadcasted_iota(I32, (8, LANES), 0)
    st = jnp.where(rowi == 0, qn_row, jnp.where(rowi == 1, kn_row, jnp.where(rowi == 2, cl[0:1, :],
                   jnp.where(rowi == 3, cl[ts - 1:ts, :], 0.0))))
    st_ref[0, 0] = st


def _foxprep(a3, b3, bf_pad, tri, ts):
    B, S, _ = b3.shape
    hw = FOX_HEADS * HEAD_BLK
    head_sel = (jnp.arange(hw)[:, None] // HEAD_BLK == jnp.arange(LANES)[None, :]).astype(BF16)
    return pl.pallas_call(
        functools.partial(_foxprep_body, ts=ts),
        grid=(B, S // ts),
        in_specs=[
            pl.BlockSpec((1, ts, LANES), lambda b, s: (b, s, 2)),
            pl.BlockSpec((1, LANES), lambda b, s: (0, 0)),
            pl.BlockSpec((ts, ts), lambda b, s: (0, 0)),
            pl.BlockSpec((hw, LANES), lambda b, s: (0, 0)),
            pl.BlockSpec((1, ts, hw), lambda b, s: (b, s, Q_COL0 // hw)),
            pl.BlockSpec((1, ts, hw), lambda b, s: (b, s, K_COL0 // hw)),
        ],
        out_specs=[pl.BlockSpec((1, ts, hw), lambda b, s: (b, s, 0)),
                   pl.BlockSpec((1, 1, 8, LANES), lambda b, s: (b, s, 0, 0))],
        out_shape=[jax.ShapeDtypeStruct((B, S, hw), BF16),
                   jax.ShapeDtypeStruct((B, S // ts, 8, LANES), F32)],
        scratch_shapes=[pltpu.VMEM((1, LANES), F32)],
        compiler_params=_cparams(("parallel", "arbitrary")),
        name="foxprep",
    )(b3, bf_pad, tri, head_sel, a3, a3)


def _fox_body(js_ref, q_ref, k_ref, v_ref, g_ref, o_ref, s_ref, p_ref, m_ref, acc_ref, *, t, rc):
    b = pl.program_id(0)
    hp = pl.program_id(1)
    qi = pl.program_id(2)
    nt = pl.num_programs(2)
    lane = lax.broadcasted_iota(I32, (1, LANES), 1)
    lo = lane < FOX_HEAD_DIM
    ones_l = (lane >= BIAS_LANE0) & (lane < BIAS_LANE0 + BIAS_TERMS)
    sum_l = (lane == FOX_HEAD_DIM, lane == 0)
    qs = []
    for hh in range(2):
        qh = q_ref[0, :, hh * HEAD_BLK:(hh + 1) * HEAD_BLK]
        qs.append(jnp.where(ones_l, jnp.ones_like(qh), qh))
    m_ref[...] = jnp.full(m_ref.shape, -jnp.inf, F32)
    acc_ref[...] = jnp.zeros(acc_ref.shape, F32)
    base = (b * FOX_HEADS + 2 * hp) * nt + qi
    j0 = jnp.minimum(js_ref[base], js_ref[base + nt])
    nt_dims = (((1,), (1,)), ((), ()))
    nblk = t // LANES

    def tile(j, masked):
        start = pl.multiple_of(j * t, t)
        for hh in range(2):
            kt = k_ref[0, pl.ds(start, t), hh * HEAD_BLK:(hh + 1) * HEAD_BLK]
            s_ref[hh] = lax.dot_general(qs[hh], kt, nt_dims, preferred_element_type=F32)
        for hh in range(2):
            for r in range(t // rc):
                rows = slice(r * rc, (r + 1) * rc)
                diag_blk = (r * rc) // LANES
                mx = None
                for cb in range(diag_blk + 1 if masked else nblk):
                    cols = slice(cb * LANES, (cb + 1) * LANES)
                    sc = s_ref[hh, rows, cols]
                    if masked and cb == diag_blk:
                        ri = lax.broadcasted_iota(I32, (rc, LANES), 0) + r * rc
                        ci = lax.broadcasted_iota(I32, (rc, LANES), 1) + cb * LANES
                        sc = jnp.where(ci <= ri, sc, -jnp.inf)
                        s_ref[hh, rows, cols] = sc
                    mx = sc if mx is None else jnp.maximum(mx, sc)
                m_old = m_ref[hh, rows, :]
                m_new = jnp.maximum(m_old, jnp.broadcast_to(jnp.max(mx, axis=-1, keepdims=True), (rc, LANES)))
                m_ref[hh, rows, :] = m_new
                acc_ref[hh, rows, :] = jnp.exp2(m_old - m_new) * acc_ref[hh, rows, :]
            for r in range(t // rc):
                rows = slice(r * rc, (r + 1) * rc)
                diag_blk = (r * rc) // LANES
                m_new = m_ref[hh, rows, :]
                for cb in range(nblk):
                    cols = slice(cb * LANES, (cb + 1) * LANES)
                    if masked and cb > diag_blk:
                        p_ref[hh, rows, cols] = jnp.zeros((rc, LANES), BF16)
                    else:
                        p_ref[hh, rows, cols] = jnp.exp2(s_ref[hh, rows, cols] - m_new).astype(BF16)
            vt = v_ref[0, pl.ds(start, t), hh * HEAD_BLK:(hh + 1) * HEAD_BLK]
            vt = jnp.where(sum_l[hh], jnp.ones_like(vt), vt)
            acc_ref[hh] += jnp.dot(p_ref[hh], vt, preferred_element_type=F32)

    def body(j, carry):
        tile(j, False)
        return carry

    lax.fori_loop(j0, qi, body, 0)
    tile(qi, True)
    outs = []
    for hh in range(2):
        acc = acc_ref[hh]
        val = lo if hh == 0 else jnp.logical_not(lo)
        w = jnp.where(val, 1.0 / FOX_HEAD_DIM, jnp.where(sum_l[hh], RMS_EPS, 0.0))
        outs.append(acc * lax.rsqrt(jnp.sum(acc * acc * w, axis=-1, keepdims=True)))
    o_ref[0] = (jnp.where(lo, outs[0], outs[1]) * g_ref[...]).astype(BF16)


def _fox_first_tiles(stats):
    qn, kn, cs, ce = (stats[:, :, i, :FOX_HEADS].transpose(0, 2, 1) for i in range(4))
    bound = cs[..., :, None] - ce[..., None, :] + qn[..., :, None] * (kn[..., None, :] + kn[..., :, None])
    nt = qn.shape[-1]
    ti = jnp.arange(nt)[:, None]
    tj = jnp.arange(nt)[None, :]
    need = (tj == ti) | ((tj < ti) & (bound >= -SKIP_LOG2))
    return jnp.argmax(need, axis=-1).astype(I32).reshape(-1)


def _fox(first, a3, ka, gain, t, rc):
    B, S, _ = a3.shape
    pair = 2 * HEAD_BLK
    grid_spec = pltpu.PrefetchScalarGridSpec(
        num_scalar_prefetch=1,
        grid=(B, FOX_HEADS // 2, S // t),
        in_specs=[
            pl.BlockSpec((1, t, pair), lambda b, hp, qi, js: (b, qi, Q_COL0 // pair + hp)),
            pl.BlockSpec((1, S, pair), lambda b, hp, qi, js: (b, 0, hp)),
            pl.BlockSpec((1, S, pair), lambda b, hp, qi, js: (b, 0, V_COL0 // pair + hp)),
            pl.BlockSpec((1, LANES), lambda b, hp, qi, js: (0, hp)),
        ],
        out_specs=pl.BlockSpec((1, t, LANES), lambda b, hp, qi, js: (b, qi, hp)),
        scratch_shapes=[pltpu.VMEM((2, t, t), F32), pltpu.VMEM((2, t, t), BF16),
                        pltpu.VMEM((2, t, LANES), F32), pltpu.VMEM((2, t, LANES), F32)],
    )
    return pl.pallas_call(
        functools.partial(_fox_body, t=t, rc=rc),
        grid_spec=grid_spec,
        out_shape=jax.ShapeDtypeStruct((B, S, FOX_WIDTH), BF16),
        compiler_params=_cparams(("parallel", "parallel", "arbitrary")),
        name="fox_attention",
    )(first, a3, ka, a3, gain)


def _hgrn_body(q_ref, f_ref, v_ref, gg_ref, lb_ref, gain_ref, t64_ref, hm_ref, o_ref, st_ref, o_acc, *, tc):
    @pl.when(pl.program_id(2) == 0)
    def _():
        st_ref[...] = jnp.zeros_like(st_ref)

    C, SB = HG_CHUNK, HG_SUB
    lane = lax.broadcasted_iota(I32, (1, LANES), 1)
    lo = lane < HG_DK
    lb = lb_ref[...]
    fl = f_ref[0]
    t2 = jnp.log1p(-lb) + _log_sigmoid(fl)
    t1 = jnp.where(lb > 0.0, jnp.log(jnp.maximum(lb, 1e-37)), NEG_BIG)
    logf = jnp.maximum(t1, t2) + jnp.log1p(jnp.exp(-jnp.abs(t1 - t2)))
    kk = (1.0 - lb) / (1.0 + jnp.exp(fl))
    q = q_ref[0].astype(F32)
    nch = tc // C
    wide = jnp.concatenate([logf[ch * C:(ch + 1) * C] for ch in range(nch)], axis=1)
    a_wide = _tri_dot(t64_ref[...], wide)
    a = jnp.concatenate([a_wide[:, ch * LANES:(ch + 1) * LANES] for ch in range(nch)], axis=0)
    qa = (q * jnp.exp(a)).astype(BF16)

    rowblk = lax.broadcasted_iota(I32, (C, 1), 0) // SB
    r64 = lax.broadcasted_iota(I32, (C, C), 0)
    c64 = lax.broadcasted_iota(I32, (C, C), 1)
    causal = c64 <= r64
    rr = lax.broadcasted_iota(I32, (LANES, LANES), 0)
    cc = lax.broadcasted_iota(I32, (LANES, LANES), 1)
    same_head = (rr < HG_DK) == (cc < HG_DK)
    nt = (((1,), (1,)), ((), ()))
    tn = (((0,), (0,)), ((), ()))

    st = st_ref[...]
    for ch in range(tc // C):
        sl = slice(ch * C, (ch + 1) * C)
        a_c, kk_c, v_c = a[sl], kk[sl], v_ref[0, sl, :]
        a_last = a_c[C - 1:C, :]
        rows0, rows1 = [], []
        for blk in range(C // SB):
            b0 = ch * C + blk * SB
            e_blk = a_c[blk * SB - 1:blk * SB, :] if blk else jnp.zeros((1, LANES), F32)
            live = rowblk <= blk
            k_blk = jnp.where(live, kk_c * jnp.exp(jnp.where(live, e_blk - a_c, 0.0)), 0.0).astype(BF16)
            q_blk = q[b0:b0 + SB, :] * jnp.exp(a_c[blk * SB:(blk + 1) * SB, :] - e_blk)
            lhs = jnp.concatenate([jnp.where(lo, q_blk, 0.0), jnp.where(lo, 0.0, q_blk)], axis=0).astype(BF16)
            sc = lax.dot_general(lhs, k_blk, nt, preferred_element_type=F32)
            rows0.append(sc[0:SB])
            rows1.append(sc[SB:2 * SB])
        s0 = jnp.where(causal, jnp.concatenate(rows0, axis=0), 0.0).astype(BF16)
        s1 = jnp.where(causal, jnp.concatenate(rows1, axis=0), 0.0).astype(BF16)
        intra = jnp.where(lo, jnp.dot(s0, v_c, preferred_element_type=F32),
                          jnp.dot(s1, v_c, preferred_element_type=F32))
        inter = lax.dot_general(qa[sl], st.astype(BF16), nt, preferred_element_type=F32)
        o = inter + intra
        kd = (kk_c * jnp.exp(a_last - a_c)).astype(BF16)
        upd = lax.dot_general(v_c, kd, tn, preferred_element_type=F32)
        st = jnp.where(same_head, st * jnp.exp(a_last) + upd, 0.0)
        o_acc[sl, :] = o
    st_ref[...] = st

    o = o_acc[...]
    sq = o * o
    hi = sq.astype(BF16)
    lo_t = (sq - hi.astype(F32)).astype(BF16)
    ms = (jnp.dot(hi, hm_ref[...], preferred_element_type=F32)
          + jnp.dot(lo_t, hm_ref[...], preferred_element_type=F32))
    g = gg_ref[0].astype(F32)
    silu = g / (1.0 + jnp.exp(-g))
    o_ref[0] = (o * lax.rsqrt(ms + RMS_EPS) * gain_ref[...] * silu).astype(BF16)


def _hgrn(a3, b3, lb, gain, tri, head_mean, tc):
    B, S, _ = a3.shape
    blk = lambda c0: pl.BlockSpec((1, tc, LANES), lambda b, hp, s: (b, s, c0 + hp))
    row = pl.BlockSpec((1, LANES), lambda b, hp, s: (0, hp))
    const = lambda n: pl.BlockSpec((n, n), lambda b, hp, s: (0, 0))
    return pl.pallas_call(
        functools.partial(_hgrn_body, tc=tc),
        grid=(B, HG_HEADS // 2, S // tc),
        in_specs=[blk(GQ0), blk(0), blk(GI0), blk(GG0), row, row, const(HG_CHUNK), const(LANES)],
        out_specs=pl.BlockSpec((1, tc, LANES), lambda b, hp, s: (b, s, hp)),
        out_shape=jax.ShapeDtypeStruct((B, S, HG_WIDTH), BF16),
        scratch_shapes=[pltpu.VMEM((LANES, LANES), F32), pltpu.VMEM((tc, LANES), F32)],
        compiler_params=_cparams(("parallel", "parallel", "arbitrary")),
        name="hgrn2",
    )(a3, b3, a3, a3, lb, gain, tri, head_mean)


def _pool_body(u_ref, w_ref, sc_ref, o_ref, ext_ref, *, tc):
    si = pl.program_id(1)

    @pl.when(si == 0)
    def _():
        ext_ref[0:POOL_HALO, :] = jnp.zeros((POOL_HALO, POOL_WIDTH), F32)

    u = u_ref[0].astype(F32)
    ext_ref[POOL_HALO:POOL_HALO + tc, :] = u
    acc = u
    sums = {}
    for j in range(1, max(POOL_WINDOWS)):
        acc = acc + ext_ref[POOL_HALO - j:POOL_HALO - j + tc, :]
        if j + 1 in POOL_WINDOWS:
            sums[j + 1] = acc
    lane = lax.broadcasted_iota(I32, (1, POOL_WIDTH), 1)
    grp = lane // POOL_GROUP_DIM
    win = sums[POOL_WINDOWS[-1]]
    wlen = jnp.full((1, POOL_WIDTH), float(POOL_WINDOWS[-1]), F32)
    for gi in range(len(POOL_WINDOWS) - 2, -1, -1):
        win = jnp.where(grp == gi, sums[POOL_WINDOWS[gi]], win)
        wlen = jnp.where(grp == gi, float(POOL_WINDOWS[gi]), wlen)
    pos = (si * tc + lax.broadcasted_iota(I32, (tc, 1), 0) + 1).astype(F32)
    pooled = win / jnp.minimum(pos, wlen) - u
    y = jnp.dot(pooled.astype(BF16), w_ref[...], preferred_element_type=F32) * sc_ref[...]
    o_ref[0] = y.astype(BF16)
    ext_ref[0:POOL_HALO, :] = u[tc - POOL_HALO:tc, :]


def _pool(a3, wbd, scale, tc):
    B, S, _ = a3.shape
    return pl.pallas_call(
        functools.partial(_pool_body, tc=tc),
        grid=(B, S // tc),
        in_specs=[
            pl.BlockSpec((1, tc, POOL_WIDTH), lambda b, s: (b, s, PU_BLK256)),
            pl.BlockSpec((POOL_WIDTH, POOL_WIDTH), lambda b, s: (0, 0)),
            pl.BlockSpec((1, POOL_WIDTH), lambda b, s: (0, 0)),
        ],
        out_specs=pl.BlockSpec((1, tc, POOL_WIDTH), lambda b, s: (b, s, 0)),
        out_shape=jax.ShapeDtypeStruct((B, S, POOL_WIDTH), BF16),
        scratch_shapes=[pltpu.VMEM((POOL_HALO + tc, POOL_WIDTH), F32)],
        compiler_params=_cparams(("parallel", "arbitrary")),
        name="pool_mixer",
    )(a3, wbd, scale)


def _router_body(of_ref, oh_ref, op_ref, h_ref, wo_ref, g_ref, wr_ref, br_ref, tri_ref,
                 hn_ref, ri_ref, rg_ref, cnt_ref, carry_ref, *, tm):
    @pl.when(pl.program_id(0) == 0)
    def _():
        carry_ref[...] = jnp.zeros_like(carry_ref)

    f0, f1 = FOX_WIDTH, FOX_WIDTH + HG_WIDTH
    mix = (jnp.dot(of_ref[...], wo_ref[0:f0, :], preferred_element_type=F32)
           + jnp.dot(oh_ref[...], wo_ref[f0:f1, :], preferred_element_type=F32)
           + jnp.dot(op_ref[...], wo_ref[f1:f1 + POOL_WIDTH, :], preferred_element_type=F32))
    hn = h_ref[...] + mix
    hn_ref[...] = hn
    ms = jnp.mean(hn * hn, axis=-1, keepdims=True)
    xn = hn * lax.rsqrt(ms + RMS_EPS) * g_ref[...]
    x1 = xn.astype(BF16)
    x2 = (xn - x1.astype(F32)).astype(BF16)
    d = lambda a, b: jnp.dot(a, b, preferred_element_type=F32)
    logits = d(x1, wr_ref[0]) + d(x1, wr_ref[1]) + d(x2, wr_ref[0]) + br_ref[...]

    lanef = lax.broadcasted_iota(I32, (tm, LANES), 1).astype(F32)
    l = logits
    vals, idxs, sels = [], [], []
    for _ in range(TOP_K):
        mk = jnp.max(l, axis=-1, keepdims=True)
        ik = jnp.min(jnp.where(l == mk, lanef, float(LANES)), axis=-1, keepdims=True)
        sk = lanef == ik
        vals.append(mk)
        idxs.append(ik)
        sels.append(sk)
        l = jnp.where(sk, -jnp.inf, l)
    es = [jnp.exp(v - vals[0]) for v in vals]
    den = es[0] + es[1] + es[2] + es[3]
    sel = jnp.zeros((tm, LANES), F32)
    for sk in sels:
        sel = jnp.where(sk, 1.0, sel)
    rank = jnp.dot(tri_ref[...], sel.astype(BF16), preferred_element_type=F32) + carry_ref[...]
    carry = carry_ref[...] + jnp.sum(sel, axis=0, keepdims=True)
    carry_ref[...] = carry
    cnt_ref[...] = carry
    ri = jnp.zeros((tm, LANES), F32)
    rg = jnp.zeros((tm, LANES), F32)
    for k in range(TOP_K):
        pos_k = jnp.sum(jnp.where(sels[k], rank, 0.0), axis=-1, keepdims=True)
        ri = jnp.where(lanef == float(k), idxs[k], ri)
        ri = jnp.where(lanef == float(TOP_K + k), pos_k, ri)
        rg = jnp.where(lanef == float(k), es[k] / den, rg)
    ri_ref[...] = ri.astype(I32)
    rg_ref[...] = rg


def _router(o_fox, o_hg, o_pool, h, wo, g, wr, br, tri, tm):
    T, D = h.shape
    rowblk = lambda w: pl.BlockSpec((tm, w), lambda i: (i, 0))
    const = lambda shp: pl.BlockSpec(shp, lambda i: tuple(0 for _ in shp))
    return pl.pallas_call(
        functools.partial(_router_body, tm=tm),
        grid=(T // tm,),
        in_specs=[rowblk(FOX_WIDTH), rowblk(HG_WIDTH), rowblk(POOL_WIDTH), rowblk(D),
                  const((D, D)), const((1, D)), const((2, D, LANES)), const((1, LANES)), const((tm, tm))],
        out_specs=[rowblk(D), rowblk(LANES), rowblk(LANES), const((1, LANES))],
        out_shape=[jax.ShapeDtypeStruct((T, D), F32), jax.ShapeDtypeStruct((T, LANES), I32),
                   jax.ShapeDtypeStruct((T, LANES), F32), jax.ShapeDtypeStruct((1, LANES), F32)],
        scratch_shapes=[pltpu.VMEM((1, LANES), F32)],
        compiler_params=_cparams(("arbitrary",)),
        name="outproj_router",
    )(o_fox, o_hg, o_pool, h, wo, g, wr, br, tri)


def _to_row_tiles(ref, x):
    ref[...] = pltpu.einshape("r(cl)->rcl", x, c=ref.shape[-2])


def _from_row_tiles(ref):
    return pltpu.einshape("rcl->r(cl)", ref[...])


def _row_copy(src_ref, src_row, dst_ref, dst_row, sem):
    return pltpu.make_async_copy(src_ref.at[pl.ds(src_row, 1)], dst_ref.at[pl.ds(dst_row, 1)], sem)


def _wait_rows(tile_ref, sem, times):
    for _ in range(times):
        pltpu.make_async_copy(tile_ref, tile_ref, sem).wait()


def _dispatch_body(pad0_ref, padn_ref, dst_ref, h_ref, g_ref, buf_ref, xn_ref, zero_ref, sem, zsem, *, tt, bm):
    i = pl.program_id(0)
    n = pl.num_programs(0)
    slot = i % 2
    stage = xn_ref.at[slot]

    @pl.when(i >= 2)
    def _():
        _wait_rows(stage, sem.at[slot], TOP_K)

    x = h_ref[...]
    ms = jnp.mean(x * x, axis=-1, keepdims=True)
    _to_row_tiles(stage, x * lax.rsqrt(ms + RMS_EPS) * g_ref[...])

    def issue(r, _):
        for k in range(TOP_K):
            _row_copy(stage, r, buf_ref, dst_ref[0, 0, r * TOP_K + k], sem.at[slot]).start(priority=k % 2)
        return 0

    lax.fori_loop(0, tt, issue, 0, unroll=DMA_UNROLL)

    @pl.when(i == 0)
    def _():
        zero_ref[...] = jnp.zeros_like(zero_ref)
        half = bm // 2
        n_exp = pad0_ref.shape[0] - 1

        def run(cp, pred, wait):
            @pl.when(pred)
            def _():
                cp.wait() if wait else cp.start()

        def pad_copies(e, wait):
            off, cnt = pad0_ref[e], padn_ref[e]
            for r in range(SUBLANES - 1):
                run(_row_copy(zero_ref, 0, buf_ref, off + r, zsem), r < (cnt & (SUBLANES - 1)), wait)
            off = off + (cnt & (SUBLANES - 1))
            sz = SUBLANES
            while sz <= half:
                take = (cnt & sz) != 0
                dst = buf_ref.at[pl.ds(pl.multiple_of(off, SUBLANES), sz)]
                run(pltpu.make_async_copy(zero_ref.at[pl.ds(0, sz)], dst, zsem), take, wait)
                off = off + jnp.where(take, sz, 0)
                sz *= 2
            return 0

        def tail_copy(c, wait):
            dst = buf_ref.at[pl.ds(pl.multiple_of(pad0_ref[n_exp] + c * half, SUBLANES), half)]
            cp = pltpu.make_async_copy(zero_ref, dst, zsem)
            cp.wait() if wait else cp.start()
            return 0

        n_tail = padn_ref[n_exp] // half
        for wait in (False, True):
            lax.fori_loop(0, n_exp, lambda e, c, w=wait: pad_copies(e, w), 0)
            lax.fori_loop(0, n_tail, lambda c, _, w=wait: tail_copy(c, w), 0)

    @pl.when(i == n - 1)
    def _():
        _wait_rows(stage, sem.at[slot], TOP_K)

    @pl.when((i == n - 1) & (n >= 2))
    def _():
        _wait_rows(xn_ref.at[1 - slot], sem.at[1 - slot], TOP_K)


def _dispatch(pad0, padn, dst3, hn, g, R, tt, bm):
    T, D = hn.shape
    grid_spec = pltpu.PrefetchScalarGridSpec(
        num_scalar_prefetch=2,
        grid=(T // tt,),
        in_specs=[
            pl.BlockSpec((1, 1, tt * TOP_K), lambda i, p0, pn: (i, 0, 0), memory_space=pltpu.SMEM),
            pl.BlockSpec((tt, D), lambda i, p0, pn: (i, 0)),
            pl.BlockSpec((1, D), lambda i, p0, pn: (0, 0)),
        ],
        out_specs=pl.BlockSpec(memory_space=pl.ANY),
        scratch_shapes=[pltpu.VMEM((2, tt, D // LANES, LANES), F32), pltpu.VMEM((bm // 2, D // LANES, LANES), F32),
                        pltpu.SemaphoreType.DMA((2,)), pltpu.SemaphoreType.DMA],
    )
    return pl.pallas_call(
        functools.partial(_dispatch_body, tt=tt, bm=bm),
        grid_spec=grid_spec,
        out_shape=jax.ShapeDtypeStruct((R, D // LANES, LANES), F32),
        compiler_params=_cparams(("arbitrary",)),
        name="moe_dispatch",
    )(pad0, padn, dst3, hn, g)


MXU_DIM = 256


def _expert_body(te_ref, nu_ref, x_ref, w1_ref, w2_ref, perm_ref, b1g_ref, b1l_ref, b2_ref, y_ref,
                 w1g_scr, w1l_scr, w2_scr):
    i = pl.program_id(0)
    live = i < nu_ref[0]
    prev = te_ref[jnp.maximum(i - 1, 0)]

    @pl.when(live & ((i == 0) | (te_ref[i] != prev)))
    def _():
        half = MXU_DIM // 2
        for grp in range(w1_ref.shape[-1] // MXU_DIM):
            w = w1_ref[0, :, grp * MXU_DIM:(grp + 1) * MXU_DIM].astype(BF16)
            r = jnp.dot(w, perm_ref[...], preferred_element_type=F32).astype(BF16)
            w1g_scr[:, grp * half:(grp + 1) * half] = r[:, :half]
            w1l_scr[:, grp * half:(grp + 1) * half] = r[:, half:]
        w2_scr[...] = w2_ref[0].astype(BF16)

    @pl.when(live)
    def _():
        x = _from_row_tiles(x_ref).astype(BF16)
        glu = jnp.dot(x, w1g_scr[...], preferred_element_type=F32) + b1g_ref[0]
        lin = jnp.dot(x, w1l_scr[...], preferred_element_type=F32) + b1l_ref[0]
        glu = jnp.minimum(glu, SWIGLU_LIMIT)
        lin = jnp.clip(lin, -SWIGLU_LIMIT, SWIGLU_LIMIT)
        act = glu / (1.0 + jnp.exp(-SWIGLU_ALPHA * glu)) * (lin + 1.0)
        _to_row_tiles(y_ref, jnp.dot(act.astype(BF16), w2_scr[...], preferred_element_type=F32) + b2_ref[0])

    @pl.when(jnp.logical_not(live))
    def _():
        y_ref[...] = jnp.zeros_like(y_ref)


def _experts(tile_e, n_used, buf, w1, w2, b1g, b1l, b2, bm, first_expert):
    R, dsub, _ = buf.shape
    D = dsub * LANES
    F2 = w1.shape[-1]
    F = F2 // 2
    half = MXU_DIM // 2
    pi = jnp.arange(MXU_DIM)[:, None]
    pj = jnp.arange(MXU_DIM)[None, :]
    perm = jnp.where(pj < half, pi == 2 * pj, pi == 2 * (pj - half) + 1).astype(BF16)
    rows = pl.BlockSpec((bm, dsub, LANES), lambda i, te, nu: (jnp.minimum(i, nu[0] - 1), 0, 0))
    wspec = lambda a, b, e0=0: pl.BlockSpec((1, a, b), lambda i, te, nu: (e0 + te[i], 0, 0))
    grid_spec = pltpu.PrefetchScalarGridSpec(
        num_scalar_prefetch=2,
        grid=(R // bm,),
        in_specs=[rows, wspec(D, F2, first_expert), wspec(F, D, first_expert),
                  pl.BlockSpec((MXU_DIM, MXU_DIM), lambda i, te, nu: (0, 0)),
                  wspec(1, F), wspec(1, F), wspec(1, D)],
        out_specs=pl.BlockSpec((bm, dsub, LANES), lambda i, te, nu: (i, 0, 0)),
        scratch_shapes=[pltpu.VMEM((D, F), BF16), pltpu.VMEM((D, F), BF16), pltpu.VMEM((F, D), BF16)],
    )
    return pl.pallas_call(
        _expert_body,
        grid_spec=grid_spec,
        out_shape=jax.ShapeDtypeStruct((R, dsub, LANES), F32),
        compiler_params=_cparams(("arbitrary",), vmem=EXPERT_VMEM_LIMIT),
        name="moe_experts",
    )(tile_e, n_used, buf, w1, w2, perm, b1g, b1l, b2)


def _combine_body(dst_ref, h_ref, rg_ref, gf_ref, y_ref, o_ref, rows_ref, sem, *, tt, final_norm):
    s = pl.program_id(0)
    n_tiles = pl.num_programs(0) - 1

    @pl.when(s < n_tiles)
    def _():
        slot = s % 2

        def issue(r, _):
            for k in range(TOP_K):
                _row_copy(y_ref, dst_ref[0, 0, r * TOP_K + k], rows_ref.at[slot, k], r,
                          sem.at[slot]).start(priority=k % 2)
            return 0

        lax.fori_loop(0, tt, issue, 0, unroll=DMA_UNROLL)

    @pl.when(s >= 1)
    def _():
        slot = (s - 1) % 2
        _wait_rows(rows_ref.at[slot, 0], sem.at[slot], TOP_K)
        out = h_ref[...]
        gates = rg_ref[...]
        for k in range(TOP_K):
            out = out + gates[:, k:k + 1] * _from_row_tiles(rows_ref.at[slot, k])
        if final_norm:
            ms = jnp.mean(out * out, axis=-1, keepdims=True)
            out = out * lax.rsqrt(ms + RMS_EPS) * gf_ref[...]
        o_ref[...] = out


def _combine(dst3, hn, rg, g_final, ybuf, tt, final_norm):
    T, D = hn.shape
    n_tiles = T // tt
    prev = lambda s: (jnp.maximum(s - 1, 0), 0)
    return pl.pallas_call(
        functools.partial(_combine_body, tt=tt, final_norm=final_norm),
        grid=(n_tiles + 1,),
        in_specs=[
            pl.BlockSpec((1, 1, tt * TOP_K), lambda s: (jnp.minimum(s, n_tiles - 1), 0, 0), memory_space=pltpu.SMEM),
            pl.BlockSpec((tt, D), prev),
            pl.BlockSpec((tt, LANES), prev),
            pl.BlockSpec((1, D), lambda s: (0, 0)),
            pl.BlockSpec(memory_space=pl.ANY),
        ],
        out_specs=pl.BlockSpec((tt, D), prev),
        out_shape=jax.ShapeDtypeStruct((T, D), F32),
        scratch_shapes=[pltpu.VMEM((2, TOP_K, tt, D // LANES, LANES), F32), pltpu.SemaphoreType.DMA((2,))],
        compiler_params=_cparams(("arbitrary",)),
        name="moe_combine",
    )(dst3, hn, rg, g_final, ybuf)


def _tiles(B, S):
    T = B * S
    ts = min(512, S)
    tm = min(512, T)
    tt = min(256, T)
    bm = min(512, T)
    return ts, tm, tt, bm


def _block_tri(n, blk, strict=False):
    r = jnp.arange(n)[:, None]
    c = jnp.arange(n)[None, :]
    keep = (c < r) if strict else (c <= r)
    return (keep & (r // blk == c // blk)).astype(BF16)


def kernel(x, norm_mix, w_in, b_fox_f, fox_gain, hgrn_lb, hgrn_gain, pool_w, pool_scale, w_out,
           norm_moe, w_router, b_router, w1, b1, w2, b2, norm_final):
    B, S, D = x.shape
    L = w_in.shape[0]
    T = B * S
    E = w1.shape[1]
    ts, tm, tt, bm = _tiles(B, S)

    offs = [0]
    for wdt in (FOX_WIDTH, FOX_WIDTH, FOX_WIDTH, FOX_HEADS, HG_WIDTH, HG_WIDTH, HG_WIDTH, HG_WIDTH, POOL_WIDTH):
        offs.append(offs[-1] + wdt)
    col = lambda i: w_in[:, :, offs[i]:offs[i + 1]]
    def head_blocks(w):
        w = w.reshape(L, D, FOX_HEADS, FOX_HEAD_DIM)
        w = jnp.pad(w, ((0, 0), (0, 0), (0, 0), (0, HEAD_BLK - FOX_HEAD_DIM)))
        return w.reshape(L, D, FOX_HEADS * HEAD_BLK)

    def head_blocks_v(w):
        w = w.reshape(L, D, FOX_HEADS, FOX_HEAD_DIM)
        odd = (jnp.arange(FOX_HEADS) % 2 == 1)[:, None]
        w = jnp.concatenate([jnp.where(odd, 0.0, w), jnp.where(odd, w, 0.0)], axis=-1)
        return w.reshape(L, D, FOX_HEADS * HEAD_BLK)

    scale = FOX_HEAD_DIM ** -0.5 * LOG2E
    wa = jnp.concatenate([head_blocks(col(0) * scale), head_blocks(col(1)), head_blocks_v(col(2)),
                          col(4), col(6), col(7), col(8)],
                         axis=-1).astype(BF16)
    wb = jnp.concatenate([col(5), col(3), jnp.zeros((L, D, LANES - FOX_HEADS), F32)], axis=-1).astype(BF16)
    bf_pad = jnp.pad(b_fox_f, ((0, 0), (0, LANES - FOX_HEADS)))
    p = jax.nn.softmax(hgrn_lb.astype(F32), axis=0)
    cs = jnp.cumsum(p, axis=0)
    lbs = cs - cs[0:1]
    eye = jnp.eye(len(POOL_WINDOWS), dtype=F32)
    pool_bd = jnp.einsum("lgcd,gh->lgchd", pool_w, eye).reshape(L, POOL_WIDTH, POOL_WIDTH).astype(BF16)
    wo = w_out.astype(BF16)
    wr_pad = jnp.pad(w_router, ((0, 0), (0, 0), (0, LANES - E)))
    wr_hi = wr_pad.astype(BF16)
    wr_lo = (wr_pad - wr_hi.astype(F32)).astype(BF16)
    wr = jnp.stack([wr_hi, wr_lo], axis=1)
    br_pad = jnp.pad(b_router, ((0, 0), (0, LANES - E)), constant_values=NEG_BIG)
    w1_all = w1.reshape((L * E,) + w1.shape[2:])
    w2_all = w2.reshape((L * E,) + w2.shape[2:])
    b1g = b1[..., 0::2][:, :, None, :]
    b1l = b1[..., 1::2][:, :, None, :]
    b2r = b2[:, :, None, :]

    tri_seq = _block_tri(ts, ts)
    tri64 = _block_tri(HG_CHUNK, HG_CHUNK)
    li = jnp.arange(LANES)
    head_mean = ((li[:, None] // HG_DK == li[None, :] // HG_DK) / HG_DK).astype(BF16)
    tri_tok = _block_tri(tm, tm, strict=True)

    M = T * TOP_K
    R = -(-M // bm) * bm + E * bm
    NB = R // bm

    h = x.reshape(T, D)
    for l in range(L):
        a2, b2d = _inproj(h, norm_mix[l][None], wa[l], wb[l], tm)
        a3 = a2.reshape(B, S, A_COLS)
        b3 = b2d.reshape(B, S, B_COLS)
        ka, stats = _foxprep(a3, b3, bf_pad[l][None], tri_seq, ts)
        o_fox = _fox(_fox_first_tiles(stats), a3, ka, fox_gain[l][None], ts, min(32, ts))
        o_hg = _hgrn(a3, b3, lbs[l][None], hgrn_gain[l][None], tri64, head_mean, ts)
        o_pool = _pool(a3, pool_bd[l], pool_scale[l][None], ts)
        hn, ri, rg, cnt = _router(o_fox.reshape(T, FOX_WIDTH), o_hg.reshape(T, HG_WIDTH),
                                  o_pool.reshape(T, POOL_WIDTH), h, wo[l], norm_moe[l][None],
                                  wr[l], br_pad[l][None], tri_tok, tm)
        counts = cnt[0, :E].astype(I32)
        padded = (counts + bm - 1) // bm * bm
        pend = jnp.cumsum(padded)
        pstart = pend - padded
        first_row = jnp.sum(jnp.where(ri[:, 0:TOP_K, None] == jnp.arange(E, dtype=I32), pstart.astype(I32), 0), axis=-1)
        dst = first_row + ri[:, TOP_K:2 * TOP_K]
        n_used = (pend[-1] // bm).astype(I32)
        tile_ids = jnp.minimum(jnp.arange(NB, dtype=I32), n_used - 1)
        tile_e = jnp.minimum(jnp.sum(pend[None, :] <= (tile_ids * bm)[:, None], axis=1), E - 1).astype(I32)
        dst3 = dst.reshape(T // tt, 1, tt * TOP_K)
        pad0 = jnp.concatenate([pstart + counts, pend[-1:]]).astype(I32)
        padn = jnp.concatenate([padded - counts, R - pend[-1:]]).astype(I32)
        buf = _dispatch(pad0, padn, dst3, hn, norm_moe[l][None], R, tt, bm)
        ybuf = _experts(tile_e, n_used.reshape(1), buf, w1_all, w2_all, b1g[l], b1l[l], b2r[l], bm, l * E)
        h = _combine(dst3, hn, rg, norm_final[None], ybuf, tt, final_norm=(l == L - 1))
    return h.reshape(B, S, D)
```

```python
import functools

import jax
import jax.numpy as jnp
from jax import lax
from jax.experimental import pallas as pl
from jax.experimental.pallas import tpu as pltpu

F32 = jnp.float32
BF16 = jnp.bfloat16
I32 = jnp.int32

LANES = 128
SUBLANES = 8
VMEM_LIMIT = 56 * 1024 * 1024
EXPERT_VMEM_LIMIT = 60 * 1024 * 1024

FOX_HEADS = 8
FOX_HEAD_DIM = 64
FOX_WIDTH = FOX_HEADS * FOX_HEAD_DIM
HG_HEADS = 4
HG_DK = 64
HG_WIDTH = HG_HEADS * HG_DK
HG_CHUNK = 64
HG_SUB = 16
POOL_WINDOWS = (2, 4, 8, 16)
POOL_GROUP_DIM = 64
POOL_WIDTH = len(POOL_WINDOWS) * POOL_GROUP_DIM
POOL_HALO = 16
N_EXPERTS = 32
TOP_K = 4
SWIGLU_LIMIT = 7.0
SWIGLU_ALPHA = 1.702
RMS_EPS = 1e-5
NEG_BIG = -1e30
DMA_UNROLL = 8

HEAD_BLK = LANES
QBLK0, KBLK0, VBLK0 = 0, FOX_HEADS // 2, FOX_HEADS
GQ0 = 3 * FOX_HEADS // 2
GI0, GG0 = GQ0 + 2, GQ0 + 4
PU_BLK256 = (GQ0 + 6) // 2
A_COLS = (GQ0 + 8) * LANES
BIAS_TERMS = 3


def _head_lanes(parity):
    lane = lax.broadcasted_iota(I32, (1, LANES), 1)
    lower = lane < FOX_HEAD_DIM
    return (lower, FOX_HEAD_DIM) if parity == 0 else (jnp.logical_not(lower), 0)


LOG2E = 1.4426950408889634
SKIP_LOG2 = 150.0
NORM_SLACK = 1.01
B_COLS = 384


def _cparams(sem, vmem=VMEM_LIMIT):
    return pltpu.CompilerParams(dimension_semantics=sem, vmem_limit_bytes=vmem)


def _split3(x):
    x1 = x.astype(BF16)
    r1 = x - x1.astype(F32)
    x2 = r1.astype(BF16)
    x3 = (r1 - x2.astype(F32)).astype(BF16)
    return x1, x2, x3


def _tri_dot(tri, x):
    x1, x2, x3 = _split3(x)
    d = lambda a: jnp.dot(tri, a, preferred_element_type=F32)
    return d(x1) + d(x2) + d(x3)


def _log_sigmoid(z):
    return jnp.minimum(z, 0.0) - jnp.log1p(jnp.exp(-jnp.abs(z)))


def _inproj_body(x_ref, g_ref, wa_ref, wb_ref, a_ref, b_ref):
    x = x_ref[...]
    ms = jnp.mean(x * x, axis=-1, keepdims=True)
    xn = (x * lax.rsqrt(ms + RMS_EPS) * g_ref[...]).astype(BF16)
    a_ref[...] = jnp.dot(xn, wa_ref[...], preferred_element_type=F32).astype(BF16)
    b_ref[...] = jnp.dot(xn, wb_ref[...], preferred_element_type=F32)


def _inproj(h, g, wa, wb, tm):
    T, D = h.shape
    return pl.pallas_call(
        _inproj_body,
        grid=(T // tm,),
        in_specs=[
            pl.BlockSpec((tm, D), lambda i: (i, 0)),
            pl.BlockSpec((1, D), lambda i: (0, 0)),
            pl.BlockSpec((D, A_COLS), lambda i: (0, 0)),
            pl.BlockSpec((D, B_COLS), lambda i: (0, 0)),
        ],
        out_specs=[
            pl.BlockSpec((tm, A_COLS), lambda i: (i, 0)),
            pl.BlockSpec((tm, B_COLS), lambda i: (i, 0)),
        ],
        out_shape=[jax.ShapeDtypeStruct((T, A_COLS), BF16), jax.ShapeDtypeStruct((T, B_COLS), F32)],
        compiler_params=_cparams(("parallel",)),
        name="inproj",
    )(h, g, wa, wb)


def _foxprep_body(ff_ref, bf_ref, tri_ref, hsel_ref, q_ref, k_ref, ka_ref, st_ref, carry_ref, *, ts):
    @pl.when(pl.program_id(1) == 0)
    def _():
        carry_ref[...] = jnp.zeros_like(carry_ref)

    logf = _log_sigmoid(ff_ref[0] + bf_ref[...])
    c = _tri_dot(tri_ref[...], logf) + carry_ref[...]
    carry_ref[...] = c[ts - 1:ts, :]
    cl = c * LOG2E
    lane = lax.broadcasted_iota(I32, (1, LANES), 1)
    for h in range(FOX_HEADS):
        feat, spare0 = _head_lanes(h % 2)
        terms = _split3(jnp.broadcast_to(-cl[:, h:h + 1], (ts, LANES)))
        pair = k_ref[0, :, (h // 2) * HEAD_BLK:(h // 2 + 1) * HEAD_BLK]
        ka = pair * feat.astype(BF16)
        for i, term in enumerate(terms):
            ka = jnp.where(lane == spare0 + i, term, ka)
        ka_ref[0, :, h * HEAD_BLK:(h + 1) * HEAD_BLK] = ka

    def max_norm(ref):
        x = ref[0].astype(F32)
        n2 = jnp.dot((x * x).astype(BF16), hsel_ref[...], preferred_element_type=F32)
        return jnp.sqrt(jnp.max(n2, axis=0, keepdims=True)) * NORM_SLACK

    qn_row = max_norm(q_ref)
    kn_row = max_norm(k_ref)
    rowi = lax.broadcasted_iota(I32, (8, LANES), 0)
    st = jnp.where(rowi == 0, qn_row, jnp.where(rowi == 1, kn_row, jnp.where(rowi == 2, cl[0:1, :],
                   jnp.where(rowi == 3, cl[ts - 1:ts, :], 0.0))))
    st_ref[0, 0] = st


def _foxprep(a3, b3, bf_pad, tri, ts):
    B, S, _ = b3.shape
    hw = FOX_HEADS * HEAD_BLK
    head_sel = (jnp.arange(FOX_WIDTH)[:, None] // FOX_HEAD_DIM == jnp.arange(LANES)[None, :]).astype(BF16)
    return pl.pallas_call(
        functools.partial(_foxprep_body, ts=ts),
        grid=(B, S // ts),
        in_specs=[
            pl.BlockSpec((1, ts, LANES), lambda b, s: (b, s, 2)),
            pl.BlockSpec((1, LANES), lambda b, s: (0, 0)),
            pl.BlockSpec((ts, ts), lambda b, s: (0, 0)),
            pl.BlockSpec((FOX_WIDTH, LANES), lambda b, s: (0, 0)),
            pl.BlockSpec((1, ts, FOX_WIDTH), lambda b, s: (b, s, QBLK0 * LANES // FOX_WIDTH)),
            pl.BlockSpec((1, ts, FOX_WIDTH), lambda b, s: (b, s, KBLK0 * LANES // FOX_WIDTH)),
        ],
        out_specs=[pl.BlockSpec((1, ts, hw), lambda b, s: (b, s, 0)),
                   pl.BlockSpec((1, 1, 8, LANES), lambda b, s: (b, s, 0, 0))],
        out_shape=[jax.ShapeDtypeStruct((B, S, hw), BF16),
                   jax.ShapeDtypeStruct((B, S // ts, 8, LANES), F32)],
        scratch_shapes=[pltpu.VMEM((1, LANES), F32)],
        compiler_params=_cparams(("parallel", "arbitrary")),
        name="foxprep",
    )(b3, bf_pad, tri, head_sel, a3, a3)


def _fox_body(js_ref, hd_ref, qa_ref, qb_ref, ka_ref, kb_ref, va_ref, vb_ref, g_ref, o_ref,
              s_ref, p_ref, m_ref, acc_ref, *, t, rc):
    q_refs, k_refs, v_refs = (qa_ref, qb_ref), (ka_ref, kb_ref), (va_ref, vb_ref)
    b = pl.program_id(0)
    hp = pl.program_id(1)
    qi = pl.program_id(2)
    nt = pl.num_programs(2)
    lane = lax.broadcasted_iota(I32, (1, LANES), 1)
    lo = lane < FOX_HEAD_DIM
    feat, sum_l, keep, one_at, qs = [], [], [], [], []
    for hh in range(2):
        f, spare0 = _head_lanes(hh)
        feat.append(f)
        sum_l.append(lane == spare0)
        keep.append(f.astype(BF16))
        one_at.append((lane == spare0).astype(BF16))
        ones_q = ((lane >= spare0) & (lane < spare0 + BIAS_TERMS)).astype(BF16)
        qs.append(q_refs[hh][0] * keep[hh] + ones_q)
    m_ref[...] = jnp.full(m_ref.shape, -jnp.inf, F32)
    acc_ref[...] = jnp.zeros(acc_ref.shape, F32)
    first = [js_ref[(b * FOX_HEADS + hd_ref[2 * hp + hh]) * nt + qi] for hh in range(2)]
    j0 = jnp.minimum(first[0], first[1])
    nt_dims = (((1,), (1,)), ((), ()))
    nblk = t // LANES

    def tile(j, masked):
        start = pl.multiple_of(j * t, t)
        for hh in range(2):
            kt = k_refs[hh][0, pl.ds(start, t), :]
            s_ref[hh] = lax.dot_general(qs[hh], kt, nt_dims, preferred_element_type=F32)
        for hh in range(2):
            for r in range(t // rc):
                rows = slice(r * rc, (r + 1) * rc)
                diag_blk = (r * rc) // LANES
                mx = None
                for cb in range(diag_blk + 1 if masked else nblk):
                    cols = slice(cb * LANES, (cb + 1) * LANES)
                    sc = s_ref[hh, rows, cols]
                    if masked and cb == diag_blk:
                        ri = lax.broadcasted_iota(I32, (rc, LANES), 0) + r * rc
                        ci = lax.broadcasted_iota(I32, (rc, LANES), 1) + cb * LANES
                        sc = jnp.where(ci <= ri, sc, -jnp.inf)
                        s_ref[hh, rows, cols] = sc
                    mx = sc if mx is None else jnp.maximum(mx, sc)
                m_old = m_ref[hh, rows, :]
                m_new = jnp.maximum(m_old, jnp.broadcast_to(jnp.max(mx, axis=-1, keepdims=True), (rc, LANES)))
                m_ref[hh, rows, :] = m_new
                acc_ref[hh, rows, :] = jnp.exp2(m_old - m_new) * acc_ref[hh, rows, :]
            for r in range(t // rc):
                rows = slice(r * rc, (r + 1) * rc)
                diag_blk = (r * rc) // LANES
                m_new = m_ref[hh, rows, :]
                for cb in range(nblk):
                    cols = slice(cb * LANES, (cb + 1) * LANES)
                    if masked and cb > diag_blk:
                        p_ref[hh, rows, cols] = jnp.zeros((rc, LANES), BF16)
                    else:
                        p_ref[hh, rows, cols] = jnp.exp2(s_ref[hh, rows, cols] - m_new).astype(BF16)
            vt = v_refs[hh][0, pl.ds(start, t), :]
            vt = vt * keep[hh] + one_at[hh]
            acc_ref[hh] += jnp.dot(p_ref[hh], vt, preferred_element_type=F32)

    def body(j, carry):
        tile(j, False)
        return carry

    lax.fori_loop(j0, qi, body, 0)
    tile(qi, True)
    outs = []
    for hh in range(2):
        acc = acc_ref[hh]
        w = jnp.where(feat[hh], 1.0 / FOX_HEAD_DIM, jnp.where(sum_l[hh], RMS_EPS, 0.0))
        outs.append(acc * lax.rsqrt(jnp.sum(acc * acc * w, axis=-1, keepdims=True)))
    o_ref[0] = (jnp.where(lo, outs[0], outs[1]) * g_ref[...]).astype(BF16)


def _fox_first_tiles(stats):
    qn, kn, cs, ce = (stats[:, :, i, :FOX_HEADS].transpose(0, 2, 1) for i in range(4))
    bound = cs[..., :, None] - ce[..., None, :] + qn[..., :, None] * (kn[..., None, :] + kn[..., :, None])
    nt = qn.shape[-1]
    ti = jnp.arange(nt)[:, None]
    tj = jnp.arange(nt)[None, :]
    need = (tj == ti) | ((tj < ti) & (bound >= -SKIP_LOG2))
    return jnp.argmax(need, axis=-1).astype(I32).reshape(-1)


def _fox_pairing(b_f):
    even = jnp.argsort(b_f[0::2]) * 2
    odd = jnp.argsort(b_f[1::2]) * 2 + 1
    return jnp.stack([even, odd], axis=1).reshape(-1).astype(I32)


def _fox(first, heads, a3, ka, gain, t, rc):
    B, S, _ = a3.shape
    qblk = lambda side: pl.BlockSpec(
        (1, t, HEAD_BLK), lambda b, hp, qi, js, hd: (b, qi, QBLK0 + hd[2 * hp + side] // 2))
    kblk = lambda side: pl.BlockSpec(
        (1, S, HEAD_BLK), lambda b, hp, qi, js, hd: (b, 0, hd[2 * hp + side]))
    vblk = lambda side: pl.BlockSpec(
        (1, S, HEAD_BLK), lambda b, hp, qi, js, hd: (b, 0, VBLK0 + hd[2 * hp + side] // 2))
    grid_spec = pltpu.PrefetchScalarGridSpec(
        num_scalar_prefetch=2,
        grid=(B, FOX_HEADS // 2, S // t),
        in_specs=[
            qblk(0), qblk(1), kblk(0), kblk(1), vblk(0), vblk(1),
            pl.BlockSpec((1, LANES), lambda b, hp, qi, js, hd: (0, hp)),
        ],
        out_specs=pl.BlockSpec((1, t, LANES), lambda b, hp, qi, js, hd: (b, qi, hp)),
        scratch_shapes=[pltpu.VMEM((2, t, t), F32), pltpu.VMEM((2, t, t), BF16),
                        pltpu.VMEM((2, t, LANES), F32), pltpu.VMEM((2, t, LANES), F32)],
    )
    return pl.pallas_call(
        functools.partial(_fox_body, t=t, rc=rc),
        grid_spec=grid_spec,
        out_shape=jax.ShapeDtypeStruct((B, S, FOX_WIDTH), BF16),
        compiler_params=_cparams(("parallel", "parallel", "arbitrary")),
        name="fox_attention",
    )(first, heads, a3, a3, ka, ka, a3, a3, gain)


def _hgrn_body(q_ref, f_ref, v_ref, gg_ref, lb_ref, gain_ref, t64_ref, hm_ref, o_ref, st_ref, o_acc, *, tc):
    @pl.when(pl.program_id(2) == 0)
    def _():
        st_ref[...] = jnp.zeros_like(st_ref)

    C, SB = HG_CHUNK, HG_SUB
    lane = lax.broadcasted_iota(I32, (1, LANES), 1)
    lo = lane < HG_DK
    lb = lb_ref[...]
    fl = f_ref[0]
    t2 = jnp.log1p(-lb) + _log_sigmoid(fl)
    t1 = jnp.where(lb > 0.0, jnp.log(jnp.maximum(lb, 1e-37)), NEG_BIG)
    logf = jnp.maximum(t1, t2) + jnp.log1p(jnp.exp(-jnp.abs(t1 - t2)))
    kk = (1.0 - lb) / (1.0 + jnp.exp(fl))
    q = q_ref[0].astype(F32)
    nch = tc // C
    wide = jnp.concatenate([logf[ch * C:(ch + 1) * C] for ch in range(nch)], axis=1)
    a_wide = _tri_dot(t64_ref[...], wide)
    a = jnp.concatenate([a_wide[:, ch * LANES:(ch + 1) * LANES] for ch in range(nch)], axis=0)
    qa = (q * jnp.exp(a)).astype(BF16)

    rowblk = lax.broadcasted_iota(I32, (C, 1), 0) // SB
    r64 = lax.broadcasted_iota(I32, (C, C), 0)
    c64 = lax.broadcasted_iota(I32, (C, C), 1)
    causal = c64 <= r64
    rr = lax.broadcasted_iota(I32, (LANES, LANES), 0)
    cc = lax.broadcasted_iota(I32, (LANES, LANES), 1)
    same_head = (rr < HG_DK) == (cc < HG_DK)
    nt = (((1,), (1,)), ((), ()))
    tn = (((0,), (0,)), ((), ()))

    st = st_ref[...]
    for ch in range(tc // C):
        sl = slice(ch * C, (ch + 1) * C)
        a_c, kk_c, v_c = a[sl], kk[sl], v_ref[0, sl, :]
        a_last = a_c[C - 1:C, :]
        rows0, rows1 = [], []
        for blk in range(C // SB):
            b0 = ch * C + blk * SB
            e_blk = a_c[blk * SB - 1:blk * SB, :] if blk else jnp.zeros((1, LANES), F32)
            live = rowblk <= blk
            k_blk = jnp.where(live, kk_c * jnp.exp(jnp.where(live, e_blk - a_c, 0.0)), 0.0).astype(BF16)
            q_blk = q[b0:b0 + SB, :] * jnp.exp(a_c[blk * SB:(blk + 1) * SB, :] - e_blk)
            lhs = jnp.concatenate([jnp.where(lo, q_blk, 0.0), jnp.where(lo, 0.0, q_blk)], axis=0).astype(BF16)
            sc = lax.dot_general(lhs, k_blk, nt, preferred_element_type=F32)
            rows0.append(sc[0:SB])
            rows1.append(sc[SB:2 * SB])
        s0 = jnp.where(causal, jnp.concatenate(rows0, axis=0), 0.0).astype(BF16)
        s1 = jnp.where(causal, jnp.concatenate(rows1, axis=0), 0.0).astype(BF16)
        intra = jnp.where(lo, jnp.dot(s0, v_c, preferred_element_type=F32),
                          jnp.dot(s1, v_c, preferred_element_type=F32))
        inter = lax.dot_general(qa[sl], st.astype(BF16), nt, preferred_element_type=F32)
        o = inter + intra
        kd = (kk_c * jnp.exp(a_last - a_c)).astype(BF16)
        upd = lax.dot_general(v_c, kd, tn, preferred_element_type=F32)
        st = jnp.where(same_head, st * jnp.exp(a_last) + upd, 0.0)
        o_acc[sl, :] = o
    st_ref[...] = st

    o = o_acc[...]
    sq = o * o
    hi = sq.astype(BF16)
    lo_t = (sq - hi.astype(F32)).astype(BF16)
    ms = (jnp.dot(hi, hm_ref[...], preferred_element_type=F32)
          + jnp.dot(lo_t, hm_ref[...], preferred_element_type=F32))
    g = gg_ref[0].astype(F32)
    silu = g / (1.0 + jnp.exp(-g))
    o_ref[0] = (o * lax.rsqrt(ms + RMS_EPS) * gain_ref[...] * silu).astype(BF16)


def _hgrn(a3, b3, lb, gain, tri, head_mean, tc):
    B, S, _ = a3.shape
    blk = lambda c0: pl.BlockSpec((1, tc, LANES), lambda b, hp, s: (b, s, c0 + hp))
    row = pl.BlockSpec((1, LANES), lambda b, hp, s: (0, hp))
    const = lambda n: pl.BlockSpec((n, n), lambda b, hp, s: (0, 0))
    return pl.pallas_call(
        functools.partial(_hgrn_body, tc=tc),
        grid=(B, HG_HEADS // 2, S // tc),
        in_specs=[blk(GQ0), blk(0), blk(GI0), blk(GG0), row, row, const(HG_CHUNK), const(LANES)],
        out_specs=pl.BlockSpec((1, tc, LANES), lambda b, hp, s: (b, s, hp)),
        out_shape=jax.ShapeDtypeStruct((B, S, HG_WIDTH), BF16),
        scratch_shapes=[pltpu.VMEM((LANES, LANES), F32), pltpu.VMEM((tc, LANES), F32)],
        compiler_params=_cparams(("parallel", "parallel", "arbitrary")),
        name="hgrn2",
    )(a3, b3, a3, a3, lb, gain, tri, head_mean)


def _pool_body(u_ref, w_ref, sc_ref, o_ref, ext_ref, *, tc):
    si = pl.program_id(1)

    @pl.when(si == 0)
    def _():
        ext_ref[0:POOL_HALO, :] = jnp.zeros((POOL_HALO, POOL_WIDTH), F32)

    u = u_ref[0].astype(F32)
    ext_ref[POOL_HALO:POOL_HALO + tc, :] = u
    acc = u
    sums = {}
    for j in range(1, max(POOL_WINDOWS)):
        acc = acc + ext_ref[POOL_HALO - j:POOL_HALO - j + tc, :]
        if j + 1 in POOL_WINDOWS:
            sums[j + 1] = acc
    lane = lax.broadcasted_iota(I32, (1, POOL_WIDTH), 1)
    grp = lane // POOL_GROUP_DIM
    win = sums[POOL_WINDOWS[-1]]
    wlen = jnp.full((1, POOL_WIDTH), float(POOL_WINDOWS[-1]), F32)
    for gi in range(len(POOL_WINDOWS) - 2, -1, -1):
        win = jnp.where(grp == gi, sums[POOL_WINDOWS[gi]], win)
        wlen = jnp.where(grp == gi, float(POOL_WINDOWS[gi]), wlen)
    pos = (si * tc + lax.broadcasted_iota(I32, (tc, 1), 0) + 1).astype(F32)
    pooled = win / jnp.minimum(pos, wlen) - u
    y = jnp.dot(pooled.astype(BF16), w_ref[...], preferred_element_type=F32) * sc_ref[...]
    o_ref[0] = y.astype(BF16)
    ext_ref[0:POOL_HALO, :] = u[tc - POOL_HALO:tc, :]


def _pool(a3, wbd, scale, tc):
    B, S, _ = a3.shape
    return pl.pallas_call(
        functools.partial(_pool_body, tc=tc),
        grid=(B, S // tc),
        in_specs=[
            pl.BlockSpec((1, tc, POOL_WIDTH), lambda b, s: (b, s, PU_BLK256)),
            pl.BlockSpec((POOL_WIDTH, POOL_WIDTH), lambda b, s: (0, 0)),
            pl.BlockSpec((1, POOL_WIDTH), lambda b, s: (0, 0)),
        ],
        out_specs=pl.BlockSpec((1, tc, POOL_WIDTH), lambda b, s: (b, s, 0)),
        out_shape=jax.ShapeDtypeStruct((B, S, POOL_WIDTH), BF16),
        scratch_shapes=[pltpu.VMEM((POOL_HALO + tc, POOL_WIDTH), F32)],
        compiler_params=_cparams(("parallel", "arbitrary")),
        name="pool_mixer",
    )(a3, wbd, scale)


def _router_body(of_ref, oh_ref, op_ref, h_ref, wo_ref, g_ref, wr_ref, br_ref, tri_ref,
                 hn_ref, ri_ref, rg_ref, cnt_ref, carry_ref, *, tm):
    @pl.when(pl.program_id(0) == 0)
    def _():
        carry_ref[...] = jnp.zeros_like(carry_ref)

    f0, f1 = FOX_WIDTH, FOX_WIDTH + HG_WIDTH
    mix = (jnp.dot(of_ref[...], wo_ref[0:f0, :], preferred_element_type=F32)
           + jnp.dot(oh_ref[...], wo_ref[f0:f1, :], preferred_element_type=F32)
           + jnp.dot(op_ref[...], wo_ref[f1:f1 + POOL_WIDTH, :], preferred_element_type=F32))
    hn = h_ref[...] + mix
    hn_ref[...] = hn
    ms = jnp.mean(hn * hn, axis=-1, keepdims=True)
    xn = hn * lax.rsqrt(ms + RMS_EPS) * g_ref[...]
    x1 = xn.astype(BF16)
    x2 = (xn - x1.astype(F32)).astype(BF16)
    d = lambda a, b: jnp.dot(a, b, preferred_element_type=F32)
    logits = d(x1, wr_ref[0]) + d(x1, wr_ref[1]) + d(x2, wr_ref[0]) + br_ref[...]

    lanef = lax.broadcasted_iota(I32, (tm, LANES), 1).astype(F32)
    l = logits
    vals, idxs, sels = [], [], []
    for _ in range(TOP_K):
        mk = jnp.max(l, axis=-1, keepdims=True)
        ik = jnp.min(jnp.where(l == mk, lanef, float(LANES)), axis=-1, keepdims=True)
        sk = lanef == ik
        vals.append(mk)
        idxs.append(ik)
        sels.append(sk)
        l = jnp.where(sk, -jnp.inf, l)
    es = [jnp.exp(v - vals[0]) for v in vals]
    den = es[0] + es[1] + es[2] + es[3]
    sel = jnp.zeros((tm, LANES), F32)
    for sk in sels:
        sel = jnp.where(sk, 1.0, sel)
    rank = jnp.dot(tri_ref[...], sel.astype(BF16), preferred_element_type=F32) + carry_ref[...]
    carry = carry_ref[...] + jnp.sum(sel, axis=0, keepdims=True)
    carry_ref[...] = carry
    cnt_ref[...] = carry
    ri = jnp.zeros((tm, LANES), F32)
    rg = jnp.zeros((tm, LANES), F32)
    for k in range(TOP_K):
        pos_k = jnp.sum(jnp.where(sels[k], rank, 0.0), axis=-1, keepdims=True)
        ri = jnp.where(lanef == float(k), idxs[k], ri)
        ri = jnp.where(lanef == float(TOP_K + k), pos_k, ri)
        rg = jnp.where(lanef == float(k), es[k] / den, rg)
    ri_ref[...] = ri.astype(I32)
    rg_ref[...] = rg


def _router(o_fox, o_hg, o_pool, h, wo, g, wr, br, tri, tm):
    T, D = h.shape
    rowblk = lambda w: pl.BlockSpec((tm, w), lambda i: (i, 0))
    const = lambda shp: pl.BlockSpec(shp, lambda i: tuple(0 for _ in shp))
    return pl.pallas_call(
        functools.partial(_router_body, tm=tm),
        grid=(T // tm,),
        in_specs=[rowblk(FOX_WIDTH), rowblk(HG_WIDTH), rowblk(POOL_WIDTH), rowblk(D),
                  const((D, D)), const((1, D)), const((2, D, LANES)), const((1, LANES)), const((tm, tm))],
        out_specs=[rowblk(D), rowblk(LANES), rowblk(LANES), const((1, LANES))],
        out_shape=[jax.ShapeDtypeStruct((T, D), F32), jax.ShapeDtypeStruct((T, LANES), I32),
                   jax.ShapeDtypeStruct((T, LANES), F32), jax.ShapeDtypeStruct((1, LANES), F32)],
        scratch_shapes=[pltpu.VMEM((1, LANES), F32)],
        compiler_params=_cparams(("arbitrary",)),
        name="outproj_router",
    )(o_fox, o_hg, o_pool, h, wo, g, wr, br, tri)


def _to_row_tiles(ref, x):
    ref[...] = pltpu.einshape("r(cl)->rcl", x, c=ref.shape[-2])


def _from_row_tiles(ref):
    return pltpu.einshape("rcl->r(cl)", ref[...])


def _row_copy(src_ref, src_row, dst_ref, dst_row, sem):
    return pltpu.make_async_copy(src_ref.at[pl.ds(src_row, 1)], dst_ref.at[pl.ds(dst_row, 1)], sem)


def _wait_rows(tile_ref, sem, times):
    for _ in range(times):
        pltpu.make_async_copy(tile_ref, tile_ref, sem).wait()


def _dispatch_body(pad0_ref, padn_ref, dst_ref, h_ref, g_ref, buf_ref, xn_ref, zero_ref, sem, zsem, *, tt, bm):
    i = pl.program_id(0)
    n = pl.num_programs(0)
    slot = i % 2
    stage = xn_ref.at[slot]

    @pl.when(i >= 2)
    def _():
        _wait_rows(stage, sem.at[slot], TOP_K)

    x = h_ref[...]
    ms = jnp.mean(x * x, axis=-1, keepdims=True)
    _to_row_tiles(stage, x * lax.rsqrt(ms + RMS_EPS) * g_ref[...])

    def issue(r, _):
        for k in range(TOP_K):
            _row_copy(stage, r, buf_ref, dst_ref[0, 0, r * TOP_K + k], sem.at[slot]).start(priority=k % 2)
        return 0

    lax.fori_loop(0, tt, issue, 0, unroll=DMA_UNROLL)

    @pl.when(i == 0)
    def _():
        zero_ref[...] = jnp.zeros_like(zero_ref)
        half = bm // 2
        n_exp = pad0_ref.shape[0] - 1

        def run(cp, pred, wait):
            @pl.when(pred)
            def _():
                cp.wait() if wait else cp.start()

        def pad_copies(e, wait):
            off, cnt = pad0_ref[e], padn_ref[e]
            for r in range(SUBLANES - 1):
                run(_row_copy(zero_ref, 0, buf_ref, off + r, zsem), r < (cnt & (SUBLANES - 1)), wait)
            off = off + (cnt & (SUBLANES - 1))
            sz = SUBLANES
            while sz <= half:
                take = (cnt & sz) != 0
                dst = buf_ref.at[pl.ds(pl.multiple_of(off, SUBLANES), sz)]
                run(pltpu.make_async_copy(zero_ref.at[pl.ds(0, sz)], dst, zsem), take, wait)
                off = off + jnp.where(take, sz, 0)
                sz *= 2
            return 0

        def tail_copy(c, wait):
            dst = buf_ref.at[pl.ds(pl.multiple_of(pad0_ref[n_exp] + c * half, SUBLANES), half)]
            cp = pltpu.make_async_copy(zero_ref, dst, zsem)
            cp.wait() if wait else cp.start()
            return 0

        n_tail = padn_ref[n_exp] // half
        for wait in (False, True):
            lax.fori_loop(0, n_exp, lambda e, c, w=wait: pad_copies(e, w), 0)
            lax.fori_loop(0, n_tail, lambda c, _, w=wait: tail_copy(c, w), 0)

    @pl.when(i == n - 1)
    def _():
        _wait_rows(stage, sem.at[slot], TOP_K)

    @pl.when((i == n - 1) & (n >= 2))
    def _():
        _wait_rows(xn_ref.at[1 - slot], sem.at[1 - slot], TOP_K)


def _dispatch(pad0, padn, dst3, hn, g, R, tt, bm):
    T, D = hn.shape
    grid_spec = pltpu.PrefetchScalarGridSpec(
        num_scalar_prefetch=2,
        grid=(T // tt,),
        in_specs=[
            pl.BlockSpec((1, 1, tt * TOP_K), lambda i, p0, pn: (i, 0, 0), memory_space=pltpu.SMEM),
            pl.BlockSpec((tt, D), lambda i, p0, pn: (i, 0)),
            pl.BlockSpec((1, D), lambda i, p0, pn: (0, 0)),
        ],
        out_specs=pl.BlockSpec(memory_space=pl.ANY),
        scratch_shapes=[pltpu.VMEM((2, tt, D // LANES, LANES), F32), pltpu.VMEM((bm // 2, D // LANES, LANES), F32),
                        pltpu.SemaphoreType.DMA((2,)), pltpu.SemaphoreType.DMA],
    )
    return pl.pallas_call(
        functools.partial(_dispatch_body, tt=tt, bm=bm),
        grid_spec=grid_spec,
        out_shape=jax.ShapeDtypeStruct((R, D // LANES, LANES), F32),
        compiler_params=_cparams(("arbitrary",)),
        name="moe_dispatch",
    )(pad0, padn, dst3, hn, g)


MXU_DIM = 256


def _expert_body(te_ref, nu_ref, x_ref, w1_ref, w2_ref, perm_ref, b1g_ref, b1l_ref, b2_ref, y_ref,
                 w1g_scr, w1l_scr, w2_scr):
    i = pl.program_id(0)
    live = i < nu_ref[0]
    prev = te_ref[jnp.maximum(i - 1, 0)]

    @pl.when(live & ((i == 0) | (te_ref[i] != prev)))
    def _():
        half = MXU_DIM // 2
        for grp in range(w1_ref.shape[-1] // MXU_DIM):
            w = w1_ref[0, :, grp * MXU_DIM:(grp + 1) * MXU_DIM].astype(BF16)
            r = jnp.dot(w, perm_ref[...], preferred_element_type=F32).astype(BF16)
            w1g_scr[:, grp * half:(grp + 1) * half] = r[:, :half]
            w1l_scr[:, grp * half:(grp + 1) * half] = r[:, half:]
        w2_scr[...] = w2_ref[0].astype(BF16)

    @pl.when(live)
    def _():
        x = _from_row_tiles(x_ref).astype(BF16)
        glu = jnp.dot(x, w1g_scr[...], preferred_element_type=F32) + b1g_ref[0]
        lin = jnp.dot(x, w1l_scr[...], preferred_element_type=F32) + b1l_ref[0]
        glu = jnp.minimum(glu, SWIGLU_LIMIT)
        lin = jnp.clip(lin, -SWIGLU_LIMIT, SWIGLU_LIMIT)
        act = glu / (1.0 + jnp.exp(-SWIGLU_ALPHA * glu)) * (lin + 1.0)
        _to_row_tiles(y_ref, jnp.dot(act.astype(BF16), w2_scr[...], preferred_element_type=F32) + b2_ref[0])

    @pl.when(jnp.logical_not(live))
    def _():
        y_ref[...] = jnp.zeros_like(y_ref)


def _experts(tile_e, n_used, buf, w1, w2, b1g, b1l, b2, bm, first_expert):
    R, dsub, _ = buf.shape
    D = dsub * LANES
    F2 = w1.shape[-1]
    F = F2 // 2
    half = MXU_DIM // 2
    pi = jnp.arange(MXU_DIM)[:, None]
    pj = jnp.arange(MXU_DIM)[None, :]
    perm = jnp.where(pj < half, pi == 2 * pj, pi == 2 * (pj - half) + 1).astype(BF16)
    rows = pl.BlockSpec((bm, dsub, LANES), lambda i, te, nu: (jnp.minimum(i, nu[0] - 1), 0, 0))
    wspec = lambda a, b, e0=0: pl.BlockSpec((1, a, b), lambda i, te, nu: (e0 + te[i], 0, 0))
    grid_spec = pltpu.PrefetchScalarGridSpec(
        num_scalar_prefetch=2,
        grid=(R // bm,),
        in_specs=[rows, wspec(D, F2, first_expert), wspec(F, D, first_expert),
                  pl.BlockSpec((MXU_DIM, MXU_DIM), lambda i, te, nu: (0, 0)),
                  wspec(1, F), wspec(1, F), wspec(1, D)],
        out_specs=pl.BlockSpec((bm, dsub, LANES), lambda i, te, nu: (i, 0, 0)),
        scratch_shapes=[pltpu.VMEM((D, F), BF16), pltpu.VMEM((D, F), BF16), pltpu.VMEM((F, D), BF16)],
    )
    return pl.pallas_call(
        _expert_body,
        grid_spec=grid_spec,
        out_shape=jax.ShapeDtypeStruct((R, dsub, LANES), F32),
        compiler_params=_cparams(("arbitrary",), vmem=EXPERT_VMEM_LIMIT),
        name="moe_experts",
    )(tile_e, n_used, buf, w1, w2, perm, b1g, b1l, b2)


def _combine_body(dst_ref, h_ref, rg_ref, gf_ref, y_ref, o_ref, rows_ref, sem, *, tt, final_norm):
    s = pl.program_id(0)
    n_tiles = pl.num_programs(0) - 1

    @pl.when(s < n_tiles)
    def _():
        slot = s % 2

        def issue(r, _):
            for k in range(TOP_K):
                _row_copy(y_ref, dst_ref[0, 0, r * TOP_K + k], rows_ref.at[slot, k], r,
                          sem.at[slot]).start(priority=k % 2)
            return 0

        lax.fori_loop(0, tt, issue, 0, unroll=DMA_UNROLL)

    @pl.when(s >= 1)
    def _():
        slot = (s - 1) % 2
        _wait_rows(rows_ref.at[slot, 0], sem.at[slot], TOP_K)
        out = h_ref[...]
        gates = rg_ref[...]
        for k in range(TOP_K):
            out = out + gates[:, k:k + 1] * _from_row_tiles(rows_ref.at[slot, k])
        if final_norm:
            ms = jnp.mean(out * out, axis=-1, keepdims=True)
            out = out * lax.rsqrt(ms + RMS_EPS) * gf_ref[...]
        o_ref[...] = out


def _combine(dst3, hn, rg, g_final, ybuf, tt, final_norm):
    T, D = hn.shape
    n_tiles = T // tt
    prev = lambda s: (jnp.maximum(s - 1, 0), 0)
    return pl.pallas_call(
        functools.partial(_combine_body, tt=tt, final_norm=final_norm),
        grid=(n_tiles + 1,),
        in_specs=[
            pl.BlockSpec((1, 1, tt * TOP_K), lambda s: (jnp.minimum(s, n_tiles - 1), 0, 0), memory_space=pltpu.SMEM),
            pl.BlockSpec((tt, D), prev),
            pl.BlockSpec((tt, LANES), prev),
            pl.BlockSpec((1, D), lambda s: (0, 0)),
            pl.BlockSpec(memory_space=pl.ANY),
        ],
        out_specs=pl.BlockSpec((tt, D), prev),
        out_shape=jax.ShapeDtypeStruct((T, D), F32),
        scratch_shapes=[pltpu.VMEM((2, TOP_K, tt, D // LANES, LANES), F32), pltpu.SemaphoreType.DMA((2,))],
        compiler_params=_cparams(("arbitrary",)),
        name="moe_combine",
    )(dst3, hn, rg, g_final, ybuf)


def _tiles(B, S):
    T = B * S
    ts = min(512, S)
    tm = min(512, T)
    tt = min(256, T)
    bm = min(512, T)
    return ts, tm, tt, bm


def _block_tri(n, blk, strict=False):
    r = jnp.arange(n)[:, None]
    c = jnp.arange(n)[None, :]
    keep = (c < r) if strict else (c <= r)
    return (keep & (r // blk == c // blk)).astype(BF16)


def kernel(x, norm_mix, w_in, b_fox_f, fox_gain, hgrn_lb, hgrn_gain, pool_w, pool_scale, w_out,
           norm_moe, w_router, b_router, w1, b1, w2, b2, norm_final):
    B, S, D = x.shape
    L = w_in.shape[0]
    T = B * S
    E = w1.shape[1]
    ts, tm, tt, bm = _tiles(B, S)

    offs = [0]
    for wdt in (FOX_WIDTH, FOX_WIDTH, FOX_WIDTH, FOX_HEADS, HG_WIDTH, HG_WIDTH, HG_WIDTH, HG_WIDTH, POOL_WIDTH):
        offs.append(offs[-1] + wdt)
    col = lambda i: w_in[:, :, offs[i]:offs[i + 1]]
    scale = FOX_HEAD_DIM ** -0.5 * LOG2E
    wa = jnp.concatenate([col(0) * scale, col(1), col(2), col(4), col(6), col(7), col(8)], axis=-1).astype(BF16)
    wb = jnp.concatenate([col(5), col(3), jnp.zeros((L, D, LANES - FOX_HEADS), F32)], axis=-1).astype(BF16)
    bf_pad = jnp.pad(b_fox_f, ((0, 0), (0, LANES - FOX_HEADS)))
    p = jax.nn.softmax(hgrn_lb.astype(F32), axis=0)
    cs = jnp.cumsum(p, axis=0)
    lbs = cs - cs[0:1]
    eye = jnp.eye(len(POOL_WINDOWS), dtype=F32)
    pool_bd = jnp.einsum("lgcd,gh->lgchd", pool_w, eye).reshape(L, POOL_WIDTH, POOL_WIDTH).astype(BF16)
    wr_pad = jnp.pad(w_router, ((0, 0), (0, 0), (0, LANES - E)))
    wr_hi = wr_pad.astype(BF16)
    wr_lo = (wr_pad - wr_hi.astype(F32)).astype(BF16)
    wr = jnp.stack([wr_hi, wr_lo], axis=1)
    br_pad = jnp.pad(b_router, ((0, 0), (0, LANES - E)), constant_values=NEG_BIG)
    w1_all = w1.reshape((L * E,) + w1.shape[2:])
    w2_all = w2.reshape((L * E,) + w2.shape[2:])
    b1g = b1[..., 0::2][:, :, None, :]
    b1l = b1[..., 1::2][:, :, None, :]
    b2r = b2[:, :, None, :]

    tri_seq = _block_tri(ts, ts)
    tri64 = _block_tri(HG_CHUNK, HG_CHUNK)
    li = jnp.arange(LANES)
    head_mean = ((li[:, None] // HG_DK == li[None, :] // HG_DK) / HG_DK).astype(BF16)
    tri_tok = _block_tri(tm, tm, strict=True)

    M = T * TOP_K
    R = -(-M // bm) * bm + E * bm
    NB = R // bm

    h = x.reshape(T, D)
    for l in range(L):
        a2, b2d = _inproj(h, norm_mix[l][None], wa[l], wb[l], tm)
        a3 = a2.reshape(B, S, A_COLS)
        b3 = b2d.reshape(B, S, B_COLS)
        ka, stats = _foxprep(a3, b3, bf_pad[l][None], tri_seq, ts)
        heads = _fox_pairing(b_fox_f[l])
        by_head = lambda w: w.reshape((FOX_HEADS, FOX_HEAD_DIM) + w.shape[1:])[heads].reshape(w.shape)
        o_fox = _fox(_fox_first_tiles(stats), heads, a3, ka, by_head(fox_gain[l])[None], ts, min(32, ts))
        wo_l = jnp.concatenate([by_head(w_out[l, :FOX_WIDTH]), w_out[l, FOX_WIDTH:]], axis=0).astype(BF16)
        o_hg = _hgrn(a3, b3, lbs[l][None], hgrn_gain[l][None], tri64, head_mean, ts)
        o_pool = _pool(a3, pool_bd[l], pool_scale[l][None], ts)
        hn, ri, rg, cnt = _router(o_fox.reshape(T, FOX_WIDTH), o_hg.reshape(T, HG_WIDTH),
                                  o_pool.reshape(T, POOL_WIDTH), h, wo_l, norm_moe[l][None],
                                  wr[l], br_pad[l][None], tri_tok, tm)
        counts = cnt[0, :E].astype(I32)
        padded = (counts + bm - 1) // bm * bm
        pend = jnp.cumsum(padded)
        pstart = pend - padded
        first_row = jnp.sum(jnp.where(ri[:, 0:TOP_K, None] == jnp.arange(E, dtype=I32), pstart.astype(I32), 0), axis=-1)
        dst = first_row + ri[:, TOP_K:2 * TOP_K]
        n_used = (pend[-1] // bm).astype(I32)
        tile_ids = jnp.minimum(jnp.arange(NB, dtype=I32), n_used - 1)
        tile_e = jnp.minimum(jnp.sum(pend[None, :] <= (tile_ids * bm)[:, None], axis=1), E - 1).astype(I32)
        dst3 = dst.reshape(T // tt, 1, tt * TOP_K)
        pad0 = jnp.concatenate([pstart + counts, pend[-1:]]).astype(I32)
        padn = jnp.concatenate([padded - counts, R - pend[-1:]]).astype(I32)
        buf = _dispatch(pad0, padn, dst3, hn, norm_moe[l][None], R, tt, bm)
        ybuf = _experts(tile_e, n_used.reshape(1), buf, w1_all, w2_all, b1g[l], b1l[l], b2r[l], bm, l * E)
        h = _combine(dst3, hn, rg, norm_final[None], ybuf, tt, final_norm=(l == L - 1))
    return h.reshape(B, S, D)
```

```python
import functools

import jax
import jax.numpy as jnp
from jax import lax
from jax.experimental import pallas as pl
from jax.experimental.pallas import tpu as pltpu

F32 = jnp.float32
BF16 = jnp.bfloat16
I32 = jnp.int32

LANES = 128
SUBLANES = 8
VMEM_LIMIT = 56 * 1024 * 1024
EXPERT_VMEM_LIMIT = 60 * 1024 * 1024

FOX_HEADS = 8
FOX_HEAD_DIM = 64
FOX_WIDTH = FOX_HEADS * FOX_HEAD_DIM
HG_HEADS = 4
HG_DK = 64
HG_WIDTH = HG_HEADS * HG_DK
HG_CHUNK = 64
HG_SUB = 16
POOL_WINDOWS = (2, 4, 8, 16)
POOL_GROUP_DIM = 64
POOL_WIDTH = len(POOL_WINDOWS) * POOL_GROUP_DIM
POOL_HALO = 16
N_EXPERTS = 32
TOP_K = 4
SWIGLU_LIMIT = 7.0
SWIGLU_ALPHA = 1.702
RMS_EPS = 1e-5
NEG_BIG = -1e30
DMA_UNROLL = 8

HEAD_BLK = LANES
QBLK0, KBLK0, VBLK0 = 0, FOX_HEADS // 2, FOX_HEADS
GQ0 = 3 * FOX_HEADS // 2
GI0, GG0 = GQ0 + 2, GQ0 + 4
PU_BLK256 = (GQ0 + 6) // 2
A_COLS = (GQ0 + 8) * LANES
BIAS_TERMS = 3


def _head_lanes(parity):
    lane = lax.broadcasted_iota(I32, (1, LANES), 1)
    lower = lane < FOX_HEAD_DIM
    return (lower, FOX_HEAD_DIM) if parity == 0 else (jnp.logical_not(lower), 0)


LOG2E = 1.4426950408889634
SKIP_LOG2 = 150.0
NORM_SLACK = 1.01
B_COLS = 384


def _cparams(sem, vmem=VMEM_LIMIT):
    return pltpu.CompilerParams(dimension_semantics=sem, vmem_limit_bytes=vmem)


def _split3(x):
    x1 = x.astype(BF16)
    r1 = x - x1.astype(F32)
    x2 = r1.astype(BF16)
    x3 = (r1 - x2.astype(F32)).astype(BF16)
    return x1, x2, x3


def _tri_dot(tri, x):
    x1, x2, x3 = _split3(x)
    d = lambda a: jnp.dot(tri, a, preferred_element_type=F32)
    return d(x1) + d(x2) + d(x3)


def _log_sigmoid(z):
    return jnp.minimum(z, 0.0) - jnp.log1p(jnp.exp(-jnp.abs(z)))


def _inproj_body(x_ref, g_ref, wa_ref, wb_ref, a_ref, b_ref):
    x = x_ref[...]
    ms = jnp.mean(x * x, axis=-1, keepdims=True)
    xn = (x * lax.rsqrt(ms + RMS_EPS) * g_ref[...]).astype(BF16)
    a_ref[...] = jnp.dot(xn, wa_ref[...], preferred_element_type=F32).astype(BF16)
    b_ref[...] = jnp.dot(xn, wb_ref[...], preferred_element_type=F32)


def _inproj(h, g, wa, wb, tm):
    T, D = h.shape
    return pl.pallas_call(
        _inproj_body,
        grid=(T // tm,),
        in_specs=[
            pl.BlockSpec((tm, D), lambda i: (i, 0)),
            pl.BlockSpec((1, D), lambda i: (0, 0)),
            pl.BlockSpec((D, A_COLS), lambda i: (0, 0)),
            pl.BlockSpec((D, B_COLS), lambda i: (0, 0)),
        ],
        out_specs=[
            pl.BlockSpec((tm, A_COLS), lambda i: (i, 0)),
            pl.BlockSpec((tm, B_COLS), lambda i: (i, 0)),
        ],
        out_shape=[jax.ShapeDtypeStruct((T, A_COLS), BF16), jax.ShapeDtypeStruct((T, B_COLS), F32)],
        compiler_params=_cparams(("parallel",)),
        name="inproj",
    )(h, g, wa, wb)


def _foxprep_body(ff_ref, bf_ref, tri_ref, hsel_ref, q_ref, k_ref, ka_ref, st_ref, carry_ref, *, ts):
    @pl.when(pl.program_id(1) == 0)
    def _():
        carry_ref[...] = jnp.zeros_like(carry_ref)

    logf = _log_sigmoid(ff_ref[0] + bf_ref[...])
    c = _tri_dot(tri_ref[...], logf) + carry_ref[...]
    carry_ref[...] = c[ts - 1:ts, :]
    cl = c * LOG2E
    lane = lax.broadcasted_iota(I32, (1, LANES), 1)
    for h in range(FOX_HEADS):
        feat, spare0 = _head_lanes(h % 2)
        terms = _split3(jnp.broadcast_to(-cl[:, h:h + 1], (ts, LANES)))
        pair = k_ref[0, :, (h // 2) * HEAD_BLK:(h // 2 + 1) * HEAD_BLK]
        ka = pair * feat.astype(BF16)
        for i, term in enumerate(terms):
            ka = jnp.where(lane == spare0 + i, term, ka)
        ka_ref[0, :, h * HEAD_BLK:(h + 1) * HEAD_BLK] = ka

    def max_norm(ref):
        x = ref[0].astype(F32)
        n2 = jnp.dot((x * x).astype(BF16), hsel_ref[...], preferred_element_type=F32)
        return jnp.sqrt(jnp.max(n2, axis=0, keepdims=True)) * NORM_SLACK

    qn_row = max_norm(q_ref)
    kn_row = max_norm(k_ref)
    rowi = lax.broadcasted_iota(I32, (8, LANES), 0)
    st = jnp.where(rowi == 0, qn_row, jnp.where(rowi == 1, kn_row, jnp.where(rowi == 2, cl[0:1, :],
                   jnp.where(rowi == 3, cl[ts - 1:ts, :], 0.0))))
    st_ref[0, 0] = st


def _foxprep(a3, b3, bf_pad, tri, ts):
    B, S, _ = b3.shape
    hw = FOX_HEADS * HEAD_BLK
    head_sel = (jnp.arange(FOX_WIDTH)[:, None] // FOX_HEAD_DIM == jnp.arange(LANES)[None, :]).astype(BF16)
    return pl.pallas_call(
        functools.partial(_foxprep_body, ts=ts),
        grid=(B, S // ts),
        in_specs=[
            pl.BlockSpec((1, ts, LANES), lambda b, s: (b, s, 2)),
            pl.BlockSpec((1, LANES), lambda b, s: (0, 0)),
            pl.BlockSpec((ts, ts), lambda b, s: (0, 0)),
            pl.BlockSpec((FOX_WIDTH, LANES), lambda b, s: (0, 0)),
            pl.BlockSpec((1, ts, FOX_WIDTH), lambda b, s: (b, s, QBLK0 * LANES // FOX_WIDTH)),
            pl.BlockSpec((1, ts, FOX_WIDTH), lambda b, s: (b, s, KBLK0 * LANES // FOX_WIDTH)),
        ],
        out_specs=[pl.BlockSpec((1, ts, hw), lambda b, s: (b, s, 0)),
                   pl.BlockSpec((1, 1, 8, LANES), lambda b, s: (b, s, 0, 0))],
        out_shape=[jax.ShapeDtypeStruct((B, S, hw), BF16),
                   jax.ShapeDtypeStruct((B, S // ts, 8, LANES), F32)],
        scratch_shapes=[pltpu.VMEM((1, LANES), F32)],
        compiler_params=_cparams(("parallel", "arbitrary")),
        name="foxprep",
    )(b3, bf_pad, tri, head_sel, a3, a3)


def _fox_body(js_ref, hd_ref, qa_ref, qb_ref, ka_ref, kb_ref, va_ref, vb_ref, g_ref, o_ref,
              s_ref, p_ref, m_ref, acc_ref, vp_ref, *, t, rc):
    q_refs, k_refs, v_refs = (qa_ref, qb_ref), (ka_ref, kb_ref), (va_ref, vb_ref)
    b = pl.program_id(0)
    hp = pl.program_id(1)
    qi = pl.program_id(2)
    nt = pl.num_programs(2)
    lane = lax.broadcasted_iota(I32, (1, LANES), 1)
    lo = lane < FOX_HEAD_DIM
    feat, sum_l, keep, one_at, qs = [], [], [], [], []
    for hh in range(2):
        f, spare0 = _head_lanes(hh)
        feat.append(f)
        sum_l.append(lane == spare0)
        keep.append(f.astype(BF16))
        one_at.append((lane == spare0).astype(BF16))
        ones_q = ((lane >= spare0) & (lane < spare0 + BIAS_TERMS)).astype(BF16)
        qs.append(q_refs[hh][0] * keep[hh] + ones_q)

    @pl.when(qi == 0)
    def _():
        for hh in range(2):
            vp_ref[hh] = v_refs[hh][0] * keep[hh] + one_at[hh]

    m_ref[...] = jnp.full(m_ref.shape, -jnp.inf, F32)
    acc_ref[...] = jnp.zeros(acc_ref.shape, F32)
    first = [js_ref[(b * FOX_HEADS + hd_ref[2 * hp + hh]) * nt + qi] for hh in range(2)]
    j0 = jnp.minimum(first[0], first[1])
    nt_dims = (((1,), (1,)), ((), ()))
    nblk = t // LANES

    def tile(j, masked):
        start = pl.multiple_of(j * t, t)
        for hh in range(2):
            kt = k_refs[hh][0, pl.ds(start, t), :]
            s_ref[hh] = lax.dot_general(qs[hh], kt, nt_dims, preferred_element_type=F32)
        for hh in range(2):
            for r in range(t // rc):
                rows = slice(r * rc, (r + 1) * rc)
                diag_blk = (r * rc) // LANES
                mx = None
                for cb in range(diag_blk + 1 if masked else nblk):
                    cols = slice(cb * LANES, (cb + 1) * LANES)
                    sc = s_ref[hh, rows, cols]
                    if masked and cb == diag_blk:
                        ri = lax.broadcasted_iota(I32, (rc, LANES), 0) + r * rc
                        ci = lax.broadcasted_iota(I32, (rc, LANES), 1) + cb * LANES
                        sc = jnp.where(ci <= ri, sc, -jnp.inf)
                        s_ref[hh, rows, cols] = sc
                    mx = sc if mx is None else jnp.maximum(mx, sc)
                m_old = m_ref[hh, rows, :]
                m_new = jnp.maximum(m_old, jnp.broadcast_to(jnp.max(mx, axis=-1, keepdims=True), (rc, LANES)))
                m_ref[hh, rows, :] = m_new
                acc_ref[hh, rows, :] = jnp.exp2(m_old - m_new) * acc_ref[hh, rows, :]
            for r in range(t // rc):
                rows = slice(r * rc, (r + 1) * rc)
                diag_blk = (r * rc) // LANES
                m_new = m_ref[hh, rows, :]
                for cb in range(nblk):
                    cols = slice(cb * LANES, (cb + 1) * LANES)
                    if masked and cb > diag_blk:
                        p_ref[hh, rows, cols] = jnp.zeros((rc, LANES), BF16)
                    else:
                        p_ref[hh, rows, cols] = jnp.exp2(s_ref[hh, rows, cols] - m_new).astype(BF16)
            acc_ref[hh] += jnp.dot(p_ref[hh], vp_ref[hh, pl.ds(start, t), :], preferred_element_type=F32)

    def body(j, carry):
        tile(j, False)
        return carry

    lax.fori_loop(j0, qi, body, 0)
    tile(qi, True)
    outs = []
    for hh in range(2):
        acc = acc_ref[hh]
        w = jnp.where(feat[hh], 1.0 / FOX_HEAD_DIM, jnp.where(sum_l[hh], RMS_EPS, 0.0))
        outs.append(acc * lax.rsqrt(jnp.sum(acc * acc * w, axis=-1, keepdims=True)))
    o_ref[0] = (jnp.where(lo, outs[0], outs[1]) * g_ref[...]).astype(BF16)


def _fox_first_tiles(stats):
    qn, kn, cs, ce = (stats[:, :, i, :FOX_HEADS].transpose(0, 2, 1) for i in range(4))
    bound = cs[..., :, None] - ce[..., None, :] + qn[..., :, None] * (kn[..., None, :] + kn[..., :, None])
    nt = qn.shape[-1]
    ti = jnp.arange(nt)[:, None]
    tj = jnp.arange(nt)[None, :]
    need = (tj == ti) | ((tj < ti) & (bound >= -SKIP_LOG2))
    return jnp.argmax(need, axis=-1).astype(I32).reshape(-1)


def _fox_pairing(b_f):
    even = jnp.argsort(b_f[0::2]) * 2
    odd = jnp.argsort(b_f[1::2]) * 2 + 1
    return jnp.stack([even, odd], axis=1).reshape(-1).astype(I32)


def _fox(first, heads, a3, ka, gain, t, rc):
    B, S, _ = a3.shape
    qblk = lambda side: pl.BlockSpec(
        (1, t, HEAD_BLK), lambda b, hp, qi, js, hd: (b, qi, QBLK0 + hd[2 * hp + side] // 2))
    kblk = lambda side: pl.BlockSpec(
        (1, S, HEAD_BLK), lambda b, hp, qi, js, hd: (b, 0, hd[2 * hp + side]))
    vblk = lambda side: pl.BlockSpec(
        (1, S, HEAD_BLK), lambda b, hp, qi, js, hd: (b, 0, VBLK0 + hd[2 * hp + side] // 2))
    grid_spec = pltpu.PrefetchScalarGridSpec(
        num_scalar_prefetch=2,
        grid=(B, FOX_HEADS // 2, S // t),
        in_specs=[
            qblk(0), qblk(1), kblk(0), kblk(1), vblk(0), vblk(1),
            pl.BlockSpec((1, LANES), lambda b, hp, qi, js, hd: (0, hp)),
        ],
        out_specs=pl.BlockSpec((1, t, LANES), lambda b, hp, qi, js, hd: (b, qi, hp)),
        scratch_shapes=[pltpu.VMEM((2, t, t), F32), pltpu.VMEM((2, t, t), BF16),
                        pltpu.VMEM((2, t, LANES), F32), pltpu.VMEM((2, t, LANES), F32),
                        pltpu.VMEM((2, S, HEAD_BLK), BF16)],
    )
    return pl.pallas_call(
        functools.partial(_fox_body, t=t, rc=rc),
        grid_spec=grid_spec,
        out_shape=jax.ShapeDtypeStruct((B, S, FOX_WIDTH), BF16),
        compiler_params=_cparams(("parallel", "parallel", "arbitrary")),
        name="fox_attention",
    )(first, heads, a3, a3, ka, ka, a3, a3, gain)


def _hgrn_body(q_ref, f_ref, v_ref, gg_ref, lb_ref, gain_ref, t64_ref, hm_ref, o_ref, st_ref, o_acc, *, tc):
    @pl.when(pl.program_id(1) == 0)
    def _():
        st_ref[...] = jnp.zeros_like(st_ref)

    for half in range(HG_HEADS // 2):
        _hgrn_half(slice(half * LANES, (half + 1) * LANES), q_ref, f_ref, v_ref, gg_ref, lb_ref, gain_ref,
                   t64_ref, hm_ref, o_ref, st_ref.at[half], o_acc.at[half], tc)


def _hgrn_half(cols, q_ref, f_ref, v_ref, gg_ref, lb_ref, gain_ref, t64_ref, hm_ref, o_ref, st_ref, o_acc, tc):
    C, SB = HG_CHUNK, HG_SUB
    lane = lax.broadcasted_iota(I32, (1, LANES), 1)
    lo = lane < HG_DK
    lb = lb_ref[:, cols]
    fl = f_ref[0, :, cols]
    t2 = jnp.log1p(-lb) + _log_sigmoid(fl)
    t1 = jnp.where(lb > 0.0, jnp.log(jnp.maximum(lb, 1e-37)), NEG_BIG)
    logf = jnp.maximum(t1, t2) + jnp.log1p(jnp.exp(-jnp.abs(t1 - t2)))
    kk = (1.0 - lb) / (1.0 + jnp.exp(fl))
    q = q_ref[0, :, cols].astype(F32)
    nch = tc // C
    wide = jnp.concatenate([logf[ch * C:(ch + 1) * C] for ch in range(nch)], axis=1)
    a_wide = _tri_dot(t64_ref[...], wide)
    a = jnp.concatenate([a_wide[:, ch * LANES:(ch + 1) * LANES] for ch in range(nch)], axis=0)
    qa = (q * jnp.exp(a)).astype(BF16)

    rowblk = lax.broadcasted_iota(I32, (C, 1), 0) // SB
    r64 = lax.broadcasted_iota(I32, (C, C), 0)
    c64 = lax.broadcasted_iota(I32, (C, C), 1)
    causal = c64 <= r64
    rr = lax.broadcasted_iota(I32, (LANES, LANES), 0)
    cc = lax.broadcasted_iota(I32, (LANES, LANES), 1)
    same_head = (rr < HG_DK) == (cc < HG_DK)
    nt = (((1,), (1,)), ((), ()))
    tn = (((0,), (0,)), ((), ()))

    scores, updates, decays = [], [], []
    for ch in range(tc // C):
        sl = slice(ch * C, (ch + 1) * C)
        a_c, kk_c, v_c = a[sl], kk[sl], v_ref[0, sl, cols]
        a_last = a_c[C - 1:C, :]
        rows0, rows1 = [], []
        for blk in range(C // SB):
            b0 = ch * C + blk * SB
            e_blk = a_c[blk * SB - 1:blk * SB, :] if blk else jnp.zeros((1, LANES), F32)
            live = rowblk <= blk
            k_blk = jnp.where(live, kk_c * jnp.exp(jnp.where(live, e_blk - a_c, 0.0)), 0.0).astype(BF16)
            q_blk = q[b0:b0 + SB, :] * jnp.exp(a_c[blk * SB:(blk + 1) * SB, :] - e_blk)
            lhs = jnp.concatenate([jnp.where(lo, q_blk, 0.0), jnp.where(lo, 0.0, q_blk)], axis=0).astype(BF16)
            sc = lax.dot_general(lhs, k_blk, nt, preferred_element_type=F32)
            rows0.append(sc[0:SB])
            rows1.append(sc[SB:2 * SB])
        scores.append((jnp.where(causal, jnp.concatenate(rows0, axis=0), 0.0).astype(BF16),
                       jnp.where(causal, jnp.concatenate(rows1, axis=0), 0.0).astype(BF16)))
        kd = (kk_c * jnp.exp(a_last - a_c)).astype(BF16)
        updates.append(lax.dot_general(v_c, kd, tn, preferred_element_type=F32))
        decays.append(jnp.exp(a_last))
    intras = []
    for ch in range(tc // C):
        v_c = v_ref[0, ch * C:(ch + 1) * C, cols]
        intras.append(jnp.where(lo, jnp.dot(scores[ch][0], v_c, preferred_element_type=F32),
                                jnp.dot(scores[ch][1], v_c, preferred_element_type=F32)))
    st = st_ref[...]
    for ch in range(tc // C):
        sl = slice(ch * C, (ch + 1) * C)
        inter = lax.dot_general(qa[sl], st.astype(BF16), nt, preferred_element_type=F32)
        o_acc[sl, :] = inter + intras[ch]
        st = jnp.where(same_head, st * decays[ch] + updates[ch], 0.0)
    st_ref[...] = st

    o = o_acc[...]
    sq = o * o
    hi = sq.astype(BF16)
    lo_t = (sq - hi.astype(F32)).astype(BF16)
    ms = (jnp.dot(hi, hm_ref[...], preferred_element_type=F32)
          + jnp.dot(lo_t, hm_ref[...], preferred_element_type=F32))
    g = gg_ref[0, :, cols].astype(F32)
    silu = g / (1.0 + jnp.exp(-g))
    o_ref[0, :, cols] = (o * lax.rsqrt(ms + RMS_EPS) * gain_ref[:, cols] * silu).astype(BF16)


def _hgrn(a3, b3, lb, gain, tri, head_mean, tc):
    B, S, _ = a3.shape
    blk = lambda c0: pl.BlockSpec((1, tc, HG_WIDTH), lambda b, s: (b, s, c0 * LANES // HG_WIDTH))
    row = pl.BlockSpec((1, HG_WIDTH), lambda b, s: (0, 0))
    const = lambda n: pl.BlockSpec((n, n), lambda b, s: (0, 0))
    halves = HG_HEADS // 2
    return pl.pallas_call(
        functools.partial(_hgrn_body, tc=tc),
        grid=(B, S // tc),
        in_specs=[blk(GQ0), blk(0), blk(GI0), blk(GG0), row, row, const(HG_CHUNK), const(LANES)],
        out_specs=pl.BlockSpec((1, tc, HG_WIDTH), lambda b, s: (b, s, 0)),
        out_shape=jax.ShapeDtypeStruct((B, S, HG_WIDTH), BF16),
        scratch_shapes=[pltpu.VMEM((halves, LANES, LANES), F32), pltpu.VMEM((halves, tc, LANES), F32)],
        compiler_params=_cparams(("parallel", "arbitrary")),
        name="hgrn2",
    )(a3, b3, a3, a3, lb, gain, tri, head_mean)


def _pool_body(u_ref, w_ref, sc_ref, o_ref, ext_ref, *, tc):
    si = pl.program_id(1)

    @pl.when(si == 0)
    def _():
        ext_ref[0:POOL_HALO, :] = jnp.zeros((POOL_HALO, POOL_WIDTH), F32)

    u = u_ref[0].astype(F32)
    ext_ref[POOL_HALO:POOL_HALO + tc, :] = u
    acc = u
    sums = {}
    for j in range(1, max(POOL_WINDOWS)):
        acc = acc + ext_ref[POOL_HALO - j:POOL_HALO - j + tc, :]
        if j + 1 in POOL_WINDOWS:
            sums[j + 1] = acc
    lane = lax.broadcasted_iota(I32, (1, POOL_WIDTH), 1)
    grp = lane // POOL_GROUP_DIM
    win = sums[POOL_WINDOWS[-1]]
    wlen = jnp.full((1, POOL_WIDTH), float(POOL_WINDOWS[-1]), F32)
    for gi in range(len(POOL_WINDOWS) - 2, -1, -1):
        win = jnp.where(grp == gi, sums[POOL_WINDOWS[gi]], win)
        wlen = jnp.where(grp == gi, float(POOL_WINDOWS[gi]), wlen)
    pos = (si * tc + lax.broadcasted_iota(I32, (tc, 1), 0) + 1).astype(F32)
    pooled = win / jnp.minimum(pos, wlen) - u
    y = jnp.dot(pooled.astype(BF16), w_ref[...], preferred_element_type=F32) * sc_ref[...]
    o_ref[0] = y.astype(BF16)
    ext_ref[0:POOL_HALO, :] = u[tc - POOL_HALO:tc, :]


def _pool(a3, wbd, scale, tc):
    B, S, _ = a3.shape
    return pl.pallas_call(
        functools.partial(_pool_body, tc=tc),
        grid=(B, S // tc),
        in_specs=[
            pl.BlockSpec((1, tc, POOL_WIDTH), lambda b, s: (b, s, PU_BLK256)),
            pl.BlockSpec((POOL_WIDTH, POOL_WIDTH), lambda b, s: (0, 0)),
            pl.BlockSpec((1, POOL_WIDTH), lambda b, s: (0, 0)),
        ],
        out_specs=pl.BlockSpec((1, tc, POOL_WIDTH), lambda b, s: (b, s, 0)),
        out_shape=jax.ShapeDtypeStruct((B, S, POOL_WIDTH), BF16),
        scratch_shapes=[pltpu.VMEM((POOL_HALO + tc, POOL_WIDTH), F32)],
        compiler_params=_cparams(("parallel", "arbitrary")),
        name="pool_mixer",
    )(a3, wbd, scale)


def _router_body(of_ref, oh_ref, op_ref, h_ref, wo_ref, g_ref, wr_ref, br_ref, tri_ref,
                 hn_ref, ri_ref, rg_ref, cnt_ref, carry_ref, *, tm):
    @pl.when(pl.program_id(0) == 0)
    def _():
        carry_ref[...] = jnp.zeros_like(carry_ref)

    f0, f1 = FOX_WIDTH, FOX_WIDTH + HG_WIDTH
    mix = (jnp.dot(of_ref[...], wo_ref[0:f0, :], preferred_element_type=F32)
           + jnp.dot(oh_ref[...], wo_ref[f0:f1, :], preferred_element_type=F32)
           + jnp.dot(op_ref[...], wo_ref[f1:f1 + POOL_WIDTH, :], preferred_element_type=F32))
    hn = h_ref[...] + mix
    hn_ref[...] = hn
    ms = jnp.mean(hn * hn, axis=-1, keepdims=True)
    xn = hn * lax.rsqrt(ms + RMS_EPS) * g_ref[...]
    x1 = xn.astype(BF16)
    x2 = (xn - x1.astype(F32)).astype(BF16)
    r1 = jnp.dot(x1, wr_ref[...], preferred_element_type=F32)
    r2 = jnp.dot(x2, wr_ref[...], preferred_element_type=F32)
    logits = r1[:, :LANES] + r1[:, LANES:] + r2[:, :LANES] + br_ref[...]

    lanef = lax.broadcasted_iota(I32, (tm, LANES), 1).astype(F32)
    l = logits
    vals, idxs, sels = [], [], []
    for _ in range(TOP_K):
        mk = jnp.max(l, axis=-1, keepdims=True)
        ik = jnp.min(jnp.where(l == mk, lanef, float(LANES)), axis=-1, keepdims=True)
        sk = lanef == ik
        vals.append(mk)
        idxs.append(ik)
        sels.append(sk)
        l = jnp.where(sk, -jnp.inf, l)
    es = [jnp.exp(v - vals[0]) for v in vals]
    den = es[0] + es[1] + es[2] + es[3]
    sel = jnp.zeros((tm, LANES), F32)
    for sk in sels:
        sel = jnp.where(sk, 1.0, sel)
    rank = jnp.dot(tri_ref[...], sel.astype(BF16), preferred_element_type=F32) + carry_ref[...]
    carry = carry_ref[...] + jnp.sum(sel, axis=0, keepdims=True)
    carry_ref[...] = carry
    cnt_ref[...] = carry
    ri = jnp.zeros((tm, LANES), F32)
    rg = jnp.zeros((tm, LANES), F32)
    for k in range(TOP_K):
        pos_k = jnp.sum(jnp.where(sels[k], rank, 0.0), axis=-1, keepdims=True)
        ri = jnp.where(lanef == float(k), idxs[k], ri)
        ri = jnp.where(lanef == float(TOP_K + k), pos_k, ri)
        rg = jnp.where(lanef == float(k), es[k] / den, rg)
    ri_ref[...] = ri.astype(I32)
    rg_ref[...] = rg


def _router(o_fox, o_hg, o_pool, h, wo, g, wr, br, tri, tm):
    T, D = h.shape
    rowblk = lambda w: pl.BlockSpec((tm, w), lambda i: (i, 0))
    const = lambda shp: pl.BlockSpec(shp, lambda i: tuple(0 for _ in shp))
    return pl.pallas_call(
        functools.partial(_router_body, tm=tm),
        grid=(T // tm,),
        in_specs=[rowblk(FOX_WIDTH), rowblk(HG_WIDTH), rowblk(POOL_WIDTH), rowblk(D),
                  const((D, D)), const((1, D)), const((D, 2 * LANES)), const((1, LANES)), const((tm, tm))],
        out_specs=[rowblk(D), rowblk(LANES), rowblk(LANES), const((1, LANES))],
        out_shape=[jax.ShapeDtypeStruct((T, D), F32), jax.ShapeDtypeStruct((T, LANES), I32),
                   jax.ShapeDtypeStruct((T, LANES), F32), jax.ShapeDtypeStruct((1, LANES), F32)],
        scratch_shapes=[pltpu.VMEM((1, LANES), F32)],
        compiler_params=_cparams(("arbitrary",)),
        name="outproj_router",
    )(o_fox, o_hg, o_pool, h, wo, g, wr, br, tri)


def _to_row_tiles(ref, x):
    ref[...] = pltpu.einshape("r(cl)->rcl", x, c=ref.shape[-2])


def _from_row_tiles(ref):
    return pltpu.einshape("rcl->r(cl)", ref[...])


def _row_copy(src_ref, src_row, dst_ref, dst_row, sem):
    return pltpu.make_async_copy(src_ref.at[pl.ds(src_row, 1)], dst_ref.at[pl.ds(dst_row, 1)], sem)


def _wait_rows(tile_ref, sem, times):
    for _ in range(times):
        pltpu.make_async_copy(tile_ref, tile_ref, sem).wait()


def _dispatch_body(pad0_ref, padn_ref, dst_ref, h_ref, g_ref, buf_ref, xn_ref, zero_ref, sem, zsem, *, tt, bm):
    i = pl.program_id(0)
    n = pl.num_programs(0)
    slot = i % 2
    stage = xn_ref.at[slot]

    @pl.when(i >= 2)
    def _():
        _wait_rows(stage, sem.at[slot], TOP_K)

    x = h_ref[...]
    ms = jnp.mean(x * x, axis=-1, keepdims=True)
    _to_row_tiles(stage, x * lax.rsqrt(ms + RMS_EPS) * g_ref[...])

    def issue(r, _):
        for k in range(TOP_K):
            _row_copy(stage, r, buf_ref, dst_ref[0, 0, r * TOP_K + k], sem.at[slot]).start(priority=k % 2)
        return 0

    lax.fori_loop(0, tt, issue, 0, unroll=DMA_UNROLL)

    @pl.when(i == 0)
    def _():
        zero_ref[...] = jnp.zeros_like(zero_ref)
        half = bm // 2
        n_exp = pad0_ref.shape[0] - 1

        def run(cp, pred, wait):
            @pl.when(pred)
            def _():
                cp.wait() if wait else cp.start()

        def pad_copies(e, wait):
            off, cnt = pad0_ref[e], padn_ref[e]
            for r in range(SUBLANES - 1):
                run(_row_copy(zero_ref, 0, buf_ref, off + r, zsem), r < (cnt & (SUBLANES - 1)), wait)
            off = off + (cnt & (SUBLANES - 1))
            sz = SUBLANES
            while sz <= half:
                take = (cnt & sz) != 0
                dst = buf_ref.at[pl.ds(pl.multiple_of(off, SUBLANES), sz)]
                run(pltpu.make_async_copy(zero_ref.at[pl.ds(0, sz)], dst, zsem), take, wait)
                off = off + jnp.where(take, sz, 0)
                sz *= 2
            return 0

        def tail_copy(c, wait):
            dst = buf_ref.at[pl.ds(pl.multiple_of(pad0_ref[n_exp] + c * half, SUBLANES), half)]
            cp = pltpu.make_async_copy(zero_ref, dst, zsem)
            cp.wait() if wait else cp.start()
            return 0

        n_tail = padn_ref[n_exp] // half
        for wait in (False, True):
            lax.fori_loop(0, n_exp, lambda e, c, w=wait: pad_copies(e, w), 0)
            lax.fori_loop(0, n_tail, lambda c, _, w=wait: tail_copy(c, w), 0)

    @pl.when(i == n - 1)
    def _():
        _wait_rows(stage, sem.at[slot], TOP_K)

    @pl.when((i == n - 1) & (n >= 2))
    def _():
        _wait_rows(xn_ref.at[1 - slot], sem.at[1 - slot], TOP_K)


def _dispatch(pad0, padn, dst3, hn, g, R, tt, bm):
    T, D = hn.shape
    grid_spec = pltpu.PrefetchScalarGridSpec(
        num_scalar_prefetch=2,
        grid=(T // tt,),
        in_specs=[
            pl.BlockSpec((1, 1, tt * TOP_K), lambda i, p0, pn: (i, 0, 0), memory_space=pltpu.SMEM),
            pl.BlockSpec((tt, D), lambda i, p0, pn: (i, 0)),
            pl.BlockSpec((1, D), lambda i, p0, pn: (0, 0)),
        ],
        out_specs=pl.BlockSpec(memory_space=pl.ANY),
        scratch_shapes=[pltpu.VMEM((2, tt, D // LANES, LANES), F32), pltpu.VMEM((bm // 2, D // LANES, LANES), F32),
                        pltpu.SemaphoreType.DMA((2,)), pltpu.SemaphoreType.DMA],
    )
    return pl.pallas_call(
        functools.partial(_dispatch_body, tt=tt, bm=bm),
        grid_spec=grid_spec,
        out_shape=jax.ShapeDtypeStruct((R, D // LANES, LANES), F32),
        compiler_params=_cparams(("arbitrary",)),
        name="moe_dispatch",
    )(pad0, padn, dst3, hn, g)


MXU_DIM = 256


def _expert_body(te_ref, nu_ref, x_ref, w1_ref, w2_ref, perm_ref, b1g_ref, b1l_ref, b2_ref, y_ref,
                 w1g_scr, w1l_scr, w2_scr):
    i = pl.program_id(0)
    live = i < nu_ref[0]
    prev = te_ref[jnp.maximum(i - 1, 0)]

    @pl.when(live & ((i == 0) | (te_ref[i] != prev)))
    def _():
        half = MXU_DIM // 2
        for grp in range(w1_ref.shape[-1] // MXU_DIM):
            w = w1_ref[0, :, grp * MXU_DIM:(grp + 1) * MXU_DIM].astype(BF16)
            r = jnp.dot(w, perm_ref[...], preferred_element_type=F32).astype(BF16)
            w1g_scr[:, grp * half:(grp + 1) * half] = r[:, :half]
            w1l_scr[:, grp * half:(grp + 1) * half] = r[:, half:]
        w2_scr[...] = w2_ref[0].astype(BF16)

    @pl.when(live)
    def _():
        x = _from_row_tiles(x_ref).astype(BF16)
        glu = jnp.dot(x, w1g_scr[...], preferred_element_type=F32) + b1g_ref[0]
        lin = jnp.dot(x, w1l_scr[...], preferred_element_type=F32) + b1l_ref[0]
        glu = jnp.minimum(glu, SWIGLU_LIMIT)
        lin = jnp.clip(lin, -SWIGLU_LIMIT, SWIGLU_LIMIT)
        act = glu / (1.0 + jnp.exp(-SWIGLU_ALPHA * glu)) * (lin + 1.0)
        _to_row_tiles(y_ref, jnp.dot(act.astype(BF16), w2_scr[...], preferred_element_type=F32) + b2_ref[0])

    @pl.when(jnp.logical_not(live))
    def _():
        y_ref[...] = jnp.zeros_like(y_ref)


def _experts(tile_e, n_used, buf, w1, w2, b1g, b1l, b2, bm, first_expert):
    R, dsub, _ = buf.shape
    D = dsub * LANES
    F2 = w1.shape[-1]
    F = F2 // 2
    half = MXU_DIM // 2
    pi = jnp.arange(MXU_DIM)[:, None]
    pj = jnp.arange(MXU_DIM)[None, :]
    perm = jnp.where(pj < half, pi == 2 * pj, pi == 2 * (pj - half) + 1).astype(BF16)
    rows = pl.BlockSpec((bm, dsub, LANES), lambda i, te, nu: (jnp.minimum(i, nu[0] - 1), 0, 0))
    wspec = lambda a, b, e0=0: pl.BlockSpec((1, a, b), lambda i, te, nu: (e0 + te[i], 0, 0))
    grid_spec = pltpu.PrefetchScalarGridSpec(
        num_scalar_prefetch=2,
        grid=(R // bm,),
        in_specs=[rows, wspec(D, F2, first_expert), wspec(F, D, first_expert),
                  pl.BlockSpec((MXU_DIM, MXU_DIM), lambda i, te, nu: (0, 0)),
                  wspec(1, F), wspec(1, F), wspec(1, D)],
        out_specs=pl.BlockSpec((bm, dsub, LANES), lambda i, te, nu: (i, 0, 0)),
        scratch_shapes=[pltpu.VMEM((D, F), BF16), pltpu.VMEM((D, F), BF16), pltpu.VMEM((F, D), BF16)],
    )
    return pl.pallas_call(
        _expert_body,
        grid_spec=grid_spec,
        out_shape=jax.ShapeDtypeStruct((R, dsub, LANES), F32),
        compiler_params=_cparams(("arbitrary",), vmem=EXPERT_VMEM_LIMIT),
        name="moe_experts",
    )(tile_e, n_used, buf, w1, w2, perm, b1g, b1l, b2)


def _combine_body(dst_ref, h_ref, rg_ref, gf_ref, y_ref, o_ref, rows_ref, sem, *, tt, final_norm):
    s = pl.program_id(0)
    n_tiles = pl.num_programs(0) - 1

    @pl.when(s < n_tiles)
    def _():
        slot = s % 2

        def issue(r, _):
            for k in range(TOP_K):
                _row_copy(y_ref, dst_ref[0, 0, r * TOP_K + k], rows_ref.at[slot, k], r,
                          sem.at[slot]).start(priority=k % 2)
            return 0

        lax.fori_loop(0, tt, issue, 0, unroll=DMA_UNROLL)

    @pl.when(s >= 1)
    def _():
        slot = (s - 1) % 2
        _wait_rows(rows_ref.at[slot, 0], sem.at[slot], TOP_K)
        out = h_ref[...]
        gates = rg_ref[...]
        for k in range(TOP_K):
            out = out + gates[:, k:k + 1] * _from_row_tiles(rows_ref.at[slot, k])
        if final_norm:
            ms = jnp.mean(out * out, axis=-1, keepdims=True)
            out = out * lax.rsqrt(ms + RMS_EPS) * gf_ref[...]
        o_ref[...] = out


def _combine(dst3, hn, rg, g_final, ybuf, tt, final_norm):
    T, D = hn.shape
    n_tiles = T // tt
    prev = lambda s: (jnp.maximum(s - 1, 0), 0)
    return pl.pallas_call(
        functools.partial(_combine_body, tt=tt, final_norm=final_norm),
        grid=(n_tiles + 1,),
        in_specs=[
            pl.BlockSpec((1, 1, tt * TOP_K), lambda s: (jnp.minimum(s, n_tiles - 1), 0, 0), memory_space=pltpu.SMEM),
            pl.BlockSpec((tt, D), prev),
            pl.BlockSpec((tt, LANES), prev),
            pl.BlockSpec((1, D), lambda s: (0, 0)),
            pl.BlockSpec(memory_space=pl.ANY),
        ],
        out_specs=pl.BlockSpec((tt, D), prev),
        out_shape=jax.ShapeDtypeStruct((T, D), F32),
        scratch_shapes=[pltpu.VMEM((2, TOP_K, tt, D // LANES, LANES), F32), pltpu.SemaphoreType.DMA((2,))],
        compiler_params=_cparams(("arbitrary",)),
        name="moe_combine",
    )(dst3, hn, rg, g_final, ybuf)


def _tiles(B, S):
    T = B * S
    ts = min(512, S)
    tm = min(512, T)
    tt = min(256, T)
    bm = min(512, T)
    return ts, tm, tt, bm


def _block_tri(n, blk, strict=False):
    r = jnp.arange(n)[:, None]
    c = jnp.arange(n)[None, :]
    keep = (c < r) if strict else (c <= r)
    return (keep & (r // blk == c // blk)).astype(BF16)


def kernel(x, norm_mix, w_in, b_fox_f, fox_gain, hgrn_lb, hgrn_gain, pool_w, pool_scale, w_out,
           norm_moe, w_router, b_router, w1, b1, w2, b2, norm_final):
    B, S, D = x.shape
    L = w_in.shape[0]
    T = B * S
    E = w1.shape[1]
    ts, tm, tt, bm = _tiles(B, S)

    offs = [0]
    for wdt in (FOX_WIDTH, FOX_WIDTH, FOX_WIDTH, FOX_HEADS, HG_WIDTH, HG_WIDTH, HG_WIDTH, HG_WIDTH, POOL_WIDTH):
        offs.append(offs[-1] + wdt)
    col = lambda i: w_in[:, :, offs[i]:offs[i + 1]]
    scale = FOX_HEAD_DIM ** -0.5 * LOG2E
    wa = jnp.concatenate([col(0) * scale, col(1), col(2), col(4), col(6), col(7), col(8)], axis=-1).astype(BF16)
    wb = jnp.concatenate([col(5), col(3), jnp.zeros((L, D, LANES - FOX_HEADS), F32)], axis=-1).astype(BF16)
    bf_pad = jnp.pad(b_fox_f, ((0, 0), (0, LANES - FOX_HEADS)))
    p = jax.nn.softmax(hgrn_lb.astype(F32), axis=0)
    cs = jnp.cumsum(p, axis=0)
    lbs = cs - cs[0:1]
    eye = jnp.eye(len(POOL_WINDOWS), dtype=F32)
    pool_bd = jnp.einsum("lgcd,gh->lgchd", pool_w, eye).reshape(L, POOL_WIDTH, POOL_WIDTH).astype(BF16)
    wr_pad = jnp.pad(w_router, ((0, 0), (0, 0), (0, LANES - E)))
    wr_hi = wr_pad.astype(BF16)
    wr_lo = (wr_pad - wr_hi.astype(F32)).astype(BF16)
    wr = jnp.concatenate([wr_hi, wr_lo], axis=-1)
    br_pad = jnp.pad(b_router, ((0, 0), (0, LANES - E)), constant_values=NEG_BIG)
    w1_all = w1.reshape((L * E,) + w1.shape[2:])
    w2_all = w2.reshape((L * E,) + w2.shape[2:])
    b1g = b1[..., 0::2][:, :, None, :]
    b1l = b1[..., 1::2][:, :, None, :]
    b2r = b2[:, :, None, :]

    tri_seq = _block_tri(ts, ts)
    tri64 = _block_tri(HG_CHUNK, HG_CHUNK)
    li = jnp.arange(LANES)
    head_mean = ((li[:, None] // HG_DK == li[None, :] // HG_DK) / HG_DK).astype(BF16)
    tri_tok = _block_tri(tm, tm, strict=True)

    M = T * TOP_K
    R = -(-M // bm) * bm + E * bm
    NB = R // bm

    h = x.reshape(T, D)
    for l in range(L):
        a2, b2d = _inproj(h, norm_mix[l][None], wa[l], wb[l], tm)
        a3 = a2.reshape(B, S, A_COLS)
        b3 = b2d.reshape(B, S, B_COLS)
        ka, stats = _foxprep(a3, b3, bf_pad[l][None], tri_seq, ts)
        heads = _fox_pairing(b_fox_f[l])
        by_head = lambda w: w.reshape((FOX_HEADS, FOX_HEAD_DIM) + w.shape[1:])[heads].reshape(w.shape)
        o_fox = _fox(_fox_first_tiles(stats), heads, a3, ka, by_head(fox_gain[l])[None], ts, min(32, ts))
        wo_l = jnp.concatenate([by_head(w_out[l, :FOX_WIDTH]), w_out[l, FOX_WIDTH:]], axis=0).astype(BF16)
        o_hg = _hgrn(a3, b3, lbs[l][None], hgrn_gain[l][None], tri64, head_mean, ts)
        o_pool = _pool(a3, pool_bd[l], pool_scale[l][None], ts)
        hn, ri, rg, cnt = _router(o_fox.reshape(T, FOX_WIDTH), o_hg.reshape(T, HG_WIDTH),
                                  o_pool.reshape(T, POOL_WIDTH), h, wo_l, norm_moe[l][None],
                                  wr[l], br_pad[l][None], tri_tok, tm)
        counts = cnt[0, :E].astype(I32)
        padded = (counts + bm - 1) // bm * bm
        pend = jnp.cumsum(padded)
        pstart = pend - padded
        first_row = jnp.sum(jnp.where(ri[:, 0:TOP_K, None] == jnp.arange(E, dtype=I32), pstart.astype(I32), 0), axis=-1)
        dst = first_row + ri[:, TOP_K:2 * TOP_K]
        n_used = (pend[-1] // bm).astype(I32)
        tile_ids = jnp.minimum(jnp.arange(NB, dtype=I32), n_used - 1)
        tile_e = jnp.minimum(jnp.sum(pend[None, :] <= (tile_ids * bm)[:, None], axis=1), E - 1).astype(I32)
        dst3 = dst.reshape(T // tt, 1, tt * TOP_K)
        pad0 = jnp.concatenate([pstart + counts, pend[-1:]]).astype(I32)
        padn = jnp.concatenate([padded - counts, R - pend[-1:]]).astype(I32)
        buf = _dispatch(pad0, padn, dst3, hn, norm_moe[l][None], R, tt, bm)
        ybuf = _experts(tile_e, n_used.reshape(1), buf, w1_all, w2_all, b1g[l], b1l[l], b2r[l], bm, l * E)
        h = _combine(dst3, hn, rg, norm_final[None], ybuf, tt, final_norm=(l == L - 1))
    return h.reshape(B, S, D)
```

```python
import functools

import jax
import jax.numpy as jnp
from jax import lax
from jax.experimental import pallas as pl
from jax.experimental.pallas import tpu as pltpu

F32 = jnp.float32
BF16 = jnp.bfloat16
I32 = jnp.int32

LANES = 128
SUBLANES = 8
VMEM_LIMIT = 56 * 1024 * 1024
EXPERT_VMEM_LIMIT = 60 * 1024 * 1024

FOX_HEADS = 8
FOX_HEAD_DIM = 64
FOX_WIDTH = FOX_HEADS * FOX_HEAD_DIM
HG_HEADS = 4
HG_DK = 64
HG_WIDTH = HG_HEADS * HG_DK
HG_CHUNK = 64
HG_SUB = 16
POOL_WINDOWS = (2, 4, 8, 16)
POOL_GROUP_DIM = 64
POOL_WIDTH = len(POOL_WINDOWS) * POOL_GROUP_DIM
POOL_HALO = 16
N_EXPERTS = 32
TOP_K = 4
SWIGLU_LIMIT = 7.0
SWIGLU_ALPHA = 1.702
RMS_EPS = 1e-5
NEG_BIG = -1e30
DMA_UNROLL = 8

HEAD_BLK = LANES
QBLK0, KBLK0, VBLK0 = 0, FOX_HEADS // 2, FOX_HEADS
GQ0 = 3 * FOX_HEADS // 2
GI0, GG0 = GQ0 + 2, GQ0 + 4
PU_BLK256 = (GQ0 + 6) // 2
A_COLS = (GQ0 + 8) * LANES
BIAS_TERMS = 3


def _head_lanes(parity):
    lane = lax.broadcasted_iota(I32, (1, LANES), 1)
    lower = lane < FOX_HEAD_DIM
    return (lower, FOX_HEAD_DIM) if parity == 0 else (jnp.logical_not(lower), 0)


LOG2E = 1.4426950408889634
SKIP_LOG2 = 150.0
NORM_SLACK = 1.01
B_COLS = 384


def _cparams(sem, vmem=VMEM_LIMIT):
    return pltpu.CompilerParams(dimension_semantics=sem, vmem_limit_bytes=vmem)


def _split3(x):
    x1 = x.astype(BF16)
    r1 = x - x1.astype(F32)
    x2 = r1.astype(BF16)
    x3 = (r1 - x2.astype(F32)).astype(BF16)
    return x1, x2, x3


def _tri_dot(tri, x):
    x1, x2, x3 = _split3(x)
    d = lambda a: jnp.dot(tri, a, preferred_element_type=F32)
    return d(x1) + d(x2) + d(x3)


def _log_sigmoid(z):
    return jnp.minimum(z, 0.0) - jnp.log1p(jnp.exp(-jnp.abs(z)))


def _inproj_body(x_ref, g_ref, wa_ref, wb_ref, a_ref, b_ref):
    x = x_ref[...]
    ms = jnp.mean(x * x, axis=-1, keepdims=True)
    xn = (x * lax.rsqrt(ms + RMS_EPS) * g_ref[...]).astype(BF16)
    a_ref[...] = jnp.dot(xn, wa_ref[...], preferred_element_type=F32).astype(BF16)
    b_ref[...] = jnp.dot(xn, wb_ref[...], preferred_element_type=F32)


def _inproj(h, g, wa, wb, tm):
    T, D = h.shape
    return pl.pallas_call(
        _inproj_body,
        grid=(T // tm,),
        in_specs=[
            pl.BlockSpec((tm, D), lambda i: (i, 0)),
            pl.BlockSpec((1, D), lambda i: (0, 0)),
            pl.BlockSpec((D, A_COLS), lambda i: (0, 0)),
            pl.BlockSpec((D, B_COLS), lambda i: (0, 0)),
        ],
        out_specs=[
            pl.BlockSpec((tm, A_COLS), lambda i: (i, 0)),
            pl.BlockSpec((tm, B_COLS), lambda i: (i, 0)),
        ],
        out_shape=[jax.ShapeDtypeStruct((T, A_COLS), BF16), jax.ShapeDtypeStruct((T, B_COLS), F32)],
        compiler_params=_cparams(("parallel",)),
        name="inproj",
    )(h, g, wa, wb)


def _foxprep_body(ff_ref, bf_ref, tri_ref, hsel_ref, q_ref, k_ref, ka_ref, st_ref, carry_ref, *, ts):
    logf = _log_sigmoid(ff_ref[0] + bf_ref[...])
    c = _tri_dot(tri_ref[...], logf) + carry_ref[...]
    carry_ref[...] = c[ts - 1:ts, :]
    cl = c * LOG2E
    lane = lax.broadcasted_iota(I32, (1, LANES), 1)
    for h in range(FOX_HEADS):
        feat, spare0 = _head_lanes(h % 2)
        terms = _split3(jnp.broadcast_to(-cl[:, h:h + 1], (ts, LANES)))
        pair = k_ref[0, :, (h // 2) * HEAD_BLK:(h // 2 + 1) * HEAD_BLK]
        ka = pair * feat.astype(BF16)
        for i, term in enumerate(terms):
            ka = jnp.where(lane == spare0 + i, term, ka)
        ka_ref[0, :, h * HEAD_BLK:(h + 1) * HEAD_BLK] = ka

    def max_norm(ref):
        x = ref[0].astype(F32)
        n2 = jnp.dot((x * x).astype(BF16), hsel_ref[...], preferred_element_type=F32)
        return jnp.sqrt(jnp.max(n2, axis=0, keepdims=True)) * NORM_SLACK

    qn_row = max_norm(q_ref)
    kn_row = max_norm(k_ref)
    rowi = lax.broadcasted_iota(I32, (8, LANES), 0)
    st = jnp.where(rowi == 0, qn_row, jnp.where(rowi == 1, kn_row, jnp.where(rowi == 2, cl[0:1, :],
                   jnp.where(rowi == 3, cl[ts - 1:ts, :], 0.0))))
    st_ref[0, 0] = st


def _fox_body(js_ref, hd_ref, qa_ref, qb_ref, ka_ref, kb_ref, va_ref, vb_ref, g_ref, o_ref,
              s_ref, p_ref, m_ref, acc_ref, vp_ref, *, t, rc):
    q_refs, k_refs, v_refs = (qa_ref, qb_ref), (ka_ref, kb_ref), (va_ref, vb_ref)
    b = pl.program_id(0)
    hp = pl.program_id(1)
    qi = pl.program_id(2)
    nt = pl.num_programs(2)
    lane = lax.broadcasted_iota(I32, (1, LANES), 1)
    lo = lane < FOX_HEAD_DIM
    feat, sum_l, keep, one_at, qs = [], [], [], [], []
    for hh in range(2):
        f, spare0 = _head_lanes(hh)
        feat.append(f)
        sum_l.append(lane == spare0)
        keep.append(f.astype(BF16))
        one_at.append((lane == spare0).astype(BF16))
        ones_q = ((lane >= spare0) & (lane < spare0 + BIAS_TERMS)).astype(BF16)
        qs.append(q_refs[hh][0] * keep[hh] + ones_q)

    @pl.when(qi == 0)
    def _():
        for hh in range(2):
            vp_ref[hh] = v_refs[hh][0] * keep[hh] + one_at[hh]

    m_ref[...] = jnp.full(m_ref.shape, -jnp.inf, F32)
    acc_ref[...] = jnp.zeros(acc_ref.shape, F32)
    first = [js_ref[(b * FOX_HEADS + hd_ref[2 * hp + hh]) * nt + qi] for hh in range(2)]
    j0 = jnp.minimum(first[0], first[1])
    nt_dims = (((1,), (1,)), ((), ()))
    nblk = t // LANES

    def tile(j, masked):
        start = pl.multiple_of(j * t, t)
        for hh in range(2):
            kt = k_refs[hh][0, pl.ds(start, t), :]
            s_ref[hh] = lax.dot_general(qs[hh], kt, nt_dims, preferred_element_type=F32)
        for hh in range(2):
            for r in range(t // rc):
                rows = slice(r * rc, (r + 1) * rc)
                diag_blk = (r * rc) // LANES
                mx = None
                for cb in range(diag_blk + 1 if masked else nblk):
                    cols = slice(cb * LANES, (cb + 1) * LANES)
                    sc = s_ref[hh, rows, cols]
                    if masked and cb == diag_blk:
                        ri = lax.broadcasted_iota(I32, (rc, LANES), 0) + r * rc
                        ci = lax.broadcasted_iota(I32, (rc, LANES), 1) + cb * LANES
                        sc = jnp.where(ci <= ri, sc, -jnp.inf)
                        s_ref[hh, rows, cols] = sc
                    mx = sc if mx is None else jnp.maximum(mx, sc)
                m_old = m_ref[hh, rows, :]
                m_new = jnp.maximum(m_old, jnp.broadcast_to(jnp.max(mx, axis=-1, keepdims=True), (rc, LANES)))
                m_ref[hh, rows, :] = m_new
                acc_ref[hh, rows, :] = jnp.exp2(m_old - m_new) * acc_ref[hh, rows, :]
            for r in range(t // rc):
                rows = slice(r * rc, (r + 1) * rc)
                diag_blk = (r * rc) // LANES
                m_new = m_ref[hh, rows, :]
                for cb in range(nblk):
                    cols = slice(cb * LANES, (cb + 1) * LANES)
                    if masked and cb > diag_blk:
                        p_ref[hh, rows, cols] = jnp.zeros((rc, LANES), BF16)
                    else:
                        p_ref[hh, rows, cols] = jnp.exp2(s_ref[hh, rows, cols] - m_new).astype(BF16)
            acc_ref[hh] += jnp.dot(p_ref[hh], vp_ref[hh, pl.ds(start, t), :], preferred_element_type=F32)

    def body(j, carry):
        tile(j, False)
        return carry

    lax.fori_loop(j0, qi, body, 0)
    tile(qi, True)
    outs = []
    for hh in range(2):
        acc = acc_ref[hh]
        w = jnp.where(feat[hh], 1.0 / FOX_HEAD_DIM, jnp.where(sum_l[hh], RMS_EPS, 0.0))
        outs.append(acc * lax.rsqrt(jnp.sum(acc * acc * w, axis=-1, keepdims=True)))
    o_ref[0] = (jnp.where(lo, outs[0], outs[1]) * g_ref[...]).astype(BF16)


def _fox_first_tiles(stats):
    qn, kn, cs, ce = (stats[:, :, i, :FOX_HEADS].transpose(0, 2, 1) for i in range(4))
    bound = cs[..., :, None] - ce[..., None, :] + qn[..., :, None] * (kn[..., None, :] + kn[..., :, None])
    nt = qn.shape[-1]
    ti = jnp.arange(nt)[:, None]
    tj = jnp.arange(nt)[None, :]
    need = (tj == ti) | ((tj < ti) & (bound >= -SKIP_LOG2))
    return jnp.argmax(need, axis=-1).astype(I32).reshape(-1)


def _fox_pairing(b_f):
    even = jnp.argsort(b_f[0::2]) * 2
    odd = jnp.argsort(b_f[1::2]) * 2 + 1
    return jnp.stack([even, odd], axis=1).reshape(-1).astype(I32)


def _fox(first, heads, a3, ka, gain, t, rc):
    B, S, _ = a3.shape
    qblk = lambda side: pl.BlockSpec(
        (1, t, HEAD_BLK), lambda b, hp, qi, js, hd: (b, qi, QBLK0 + hd[2 * hp + side] // 2))
    kblk = lambda side: pl.BlockSpec(
        (1, S, HEAD_BLK), lambda b, hp, qi, js, hd: (b, 0, hd[2 * hp + side]))
    vblk = lambda side: pl.BlockSpec(
        (1, S, HEAD_BLK), lambda b, hp, qi, js, hd: (b, 0, VBLK0 + hd[2 * hp + side] // 2))
    grid_spec = pltpu.PrefetchScalarGridSpec(
        num_scalar_prefetch=2,
        grid=(B, FOX_HEADS // 2, S // t),
        in_specs=[
            qblk(0), qblk(1), kblk(0), kblk(1), vblk(0), vblk(1),
            pl.BlockSpec((1, LANES), lambda b, hp, qi, js, hd: (0, hp)),
        ],
        out_specs=pl.BlockSpec((1, t, LANES), lambda b, hp, qi, js, hd: (b, qi, hp)),
        scratch_shapes=[pltpu.VMEM((2, t, t), F32), pltpu.VMEM((2, t, t), BF16),
                        pltpu.VMEM((2, t, LANES), F32), pltpu.VMEM((2, t, LANES), F32),
                        pltpu.VMEM((2, S, HEAD_BLK), BF16)],
    )
    return pl.pallas_call(
        functools.partial(_fox_body, t=t, rc=rc),
        grid_spec=grid_spec,
        out_shape=jax.ShapeDtypeStruct((B, S, FOX_WIDTH), BF16),
        compiler_params=_cparams(("parallel", "parallel", "arbitrary")),
        name="fox_attention",
    )(first, heads, a3, a3, ka, ka, a3, a3, gain)


def _hgrn_body(q_ref, f_ref, v_ref, gg_ref, lb_ref, gain_ref, t64_ref, hm_ref, o_ref, st_ref, o_acc, *, tc):
    for half in range(HG_HEADS // 2):
        _hgrn_half(slice(half * LANES, (half + 1) * LANES), q_ref, f_ref, v_ref, gg_ref, lb_ref, gain_ref,
                   t64_ref, hm_ref, o_ref, st_ref.at[half], o_acc.at[half], tc)


def _hgrn_half(cols, q_ref, f_ref, v_ref, gg_ref, lb_ref, gain_ref, t64_ref, hm_ref, o_ref, st_ref, o_acc, tc):
    C, SB = HG_CHUNK, HG_SUB
    lane = lax.broadcasted_iota(I32, (1, LANES), 1)
    lo = lane < HG_DK
    lb = lb_ref[:, cols]
    fl = f_ref[0, :, cols]
    t2 = jnp.log1p(-lb) + _log_sigmoid(fl)
    t1 = jnp.where(lb > 0.0, jnp.log(jnp.maximum(lb, 1e-37)), NEG_BIG)
    logf = jnp.maximum(t1, t2) + jnp.log1p(jnp.exp(-jnp.abs(t1 - t2)))
    kk = (1.0 - lb) / (1.0 + jnp.exp(fl))
    q = q_ref[0, :, cols].astype(F32)
    nch = tc // C
    wide = jnp.concatenate([logf[ch * C:(ch + 1) * C] for ch in range(nch)], axis=1)
    a_wide = _tri_dot(t64_ref[...], wide)
    a = jnp.concatenate([a_wide[:, ch * LANES:(ch + 1) * LANES] for ch in range(nch)], axis=0)
    qa = (q * jnp.exp(a)).astype(BF16)

    rowblk = lax.broadcasted_iota(I32, (C, 1), 0) // SB
    r64 = lax.broadcasted_iota(I32, (C, C), 0)
    c64 = lax.broadcasted_iota(I32, (C, C), 1)
    causal = c64 <= r64
    rr = lax.broadcasted_iota(I32, (LANES, LANES), 0)
    cc = lax.broadcasted_iota(I32, (LANES, LANES), 1)
    same_head = (rr < HG_DK) == (cc < HG_DK)
    nt = (((1,), (1,)), ((), ()))
    tn = (((0,), (0,)), ((), ()))

    scores, updates, decays = [], [], []
    for ch in range(tc // C):
        sl = slice(ch * C, (ch + 1) * C)
        a_c, kk_c, v_c = a[sl], kk[sl], v_ref[0, sl, cols]
        a_last = a_c[C - 1:C, :]
        rows0, rows1 = [], []
        for blk in range(C // SB):
            b0 = ch * C + blk * SB
            e_blk = a_c[blk * SB - 1:blk * SB, :] if blk else jnp.zeros((1, LANES), F32)
            live = rowblk <= blk
            k_blk = jnp.where(live, kk_c * jnp.exp(jnp.where(live, e_blk - a_c, 0.0)), 0.0).astype(BF16)
            q_blk = q[b0:b0 + SB, :] * jnp.exp(a_c[blk * SB:(blk + 1) * SB, :] - e_blk)
            lhs = jnp.concatenate([jnp.where(lo, q_blk, 0.0), jnp.where(lo, 0.0, q_blk)], axis=0).astype(BF16)
            sc = lax.dot_general(lhs, k_blk, nt, preferred_element_type=F32)
            rows0.append(sc[0:SB])
            rows1.append(sc[SB:2 * SB])
        scores.append((jnp.where(causal, jnp.concatenate(rows0, axis=0), 0.0).astype(BF16),
                       jnp.where(causal, jnp.concatenate(rows1, axis=0), 0.0).astype(BF16)))
        kd = (kk_c * jnp.exp(a_last - a_c)).astype(BF16)
        updates.append(lax.dot_general(v_c, kd, tn, preferred_element_type=F32))
        decays.append(jnp.exp(a_last))
    intras = []
    for ch in range(tc // C):
        v_c = v_ref[0, ch * C:(ch + 1) * C, cols]
        intras.append(jnp.where(lo, jnp.dot(scores[ch][0], v_c, preferred_element_type=F32),
                                jnp.dot(scores[ch][1], v_c, preferred_element_type=F32)))
    st = st_ref[...]
    for ch in range(tc // C):
        sl = slice(ch * C, (ch + 1) * C)
        inter = lax.dot_general(qa[sl], st.astype(BF16), nt, preferred_element_type=F32)
        o_acc[sl, :] = inter + intras[ch]
        st = jnp.where(same_head, st * decays[ch] + updates[ch], 0.0)
    st_ref[...] = st

    o = o_acc[...]
    sq = o * o
    hi = sq.astype(BF16)
    lo_t = (sq - hi.astype(F32)).astype(BF16)
    ms = (jnp.dot(hi, hm_ref[...], preferred_element_type=F32)
          + jnp.dot(lo_t, hm_ref[...], preferred_element_type=F32))
    g = gg_ref[0, :, cols].astype(F32)
    silu = g / (1.0 + jnp.exp(-g))
    o_ref[0, :, cols] = (o * lax.rsqrt(ms + RMS_EPS) * gain_ref[:, cols] * silu).astype(BF16)


def _pool_body(u_ref, w_ref, sc_ref, o_ref, ext_ref, *, tc):
    si = pl.program_id(1)
    u = u_ref[0].astype(F32)
    ext_ref[POOL_HALO:POOL_HALO + tc, :] = u
    acc = u
    sums = {}
    for j in range(1, max(POOL_WINDOWS)):
        acc = acc + ext_ref[POOL_HALO - j:POOL_HALO - j + tc, :]
        if j + 1 in POOL_WINDOWS:
            sums[j + 1] = acc
    lane = lax.broadcasted_iota(I32, (1, POOL_WIDTH), 1)
    grp = lane // POOL_GROUP_DIM
    win = sums[POOL_WINDOWS[-1]]
    wlen = jnp.full((1, POOL_WIDTH), float(POOL_WINDOWS[-1]), F32)
    for gi in range(len(POOL_WINDOWS) - 2, -1, -1):
        win = jnp.where(grp == gi, sums[POOL_WINDOWS[gi]], win)
        wlen = jnp.where(grp == gi, float(POOL_WINDOWS[gi]), wlen)
    pos = (si * tc + lax.broadcasted_iota(I32, (tc, 1), 0) + 1).astype(F32)
    pooled = win / jnp.minimum(pos, wlen) - u
    y = jnp.dot(pooled.astype(BF16), w_ref[...], preferred_element_type=F32) * sc_ref[...]
    o_ref[0] = y.astype(BF16)
    ext_ref[0:POOL_HALO, :] = u[tc - POOL_HALO:tc, :]


def _seqmix_body(ff_ref, bf_ref, tri_ref, hsel_ref, q_ref, k_ref,
                 gq_ref, gf_ref, gi_ref, gg_ref, lb_ref, hgain_ref, t64_ref, hm_ref,
                 u_ref, pw_ref, psc_ref,
                 ka_ref, stats_ref, ohg_ref, opool_ref,
                 carry_ref, st_ref, o_acc, ext_ref, *, ts):
    @pl.when(pl.program_id(1) == 0)
    def _():
        carry_ref[...] = jnp.zeros_like(carry_ref)
        st_ref[...] = jnp.zeros_like(st_ref)
        ext_ref[0:POOL_HALO, :] = jnp.zeros((POOL_HALO, POOL_WIDTH), F32)

    _hgrn_body(gq_ref, gf_ref, gi_ref, gg_ref, lb_ref, hgain_ref, t64_ref, hm_ref, ohg_ref, st_ref, o_acc, tc=ts)
    _foxprep_body(ff_ref, bf_ref, tri_ref, hsel_ref, q_ref, k_ref, ka_ref, stats_ref, carry_ref, ts=ts)
    _pool_body(u_ref, pw_ref, psc_ref, opool_ref, ext_ref, tc=ts)


def _seqmix(a3, b3, bf_pad, tri, lb, hgain, tri64, head_mean, pool_bd, pool_scale, ts):
    B, S, _ = a3.shape
    hw = FOX_HEADS * HEAD_BLK
    head_sel = (jnp.arange(FOX_WIDTH)[:, None] // FOX_HEAD_DIM == jnp.arange(LANES)[None, :]).astype(BF16)
    const = lambda *shp: pl.BlockSpec(shp, lambda b, s: tuple(0 for _ in shp))
    cols = lambda w, blk128: pl.BlockSpec((1, ts, w), lambda b, s: (b, s, blk128 * LANES // w))
    halves = HG_HEADS // 2
    return pl.pallas_call(
        functools.partial(_seqmix_body, ts=ts),
        grid=(B, S // ts),
        in_specs=[
            cols(LANES, 2), const(1, LANES), const(ts, ts), const(FOX_WIDTH, LANES),
            cols(FOX_WIDTH, QBLK0), cols(FOX_WIDTH, KBLK0),
            cols(HG_WIDTH, GQ0), cols(HG_WIDTH, 0), cols(HG_WIDTH, GI0), cols(HG_WIDTH, GG0),
            const(1, HG_WIDTH), const(1, HG_WIDTH), const(HG_CHUNK, HG_CHUNK), const(LANES, LANES),
            cols(POOL_WIDTH, 2 * PU_BLK256), const(POOL_WIDTH, POOL_WIDTH), const(1, POOL_WIDTH),
        ],
        out_specs=[pl.BlockSpec((1, ts, hw), lambda b, s: (b, s, 0)),
                   pl.BlockSpec((1, 1, 8, LANES), lambda b, s: (b, s, 0, 0)),
                   pl.BlockSpec((1, ts, HG_WIDTH), lambda b, s: (b, s, 0)),
                   pl.BlockSpec((1, ts, POOL_WIDTH), lambda b, s: (b, s, 0))],
        out_shape=[jax.ShapeDtypeStruct((B, S, hw), BF16),
                   jax.ShapeDtypeStruct((B, S // ts, 8, LANES), F32),
                   jax.ShapeDtypeStruct((B, S, HG_WIDTH), BF16),
                   jax.ShapeDtypeStruct((B, S, POOL_WIDTH), BF16)],
        scratch_shapes=[pltpu.VMEM((1, LANES), F32), pltpu.VMEM((halves, LANES, LANES), F32),
                        pltpu.VMEM((halves, ts, LANES), F32), pltpu.VMEM((POOL_HALO + ts, POOL_WIDTH), F32)],
        compiler_params=_cparams(("parallel", "arbitrary")),
        name="seq_mixers",
    )(b3, bf_pad, tri, head_sel, a3, a3, a3, b3, a3, a3, lb, hgain, tri64, head_mean, a3, pool_bd, pool_scale)


def _router_body(of_ref, oh_ref, op_ref, h_ref, wo_ref, g_ref, wr_ref, br_ref, tri_ref,
                 hn_ref, ri_ref, rg_ref, cnt_ref, carry_ref, *, tm):
    @pl.when(pl.program_id(0) == 0)
    def _():
        carry_ref[...] = jnp.zeros_like(carry_ref)

    f0, f1 = FOX_WIDTH, FOX_WIDTH + HG_WIDTH
    mix = (jnp.dot(of_ref[...], wo_ref[0:f0, :], preferred_element_type=F32)
           + jnp.dot(oh_ref[...], wo_ref[f0:f1, :], preferred_element_type=F32)
           + jnp.dot(op_ref[...], wo_ref[f1:f1 + POOL_WIDTH, :], preferred_element_type=F32))
    hn = h_ref[...] + mix
    hn_ref[...] = hn
    ms = jnp.mean(hn * hn, axis=-1, keepdims=True)
    xn = hn * lax.rsqrt(ms + RMS_EPS) * g_ref[...]
    x1 = xn.astype(BF16)
    x2 = (xn - x1.astype(F32)).astype(BF16)
    r1 = jnp.dot(x1, wr_ref[...], preferred_element_type=F32)
    r2 = jnp.dot(x2, wr_ref[...], preferred_element_type=F32)
    logits = r1[:, :LANES] + r1[:, LANES:] + r2[:, :LANES] + br_ref[...]

    lanef = lax.broadcasted_iota(I32, (tm, LANES), 1).astype(F32)
    l = logits
    vals, idxs, sels = [], [], []
    for _ in range(TOP_K):
        mk = jnp.max(l, axis=-1, keepdims=True)
        ik = jnp.min(jnp.where(l == mk, lanef, float(LANES)), axis=-1, keepdims=True)
        sk = lanef == ik
        vals.append(mk)
        idxs.append(ik)
        sels.append(sk)
        l = jnp.where(sk, -jnp.inf, l)
    es = [jnp.exp(v - vals[0]) for v in vals]
    den = es[0] + es[1] + es[2] + es[3]
    sel = jnp.zeros((tm, LANES), F32)
    for sk in sels:
        sel = jnp.where(sk, 1.0, sel)
    rank = jnp.dot(tri_ref[...], sel.astype(BF16), preferred_element_type=F32) + carry_ref[...]
    carry = carry_ref[...] + jnp.sum(sel, axis=0, keepdims=True)
    carry_ref[...] = carry
    cnt_ref[...] = carry
    ri = jnp.zeros((tm, LANES), F32)
    rg = jnp.zeros((tm, LANES), F32)
    for k in range(TOP_K):
        pos_k = jnp.sum(jnp.where(sels[k], rank, 0.0), axis=-1, keepdims=True)
        ri = jnp.where(lanef == float(k), idxs[k], ri)
        ri = jnp.where(lanef == float(TOP_K + k), pos_k, ri)
        rg = jnp.where(lanef == float(k), es[k] / den, rg)
    ri_ref[...] = ri.astype(I32)
    rg_ref[...] = rg


def _router(o_fox, o_hg, o_pool, h, wo, g, wr, br, tri, tm):
    T, D = h.shape
    rowblk = lambda w: pl.BlockSpec((tm, w), lambda i: (i, 0))
    const = lambda shp: pl.BlockSpec(shp, lambda i: tuple(0 for _ in shp))
    return pl.pallas_call(
        functools.partial(_router_body, tm=tm),
        grid=(T // tm,),
        in_specs=[rowblk(FOX_WIDTH), rowblk(HG_WIDTH), rowblk(POOL_WIDTH), rowblk(D),
                  const((D, D)), const((1, D)), const((D, 2 * LANES)), const((1, LANES)), const((tm, tm))],
        out_specs=[rowblk(D), rowblk(LANES), rowblk(LANES), const((1, LANES))],
        out_shape=[jax.ShapeDtypeStruct((T, D), F32), jax.ShapeDtypeStruct((T, LANES), I32),
                   jax.ShapeDtypeStruct((T, LANES), F32), jax.ShapeDtypeStruct((1, LANES), F32)],
        scratch_shapes=[pltpu.VMEM((1, LANES), F32)],
        compiler_params=_cparams(("arbitrary",)),
        name="outproj_router",
    )(o_fox, o_hg, o_pool, h, wo, g, wr, br, tri)


def _to_row_tiles(ref, x):
    ref[...] = pltpu.einshape("r(cl)->rcl", x, c=ref.shape[-2])


def _from_row_tiles(ref):
    return pltpu.einshape("rcl->r(cl)", ref[...])


def _row_copy(src_ref, src_row, dst_ref, dst_row, sem):
    return pltpu.make_async_copy(src_ref.at[pl.ds(src_row, 1)], dst_ref.at[pl.ds(dst_row, 1)], sem)


def _wait_rows(tile_ref, sem, times):
    for _ in range(times):
        pltpu.make_async_copy(tile_ref, tile_ref, sem).wait()


def _dispatch_body(pad0_ref, padn_ref, dst_ref, h_ref, g_ref, buf_ref, xn_ref, zero_ref, sem, zsem, *, tt, bm):
    i = pl.program_id(0)
    n = pl.num_programs(0)
    slot = i % 2
    stage = xn_ref.at[slot]

    @pl.when(i >= 2)
    def _():
        _wait_rows(stage, sem.at[slot], TOP_K)

    x = h_ref[...]
    ms = jnp.mean(x * x, axis=-1, keepdims=True)
    _to_row_tiles(stage, x * lax.rsqrt(ms + RMS_EPS) * g_ref[...])

    def issue(r, _):
        for k in range(TOP_K):
            _row_copy(stage, r, buf_ref, dst_ref[0, 0, r * TOP_K + k], sem.at[slot]).start(priority=k % 2)
        return 0

    lax.fori_loop(0, tt, issue, 0, unroll=DMA_UNROLL)

    @pl.when(i == 0)
    def _():
        zero_ref[...] = jnp.zeros_like(zero_ref)
        half = bm // 2
        n_exp = pad0_ref.shape[0] - 1

        def run(cp, pred, wait):
            @pl.when(pred)
            def _():
                cp.wait() if wait else cp.start()

        def pad_copies(e, wait):
            off, cnt = pad0_ref[e], padn_ref[e]
            for r in range(SUBLANES - 1):
                run(_row_copy(zero_ref, 0, buf_ref, off + r, zsem), r < (cnt & (SUBLANES - 1)), wait)
            off = off + (cnt & (SUBLANES - 1))
            sz = SUBLANES
            while sz <= half:
                take = (cnt & sz) != 0
                dst = buf_ref.at[pl.ds(pl.multiple_of(off, SUBLANES), sz)]
                run(pltpu.make_async_copy(zero_ref.at[pl.ds(0, sz)], dst, zsem), take, wait)
                off = off + jnp.where(take, sz, 0)
                sz *= 2
            return 0

        def tail_copy(c, wait):
            dst = buf_ref.at[pl.ds(pl.multiple_of(pad0_ref[n_exp] + c * half, SUBLANES), half)]
            cp = pltpu.make_async_copy(zero_ref, dst, zsem)
            cp.wait() if wait else cp.start()
            return 0

        n_tail = padn_ref[n_exp] // half
        for wait in (False, True):
            lax.fori_loop(0, n_exp, lambda e, c, w=wait: pad_copies(e, w), 0)
            lax.fori_loop(0, n_tail, lambda c, _, w=wait: tail_copy(c, w), 0)

    @pl.when(i == n - 1)
    def _():
        _wait_rows(stage, sem.at[slot], TOP_K)

    @pl.when((i == n - 1) & (n >= 2))
    def _():
        _wait_rows(xn_ref.at[1 - slot], sem.at[1 - slot], TOP_K)


def _dispatch(pad0, padn, dst3, hn, g, R, tt, bm):
    T, D = hn.shape
    grid_spec = pltpu.PrefetchScalarGridSpec(
        num_scalar_prefetch=2,
        grid=(T // tt,),
        in_specs=[
            pl.BlockSpec((1, 1, tt * TOP_K), lambda i, p0, pn: (i, 0, 0), memory_space=pltpu.SMEM),
            pl.BlockSpec((tt, D), lambda i, p0, pn: (i, 0)),
            pl.BlockSpec((1, D), lambda i, p0, pn: (0, 0)),
        ],
        out_specs=pl.BlockSpec(memory_space=pl.ANY),
        scratch_shapes=[pltpu.VMEM((2, tt, D // LANES, LANES), F32), pltpu.VMEM((bm // 2, D // LANES, LANES), F32),
                        pltpu.SemaphoreType.DMA((2,)), pltpu.SemaphoreType.DMA],
    )
    return pl.pallas_call(
        functools.partial(_dispatch_body, tt=tt, bm=bm),
        grid_spec=grid_spec,
        out_shape=jax.ShapeDtypeStruct((R, D // LANES, LANES), F32),
        compiler_params=_cparams(("arbitrary",)),
        name="moe_dispatch",
    )(pad0, padn, dst3, hn, g)


MXU_DIM = 256


def _expert_body(te_ref, nu_ref, x_ref, w1_ref, w2_ref, perm_ref, b1g_ref, b1l_ref, b2_ref, y_ref,
                 w1g_scr, w1l_scr, w2_scr):
    i = pl.program_id(0)
    live = i < nu_ref[0]
    prev = te_ref[jnp.maximum(i - 1, 0)]

    @pl.when(live & ((i == 0) | (te_ref[i] != prev)))
    def _():
        half = MXU_DIM // 2
        for grp in range(w1_ref.shape[-1] // MXU_DIM):
            w = w1_ref[0, :, grp * MXU_DIM:(grp + 1) * MXU_DIM].astype(BF16)
            r = jnp.dot(w, perm_ref[...], preferred_element_type=F32).astype(BF16)
            w1g_scr[:, grp * half:(grp + 1) * half] = r[:, :half]
            w1l_scr[:, grp * half:(grp + 1) * half] = r[:, half:]
        w2_scr[...] = w2_ref[0].astype(BF16)

    @pl.when(live)
    def _():
        x = _from_row_tiles(x_ref).astype(BF16)
        glu = jnp.dot(x, w1g_scr[...], preferred_element_type=F32) + b1g_ref[0]
        lin = jnp.dot(x, w1l_scr[...], preferred_element_type=F32) + b1l_ref[0]
        glu = jnp.minimum(glu, SWIGLU_LIMIT)
        lin = jnp.clip(lin, -SWIGLU_LIMIT, SWIGLU_LIMIT)
        act = glu / (1.0 + jnp.exp(-SWIGLU_ALPHA * glu)) * (lin + 1.0)
        _to_row_tiles(y_ref, jnp.dot(act.astype(BF16), w2_scr[...], preferred_element_type=F32) + b2_ref[0])

    @pl.when(jnp.logical_not(live))
    def _():
        y_ref[...] = jnp.zeros_like(y_ref)


def _experts(tile_e, n_used, buf, w1, w2, b1g, b1l, b2, bm, first_expert):
    R, dsub, _ = buf.shape
    D = dsub * LANES
    F2 = w1.shape[-1]
    F = F2 // 2
    half = MXU_DIM // 2
    pi = jnp.arange(MXU_DIM)[:, None]
    pj = jnp.arange(MXU_DIM)[None, :]
    perm = jnp.where(pj < half, pi == 2 * pj, pi == 2 * (pj - half) + 1).astype(BF16)
    rows = pl.BlockSpec((bm, dsub, LANES), lambda i, te, nu: (jnp.minimum(i, nu[0] - 1), 0, 0))
    wspec = lambda a, b, e0=0: pl.BlockSpec((1, a, b), lambda i, te, nu: (e0 + te[i], 0, 0))
    grid_spec = pltpu.PrefetchScalarGridSpec(
        num_scalar_prefetch=2,
        grid=(R // bm,),
        in_specs=[rows, wspec(D, F2, first_expert), wspec(F, D, first_expert),
                  pl.BlockSpec((MXU_DIM, MXU_DIM), lambda i, te, nu: (0, 0)),
                  wspec(1, F), wspec(1, F), wspec(1, D)],
        out_specs=pl.BlockSpec((bm, dsub, LANES), lambda i, te, nu: (i, 0, 0)),
        scratch_shapes=[pltpu.VMEM((D, F), BF16), pltpu.VMEM((D, F), BF16), pltpu.VMEM((F, D), BF16)],
    )
    return pl.pallas_call(
        _expert_body,
        grid_spec=grid_spec,
        out_shape=jax.ShapeDtypeStruct((R, dsub, LANES), F32),
        compiler_params=_cparams(("arbitrary",), vmem=EXPERT_VMEM_LIMIT),
        name="moe_experts",
    )(tile_e, n_used, buf, w1, w2, perm, b1g, b1l, b2)


def _combine_body(dst_ref, h_ref, rg_ref, gf_ref, y_ref, o_ref, rows_ref, sem, *, tt, final_norm):
    s = pl.program_id(0)
    n_tiles = pl.num_programs(0) - 1

    @pl.when(s < n_tiles)
    def _():
        slot = s % 2

        def issue(r, _):
            for k in range(TOP_K):
                _row_copy(y_ref, dst_ref[0, 0, r * TOP_K + k], rows_ref.at[slot, k], r,
                          sem.at[slot]).start(priority=k % 2)
            return 0

        lax.fori_loop(0, tt, issue, 0, unroll=DMA_UNROLL)

    @pl.when(s >= 1)
    def _():
        slot = (s - 1) % 2
        _wait_rows(rows_ref.at[slot, 0], sem.at[slot], TOP_K)
        out = h_ref[...]
        gates = rg_ref[...]
        for k in range(TOP_K):
            out = out + gates[:, k:k + 1] * _from_row_tiles(rows_ref.at[slot, k])
        if final_norm:
            ms = jnp.mean(out * out, axis=-1, keepdims=True)
            out = out * lax.rsqrt(ms + RMS_EPS) * gf_ref[...]
        o_ref[...] = out


def _combine(dst3, hn, rg, g_final, ybuf, tt, final_norm):
    T, D = hn.shape
    n_tiles = T // tt
    prev = lambda s: (jnp.maximum(s - 1, 0), 0)
    return pl.pallas_call(
        functools.partial(_combine_body, tt=tt, final_norm=final_norm),
        grid=(n_tiles + 1,),
        in_specs=[
            pl.BlockSpec((1, 1, tt * TOP_K), lambda s: (jnp.minimum(s, n_tiles - 1), 0, 0), memory_space=pltpu.SMEM),
            pl.BlockSpec((tt, D), prev),
            pl.BlockSpec((tt, LANES), prev),
            pl.BlockSpec((1, D), lambda s: (0, 0)),
            pl.BlockSpec(memory_space=pl.ANY),
        ],
        out_specs=pl.BlockSpec((tt, D), prev),
        out_shape=jax.ShapeDtypeStruct((T, D), F32),
        scratch_shapes=[pltpu.VMEM((2, TOP_K, tt, D // LANES, LANES), F32), pltpu.SemaphoreType.DMA((2,))],
        compiler_params=_cparams(("arbitrary",)),
        name="moe_combine",
    )(dst3, hn, rg, g_final, ybuf)


def _tiles(B, S):
    T = B * S
    ts = min(512, S)
    tm = min(512, T)
    tt = min(256, T)
    bm = min(512, T)
    return ts, tm, tt, bm


def _block_tri(n, blk, strict=False):
    r = jnp.arange(n)[:, None]
    c = jnp.arange(n)[None, :]
    keep = (c < r) if strict else (c <= r)
    return (keep & (r // blk == c // blk)).astype(BF16)


def kernel(x, norm_mix, w_in, b_fox_f, fox_gain, hgrn_lb, hgrn_gain, pool_w, pool_scale, w_out,
           norm_moe, w_router, b_router, w1, b1, w2, b2, norm_final):
    B, S, D = x.shape
    L = w_in.shape[0]
    T = B * S
    E = w1.shape[1]
    ts, tm, tt, bm = _tiles(B, S)

    offs = [0]
    for wdt in (FOX_WIDTH, FOX_WIDTH, FOX_WIDTH, FOX_HEADS, HG_WIDTH, HG_WIDTH, HG_WIDTH, HG_WIDTH, POOL_WIDTH):
        offs.append(offs[-1] + wdt)
    col = lambda i: w_in[:, :, offs[i]:offs[i + 1]]
    scale = FOX_HEAD_DIM ** -0.5 * LOG2E
    wa = jnp.concatenate([col(0) * scale, col(1), col(2), col(4), col(6), col(7), col(8)], axis=-1).astype(BF16)
    wb = jnp.concatenate([col(5), col(3), jnp.zeros((L, D, LANES - FOX_HEADS), F32)], axis=-1).astype(BF16)
    bf_pad = jnp.pad(b_fox_f, ((0, 0), (0, LANES - FOX_HEADS)))
    p = jax.nn.softmax(hgrn_lb.astype(F32), axis=0)
    cs = jnp.cumsum(p, axis=0)
    lbs = cs - cs[0:1]
    eye = jnp.eye(len(POOL_WINDOWS), dtype=F32)
    pool_bd = jnp.einsum("lgcd,gh->lgchd", pool_w, eye).reshape(L, POOL_WIDTH, POOL_WIDTH).astype(BF16)
    wr_pad = jnp.pad(w_router, ((0, 0), (0, 0), (0, LANES - E)))
    wr_hi = wr_pad.astype(BF16)
    wr_lo = (wr_pad - wr_hi.astype(F32)).astype(BF16)
    wr = jnp.concatenate([wr_hi, wr_lo], axis=-1)
    br_pad = jnp.pad(b_router, ((0, 0), (0, LANES - E)), constant_values=NEG_BIG)
    w1_all = w1.reshape((L * E,) + w1.shape[2:])
    w2_all = w2.reshape((L * E,) + w2.shape[2:])
    b1g = b1[..., 0::2][:, :, None, :]
    b1l = b1[..., 1::2][:, :, None, :]
    b2r = b2[:, :, None, :]

    tri_seq = _block_tri(ts, ts)
    tri64 = _block_tri(HG_CHUNK, HG_CHUNK)
    li = jnp.arange(LANES)
    head_mean = ((li[:, None] // HG_DK == li[None, :] // HG_DK) / HG_DK).astype(BF16)
    tri_tok = _block_tri(tm, tm, strict=True)

    M = T * TOP_K
    R = -(-M // bm) * bm + E * bm
    NB = R // bm

    h = x.reshape(T, D)
    for l in range(L):
        a2, b2d = _inproj(h, norm_mix[l][None], wa[l], wb[l], tm)
        a3 = a2.reshape(B, S, A_COLS)
        b3 = b2d.reshape(B, S, B_COLS)
        ka, stats, o_hg, o_pool = _seqmix(a3, b3, bf_pad[l][None], tri_seq, lbs[l][None], hgrn_gain[l][None],
                                          tri64, head_mean, pool_bd[l], pool_scale[l][None], ts)
        heads = _fox_pairing(b_fox_f[l])
        by_head = lambda w: w.reshape((FOX_HEADS, FOX_HEAD_DIM) + w.shape[1:])[heads].reshape(w.shape)
        o_fox = _fox(_fox_first_tiles(stats), heads, a3, ka, by_head(fox_gain[l])[None], ts, min(32, ts))
        wo_l = jnp.concatenate([by_head(w_out[l, :FOX_WIDTH]), w_out[l, FOX_WIDTH:]], axis=0).astype(BF16)
        hn, ri, rg, cnt = _router(o_fox.reshape(T, FOX_WIDTH), o_hg.reshape(T, HG_WIDTH),
                                  o_pool.reshape(T, POOL_WIDTH), h, wo_l, norm_moe[l][None],
                                  wr[l], br_pad[l][None], tri_tok, tm)
        counts = cnt[0, :E].astype(I32)
        padded = (counts + bm - 1) // bm * bm
        pend = jnp.cumsum(padded)
        pstart = pend - padded
        first_row = jnp.sum(jnp.where(ri[:, 0:TOP_K, None] == jnp.arange(E, dtype=I32), pstart.astype(I32), 0), axis=-1)
        dst = first_row + ri[:, TOP_K:2 * TOP_K]
        n_used = (pend[-1] // bm).astype(I32)
        tile_ids = jnp.minimum(jnp.arange(NB, dtype=I32), n_used - 1)
        tile_e = jnp.minimum(jnp.sum(pend[None, :] <= (tile_ids * bm)[:, None], axis=1), E - 1).astype(I32)
        dst3 = dst.reshape(T // tt, 1, tt * TOP_K)
        pad0 = jnp.concatenate([pstart + counts, pend[-1:]]).astype(I32)
        padn = jnp.concatenate([padded - counts, R - pend[-1:]]).astype(I32)
        buf = _dispatch(pad0, padn, dst3, hn, norm_moe[l][None], R, tt, bm)
        ybuf = _experts(tile_e, n_used.reshape(1), buf, w1_all, w2_all, b1g[l], b1l[l], b2r[l], bm, l * E)
        h = _combine(dst3, hn, rg, norm_final[None], ybuf, tt, final_norm=(l == L - 1))
    return h.reshape(B, S, D)
```

```python
import functools

import jax
import jax.numpy as jnp
from jax import lax
from jax.experimental import pallas as pl
from jax.experimental.pallas import tpu as pltpu

F32 = jnp.float32
BF16 = jnp.bfloat16
I32 = jnp.int32

LANES = 128
SUBLANES = 8
VMEM_LIMIT = 56 * 1024 * 1024
EXPERT_VMEM_LIMIT = 60 * 1024 * 1024

FOX_HEADS = 8
FOX_HEAD_DIM = 64
FOX_WIDTH = FOX_HEADS * FOX_HEAD_DIM
HG_HEADS = 4
HG_DK = 64
HG_WIDTH = HG_HEADS * HG_DK
HG_CHUNK = 64
HG_SUB = 16
POOL_WINDOWS = (2, 4, 8, 16)
POOL_GROUP_DIM = 64
POOL_WIDTH = len(POOL_WINDOWS) * POOL_GROUP_DIM
POOL_HALO = 16
TOP_K = 4
SWIGLU_LIMIT = 7.0
SWIGLU_ALPHA = 1.702
RMS_EPS = 1e-5
NEG_BIG = -1e30
DMA_UNROLL = 8

HEAD_BLK = LANES
QBLK0, KBLK0, VBLK0 = 0, FOX_HEADS // 2, FOX_HEADS
GQ0 = 3 * FOX_HEADS // 2
GI0, GG0 = GQ0 + 2, GQ0 + 4
PU_BLK256 = (GQ0 + 6) // 2
A_COLS = (GQ0 + 8) * LANES
B_COLS = 384
BIAS_TERMS = 3
LOG2E = 1.4426950408889634
SKIP_LOG2 = 150.0
NORM_SLACK = 1.01
MXU_DIM = 256


def _head_lanes(parity):
    lane = lax.broadcasted_iota(I32, (1, LANES), 1)
    lower = lane < FOX_HEAD_DIM
    return (lower, FOX_HEAD_DIM) if parity == 0 else (jnp.logical_not(lower), 0)


def _cparams(sem, vmem=VMEM_LIMIT):
    return pltpu.CompilerParams(dimension_semantics=sem, vmem_limit_bytes=vmem)


def _split3(x):
    x1 = x.astype(BF16)
    r1 = x - x1.astype(F32)
    x2 = r1.astype(BF16)
    x3 = (r1 - x2.astype(F32)).astype(BF16)
    return x1, x2, x3


def _tri_dot(tri, x):
    x1, x2, x3 = _split3(x)
    d = lambda a: jnp.dot(tri, a, preferred_element_type=F32)
    return d(x1) + d(x2) + d(x3)


def _log_sigmoid(z):
    return jnp.minimum(z, 0.0) - jnp.log1p(jnp.exp(-jnp.abs(z)))


def _inproj_body(x_ref, g_ref, wa_ref, wb_ref, a_ref, b_ref):
    x = x_ref[...]
    ms = jnp.mean(x * x, axis=-1, keepdims=True)
    xn = (x * lax.rsqrt(ms + RMS_EPS) * g_ref[...]).astype(BF16)
    a_ref[...] = jnp.dot(xn, wa_ref[...], preferred_element_type=F32).astype(BF16)
    b_ref[...] = jnp.dot(xn, wb_ref[...], preferred_element_type=F32)


def _inproj(h, g, wa, wb, tm):
    T, D = h.shape
    return pl.pallas_call(
        _inproj_body,
        grid=(T // tm,),
        in_specs=[
            pl.BlockSpec((tm, D), lambda i: (i, 0)),
            pl.BlockSpec((1, D), lambda i: (0, 0)),
            pl.BlockSpec((D, A_COLS), lambda i: (0, 0)),
            pl.BlockSpec((D, B_COLS), lambda i: (0, 0)),
        ],
        out_specs=[
            pl.BlockSpec((tm, A_COLS), lambda i: (i, 0)),
            pl.BlockSpec((tm, B_COLS), lambda i: (i, 0)),
        ],
        out_shape=[jax.ShapeDtypeStruct((T, A_COLS), BF16), jax.ShapeDtypeStruct((T, B_COLS), F32)],
        compiler_params=_cparams(("parallel",)),
        name="inproj",
    )(h, g, wa, wb)


def _foxprep_body(ff_ref, bf_ref, tri_ref, hsel_ref, q_ref, k_ref, ka_ref, st_ref, carry_ref, *, ts):
    logf = _log_sigmoid(ff_ref[0] + bf_ref[...])
    c = _tri_dot(tri_ref[...], logf) + carry_ref[...]
    carry_ref[...] = c[ts - 1:ts, :]
    cl = c * LOG2E
    lane = lax.broadcasted_iota(I32, (1, LANES), 1)
    for h in range(FOX_HEADS):
        feat, spare0 = _head_lanes(h % 2)
        terms = _split3(jnp.broadcast_to(-cl[:, h:h + 1], (ts, LANES)))
        pair = k_ref[0, :, (h // 2) * HEAD_BLK:(h // 2 + 1) * HEAD_BLK]
        ka = pair * feat.astype(BF16)
        for i, term in enumerate(terms):
            ka = jnp.where(lane == spare0 + i, term, ka)
        ka_ref[0, :, h * HEAD_BLK:(h + 1) * HEAD_BLK] = ka

    def max_norm(ref):
        x = ref[0].astype(F32)
        n2 = jnp.dot((x * x).astype(BF16), hsel_ref[...], preferred_element_type=F32)
        return jnp.sqrt(jnp.max(n2, axis=0, keepdims=True)) * NORM_SLACK

    qn_row = max_norm(q_ref)
    kn_row = max_norm(k_ref)
    rowi = lax.broadcasted_iota(I32, (8, LANES), 0)
    st = jnp.where(rowi == 0, qn_row, jnp.where(rowi == 1, kn_row, jnp.where(rowi == 2, cl[0:1, :],
                   jnp.where(rowi == 3, cl[ts - 1:ts, :], 0.0))))
    st_ref[0, 0] = st


def _fox_body(js_ref, hd_ref, qa_ref, qb_ref, ka_ref, kb_ref, va_ref, vb_ref, g_ref, o_ref,
              s_ref, p_ref, m_ref, acc_ref, vp_ref, *, t, rc):
    q_refs, k_refs, v_refs = (qa_ref, qb_ref), (ka_ref, kb_ref), (va_ref, vb_ref)
    b = pl.program_id(0)
    hp = pl.program_id(1)
    qi = pl.program_id(2)
    nt = pl.num_programs(2)
    lane = lax.broadcasted_iota(I32, (1, LANES), 1)
    lo = lane < FOX_HEAD_DIM
    feat, sum_l, keep, one_at, qs = [], [], [], [], []
    for hh in range(2):
        f, spare0 = _head_lanes(hh)
        feat.append(f)
        sum_l.append(lane == spare0)
        keep.append(f.astype(BF16))
        one_at.append((lane == spare0).astype(BF16))
        ones_q = ((lane >= spare0) & (lane < spare0 + BIAS_TERMS)).astype(BF16)
        qs.append(q_refs[hh][0] * keep[hh] + ones_q)

    @pl.when(qi == 0)
    def _():
        for hh in range(2):
            vp_ref[hh] = v_refs[hh][0] * keep[hh] + one_at[hh]

    m_ref[...] = jnp.full(m_ref.shape, -jnp.inf, F32)
    acc_ref[...] = jnp.zeros(acc_ref.shape, F32)
    first = [js_ref[(b * FOX_HEADS + hd_ref[2 * hp + hh]) * nt + qi] for hh in range(2)]
    j0 = jnp.minimum(first[0], first[1])
    nt_dims = (((1,), (1,)), ((), ()))
    nblk = t // LANES

    def tile(j, masked):
        start = pl.multiple_of(j * t, t)
        for hh in range(2):
            kt = k_refs[hh][0, pl.ds(start, t), :]
            s_ref[hh] = lax.dot_general(qs[hh], kt, nt_dims, preferred_element_type=F32)
        for hh in range(2):
            for r in range(t // rc):
                rows = slice(r * rc, (r + 1) * rc)
                diag_blk = (r * rc) // LANES
                mx = None
                for cb in range(diag_blk + 1 if masked else nblk):
                    cols = slice(cb * LANES, (cb + 1) * LANES)
                    sc = s_ref[hh, rows, cols]
                    if masked and cb == diag_blk:
                        ri = lax.broadcasted_iota(I32, (rc, LANES), 0) + r * rc
                        ci = lax.broadcasted_iota(I32, (rc, LANES), 1) + cb * LANES
                        sc = jnp.where(ci <= ri, sc, -jnp.inf)
                        s_ref[hh, rows, cols] = sc
                    mx = sc if mx is None else jnp.maximum(mx, sc)
                m_old = m_ref[hh, rows, :]
                m_new = jnp.maximum(m_old, jnp.broadcast_to(jnp.max(mx, axis=-1, keepdims=True), (rc, LANES)))
                m_ref[hh, rows, :] = m_new
                acc_ref[hh, rows, :] = jnp.exp2(m_old - m_new) * acc_ref[hh, rows, :]
            for r in range(t // rc):
                rows = slice(r * rc, (r + 1) * rc)
                diag_blk = (r * rc) // LANES
                m_new = m_ref[hh, rows, :]
                for cb in range(nblk):
                    cols = slice(cb * LANES, (cb + 1) * LANES)
                    if masked and cb > diag_blk:
                        p_ref[hh, rows, cols] = jnp.zeros((rc, LANES), BF16)
                    else:
                        p_ref[hh, rows, cols] = jnp.exp2(s_ref[hh, rows, cols] - m_new).astype(BF16)
            acc_ref[hh] += jnp.dot(p_ref[hh], vp_ref[hh, pl.ds(start, t), :], preferred_element_type=F32)

    def body(j, carry):
        tile(j, False)
        return carry

    lax.fori_loop(j0, qi, body, 0)
    tile(qi, True)
    outs = []
    for hh in range(2):
        acc = acc_ref[hh]
        w = jnp.where(feat[hh], 1.0 / FOX_HEAD_DIM, jnp.where(sum_l[hh], RMS_EPS, 0.0))
        outs.append(acc * lax.rsqrt(jnp.sum(acc * acc * w, axis=-1, keepdims=True)))
    o_ref[0] = (jnp.where(lo, outs[0], outs[1]) * g_ref[...]).astype(BF16)


def _fox_first_tiles(stats):
    qn, kn, cs, ce = (stats[:, :, i, :FOX_HEADS].transpose(0, 2, 1) for i in range(4))
    bound = cs[..., :, None] - ce[..., None, :] + qn[..., :, None] * (kn[..., None, :] + kn[..., :, None])
    nt = qn.shape[-1]
    ti = jnp.arange(nt)[:, None]
    tj = jnp.arange(nt)[None, :]
    need = (tj == ti) | ((tj < ti) & (bound >= -SKIP_LOG2))
    return jnp.argmax(need, axis=-1).astype(I32).reshape(-1)


def _fox_pairing(b_f):
    even = jnp.argsort(b_f[0::2]) * 2
    odd = jnp.argsort(b_f[1::2]) * 2 + 1
    return jnp.stack([even, odd], axis=1).reshape(-1).astype(I32)


def _fox(first, heads, a3, ka, gain, t, rc):
    B, S, _ = a3.shape
    qblk = lambda side: pl.BlockSpec(
        (1, t, HEAD_BLK), lambda b, hp, qi, js, hd: (b, qi, QBLK0 + hd[2 * hp + side] // 2))
    kblk = lambda side: pl.BlockSpec(
        (1, S, HEAD_BLK), lambda b, hp, qi, js, hd: (b, 0, hd[2 * hp + side]))
    vblk = lambda side: pl.BlockSpec(
        (1, S, HEAD_BLK), lambda b, hp, qi, js, hd: (b, 0, VBLK0 + hd[2 * hp + side] // 2))
    grid_spec = pltpu.PrefetchScalarGridSpec(
        num_scalar_prefetch=2,
        grid=(B, FOX_HEADS // 2, S // t),
        in_specs=[
            qblk(0), qblk(1), kblk(0), kblk(1), vblk(0), vblk(1),
            pl.BlockSpec((1, LANES), lambda b, hp, qi, js, hd: (0, hp)),
        ],
        out_specs=pl.BlockSpec((1, t, LANES), lambda b, hp, qi, js, hd: (b, qi, hp)),
        scratch_shapes=[pltpu.VMEM((2, t, t), F32), pltpu.VMEM((2, t, t), BF16),
                        pltpu.VMEM((2, t, LANES), F32), pltpu.VMEM((2, t, LANES), F32),
                        pltpu.VMEM((2, S, HEAD_BLK), BF16)],
    )
    return pl.pallas_call(
        functools.partial(_fox_body, t=t, rc=rc),
        grid_spec=grid_spec,
        out_shape=jax.ShapeDtypeStruct((B, S, FOX_WIDTH), BF16),
        compiler_params=_cparams(("parallel", "parallel", "arbitrary")),
        name="fox_attention",
    )(first, heads, a3, a3, ka, ka, a3, a3, gain)


def _hgrn_body(q_ref, f_ref, v_ref, gg_ref, lb_ref, gain_ref, t64_ref, hm_ref, o_ref, st_ref, o_acc, *, tc):
    for half in range(HG_HEADS // 2):
        _hgrn_half(slice(half * LANES, (half + 1) * LANES), q_ref, f_ref, v_ref, gg_ref, lb_ref, gain_ref,
                   t64_ref, hm_ref, o_ref, st_ref.at[half], o_acc.at[half], tc)


def _hgrn_half(cols, q_ref, f_ref, v_ref, gg_ref, lb_ref, gain_ref, t64_ref, hm_ref, o_ref, st_ref, o_acc, tc):
    C, SB = HG_CHUNK, HG_SUB
    lane = lax.broadcasted_iota(I32, (1, LANES), 1)
    lo = lane < HG_DK
    lb = lb_ref[:, cols]
    fl = f_ref[0, :, cols]
    t2 = jnp.log1p(-lb) + _log_sigmoid(fl)
    t1 = jnp.where(lb > 0.0, jnp.log(jnp.maximum(lb, 1e-37)), NEG_BIG)
    logf = jnp.maximum(t1, t2) + jnp.log1p(jnp.exp(-jnp.abs(t1 - t2)))
    kk = (1.0 - lb) / (1.0 + jnp.exp(fl))
    q = q_ref[0, :, cols].astype(F32)
    nch = tc // C
    wide = jnp.concatenate([logf[ch * C:(ch + 1) * C] for ch in range(nch)], axis=1)
    a_wide = _tri_dot(t64_ref[...], wide)
    a = jnp.concatenate([a_wide[:, ch * LANES:(ch + 1) * LANES] for ch in range(nch)], axis=0)
    qa = (q * jnp.exp(a)).astype(BF16)

    rowblk = lax.broadcasted_iota(I32, (C, 1), 0) // SB
    r64 = lax.broadcasted_iota(I32, (C, C), 0)
    c64 = lax.broadcasted_iota(I32, (C, C), 1)
    causal = c64 <= r64
    rr = lax.broadcasted_iota(I32, (LANES, LANES), 0)
    cc = lax.broadcasted_iota(I32, (LANES, LANES), 1)
    same_head = (rr < HG_DK) == (cc < HG_DK)
    nt = (((1,), (1,)), ((), ()))
    tn = (((0,), (0,)), ((), ()))

    scores, updates, decays = [], [], []
    for ch in range(tc // C):
        sl = slice(ch * C, (ch + 1) * C)
        a_c, kk_c, v_c = a[sl], kk[sl], v_ref[0, sl, cols]
        a_last = a_c[C - 1:C, :]
        rows0, rows1 = [], []
        for blk in range(C // SB):
            b0 = ch * C + blk * SB
            e_blk = a_c[blk * SB - 1:blk * SB, :] if blk else jnp.zeros((1, LANES), F32)
            live = rowblk <= blk
            k_blk = jnp.where(live, kk_c * jnp.exp(jnp.where(live, e_blk - a_c, 0.0)), 0.0).astype(BF16)
            q_blk = q[b0:b0 + SB, :] * jnp.exp(a_c[blk * SB:(blk + 1) * SB, :] - e_blk)
            lhs = jnp.concatenate([jnp.where(lo, q_blk, 0.0), jnp.where(lo, 0.0, q_blk)], axis=0).astype(BF16)
            sc = lax.dot_general(lhs, k_blk, nt, preferred_element_type=F32)
            rows0.append(sc[0:SB])
            rows1.append(sc[SB:2 * SB])
        scores.append((jnp.where(causal, jnp.concatenate(rows0, axis=0), 0.0).astype(BF16),
                       jnp.where(causal, jnp.concatenate(rows1, axis=0), 0.0).astype(BF16)))
        kd = (kk_c * jnp.exp(a_last - a_c)).astype(BF16)
        updates.append(lax.dot_general(v_c, kd, tn, preferred_element_type=F32))
        decays.append(jnp.exp(a_last))
    intras = []
    for ch in range(tc // C):
        v_c = v_ref[0, ch * C:(ch + 1) * C, cols]
        intras.append(jnp.where(lo, jnp.dot(scores[ch][0], v_c, preferred_element_type=F32),
                                jnp.dot(scores[ch][1], v_c, preferred_element_type=F32)))
    st = st_ref[...]
    for ch in range(tc // C):
        sl = slice(ch * C, (ch + 1) * C)
        inter = lax.dot_general(qa[sl], st.astype(BF16), nt, preferred_element_type=F32)
        o_acc[sl, :] = inter + intras[ch]
        st = jnp.where(same_head, st * decays[ch] + updates[ch], 0.0)
    st_ref[...] = st

    o = o_acc[...]
    sq = o * o
    hi = sq.astype(BF16)
    lo_t = (sq - hi.astype(F32)).astype(BF16)
    ms = (jnp.dot(hi, hm_ref[...], preferred_element_type=F32)
          + jnp.dot(lo_t, hm_ref[...], preferred_element_type=F32))
    g = gg_ref[0, :, cols].astype(F32)
    silu = g / (1.0 + jnp.exp(-g))
    o_ref[0, :, cols] = (o * lax.rsqrt(ms + RMS_EPS) * gain_ref[:, cols] * silu).astype(BF16)


def _pool_body(u_ref, w_ref, sc_ref, o_ref, ext_ref, *, tc):
    si = pl.program_id(1)
    u = u_ref[0].astype(F32)
    ext_ref[POOL_HALO:POOL_HALO + tc, :] = u
    acc = u
    sums = {}
    for j in range(1, max(POOL_WINDOWS)):
        acc = acc + ext_ref[POOL_HALO - j:POOL_HALO - j + tc, :]
        if j + 1 in POOL_WINDOWS:
            sums[j + 1] = acc
    lane = lax.broadcasted_iota(I32, (1, POOL_WIDTH), 1)
    grp = lane // POOL_GROUP_DIM
    win = sums[POOL_WINDOWS[-1]]
    wlen = jnp.full((1, POOL_WIDTH), float(POOL_WINDOWS[-1]), F32)
    for gi in range(len(POOL_WINDOWS) - 2, -1, -1):
        win = jnp.where(grp == gi, sums[POOL_WINDOWS[gi]], win)
        wlen = jnp.where(grp == gi, float(POOL_WINDOWS[gi]), wlen)
    pos = (si * tc + lax.broadcasted_iota(I32, (tc, 1), 0) + 1).astype(F32)
    pooled = win / jnp.minimum(pos, wlen) - u
    y = jnp.dot(pooled.astype(BF16), w_ref[...], preferred_element_type=F32) * sc_ref[...]
    o_ref[0] = y.astype(BF16)
    ext_ref[0:POOL_HALO, :] = u[tc - POOL_HALO:tc, :]


def _seqmix_body(ff_ref, bf_ref, tri_ref, hsel_ref, q_ref, k_ref,
                 gq_ref, gf_ref, gi_ref, gg_ref, lb_ref, hgain_ref, t64_ref, hm_ref,
                 u_ref, pw_ref, psc_ref,
                 ka_ref, stats_ref, ohg_ref, opool_ref,
                 carry_ref, st_ref, o_acc, ext_ref, *, ts):
    @pl.when(pl.program_id(1) == 0)
    def _():
        carry_ref[...] = jnp.zeros_like(carry_ref)
        st_ref[...] = jnp.zeros_like(st_ref)
        ext_ref[0:POOL_HALO, :] = jnp.zeros((POOL_HALO, POOL_WIDTH), F32)

    _hgrn_body(gq_ref, gf_ref, gi_ref, gg_ref, lb_ref, hgain_ref, t64_ref, hm_ref, ohg_ref, st_ref, o_acc, tc=ts)
    _foxprep_body(ff_ref, bf_ref, tri_ref, hsel_ref, q_ref, k_ref, ka_ref, stats_ref, carry_ref, ts=ts)
    _pool_body(u_ref, pw_ref, psc_ref, opool_ref, ext_ref, tc=ts)


def _seqmix(a3, b3, bf_pad, tri, lb, hgain, tri64, head_mean, pool_bd, pool_scale, ts):
    B, S, _ = a3.shape
    hw = FOX_HEADS * HEAD_BLK
    head_sel = (jnp.arange(FOX_WIDTH)[:, None] // FOX_HEAD_DIM == jnp.arange(LANES)[None, :]).astype(BF16)
    const = lambda *shp: pl.BlockSpec(shp, lambda b, s: tuple(0 for _ in shp))
    cols = lambda w, blk128: pl.BlockSpec((1, ts, w), lambda b, s: (b, s, blk128 * LANES // w))
    halves = HG_HEADS // 2
    return pl.pallas_call(
        functools.partial(_seqmix_body, ts=ts),
        grid=(B, S // ts),
        in_specs=[
            cols(LANES, 2), const(1, LANES), const(ts, ts), const(FOX_WIDTH, LANES),
            cols(FOX_WIDTH, QBLK0), cols(FOX_WIDTH, KBLK0),
            cols(HG_WIDTH, GQ0), cols(HG_WIDTH, 0), cols(HG_WIDTH, GI0), cols(HG_WIDTH, GG0),
            const(1, HG_WIDTH), const(1, HG_WIDTH), const(HG_CHUNK, HG_CHUNK), const(LANES, LANES),
            cols(POOL_WIDTH, 2 * PU_BLK256), const(POOL_WIDTH, POOL_WIDTH), const(1, POOL_WIDTH),
        ],
        out_specs=[pl.BlockSpec((1, ts, hw), lambda b, s: (b, s, 0)),
                   pl.BlockSpec((1, 1, 8, LANES), lambda b, s: (b, s, 0, 0)),
                   pl.BlockSpec((1, ts, HG_WIDTH), lambda b, s: (b, s, 0)),
                   pl.BlockSpec((1, ts, POOL_WIDTH), lambda b, s: (b, s, 0))],
        out_shape=[jax.ShapeDtypeStruct((B, S, hw), BF16),
                   jax.ShapeDtypeStruct((B, S // ts, 8, LANES), F32),
                   jax.ShapeDtypeStruct((B, S, HG_WIDTH), BF16),
                   jax.ShapeDtypeStruct((B, S, POOL_WIDTH), BF16)],
        scratch_shapes=[pltpu.VMEM((1, LANES), F32), pltpu.VMEM((halves, LANES, LANES), F32),
                        pltpu.VMEM((halves, ts, LANES), F32), pltpu.VMEM((POOL_HALO + ts, POOL_WIDTH), F32)],
        compiler_params=_cparams(("parallel", "arbitrary")),
        name="seq_mixers",
    )(b3, bf_pad, tri, head_sel, a3, a3, a3, b3, a3, a3, lb, hgain, tri64, head_mean, a3, pool_bd, pool_scale)


def _router_body(of_ref, oh_ref, op_ref, h_ref, wo_ref, g_ref, wr_ref, br_ref, tri_ref,
                 hn_ref, ri_ref, rg_ref, cnt_ref, carry_ref, *, tm):
    @pl.when(pl.program_id(0) == 0)
    def _():
        carry_ref[...] = jnp.zeros_like(carry_ref)

    f0, f1 = FOX_WIDTH, FOX_WIDTH + HG_WIDTH
    mix = (jnp.dot(of_ref[...], wo_ref[0:f0, :], preferred_element_type=F32)
           + jnp.dot(oh_ref[...], wo_ref[f0:f1, :], preferred_element_type=F32)
           + jnp.dot(op_ref[...], wo_ref[f1:f1 + POOL_WIDTH, :], preferred_element_type=F32))
    hn = h_ref[...] + mix
    hn_ref[...] = hn
    ms = jnp.mean(hn * hn, axis=-1, keepdims=True)
    xn = hn * lax.rsqrt(ms + RMS_EPS) * g_ref[...]
    x1 = xn.astype(BF16)
    x2 = (xn - x1.astype(F32)).astype(BF16)
    r1 = jnp.dot(x1, wr_ref[...], preferred_element_type=F32)
    r2 = jnp.dot(x2, wr_ref[...], preferred_element_type=F32)
    logits = r1[:, :LANES] + r1[:, LANES:] + r2[:, :LANES] + br_ref[...]

    lanef = lax.broadcasted_iota(I32, (tm, LANES), 1).astype(F32)
    l = logits
    vals, idxs, sels = [], [], []
    for _ in range(TOP_K):
        mk = jnp.max(l, axis=-1, keepdims=True)
        ik = jnp.min(jnp.where(l == mk, lanef, float(LANES)), axis=-1, keepdims=True)
        sk = lanef == ik
        vals.append(mk)
        idxs.append(ik)
        sels.append(sk)
        l = jnp.where(sk, -jnp.inf, l)
    es = [jnp.exp(v - vals[0]) for v in vals]
    den = es[0] + es[1] + es[2] + es[3]
    sel = jnp.zeros((tm, LANES), F32)
    for sk in sels:
        sel = jnp.where(sk, 1.0, sel)
    rank = jnp.dot(tri_ref[...], sel.astype(BF16), preferred_element_type=F32) + carry_ref[...]
    carry = carry_ref[...] + jnp.sum(sel, axis=0, keepdims=True)
    carry_ref[...] = carry
    cnt_ref[...] = carry
    ri = jnp.zeros((tm, LANES), F32)
    rg = jnp.zeros((tm, LANES), F32)
    for k in range(TOP_K):
        pos_k = jnp.sum(jnp.where(sels[k], rank, 0.0), axis=-1, keepdims=True)
        ri = jnp.where(lanef == float(k), idxs[k], ri)
        ri = jnp.where(lanef == float(TOP_K + k), pos_k, ri)
        rg = jnp.where(lanef == float(k), es[k] / den, rg)
    ri_ref[...] = ri.astype(I32)
    rg_ref[...] = rg


def _router(o_fox, o_hg, o_pool, h, wo, g, wr, br, tri, tm):
    T, D = h.shape
    rowblk = lambda w: pl.BlockSpec((tm, w), lambda i: (i, 0))
    const = lambda shp: pl.BlockSpec(shp, lambda i: tuple(0 for _ in shp))
    return pl.pallas_call(
        functools.partial(_router_body, tm=tm),
        grid=(T // tm,),
        in_specs=[rowblk(FOX_WIDTH), rowblk(HG_WIDTH), rowblk(POOL_WIDTH), rowblk(D),
                  const((D, D)), const((1, D)), const((D, 2 * LANES)), const((1, LANES)), const((tm, tm))],
        out_specs=[rowblk(D), rowblk(LANES), rowblk(LANES), const((1, LANES))],
        out_shape=[jax.ShapeDtypeStruct((T, D), F32), jax.ShapeDtypeStruct((T, LANES), I32),
                   jax.ShapeDtypeStruct((T, LANES), F32), jax.ShapeDtypeStruct((1, LANES), F32)],
        scratch_shapes=[pltpu.VMEM((1, LANES), F32)],
        compiler_params=_cparams(("arbitrary",)),
        name="outproj_router",
    )(o_fox, o_hg, o_pool, h, wo, g, wr, br, tri)


def _to_row_tiles(ref, x):
    ref[...] = pltpu.einshape("r(cl)->rcl", x, c=ref.shape[-2])


def _from_row_tiles(ref):
    return pltpu.einshape("rcl->r(cl)", ref[...])


def _row_copy(src_ref, src_row, dst_ref, dst_row, sem):
    return pltpu.make_async_copy(src_ref.at[pl.ds(src_row, 1)], dst_ref.at[pl.ds(dst_row, 1)], sem)


def _wait_rows(tile_ref, sem, times):
    for _ in range(times):
        pltpu.make_async_copy(tile_ref, tile_ref, sem).wait()


def _dispatch_body(pad0_ref, padn_ref, dst_ref, h_ref, g_ref, buf_ref, xn_ref, zero_ref, sem, zsem, *, tt, bm):
    i = pl.program_id(0)
    n = pl.num_programs(0)
    slot = i % 2
    stage = xn_ref.at[slot]

    @pl.when(i >= 2)
    def _():
        _wait_rows(stage, sem.at[slot], TOP_K)

    x = h_ref[...]
    ms = jnp.mean(x * x, axis=-1, keepdims=True)
    _to_row_tiles(stage, x * lax.rsqrt(ms + RMS_EPS) * g_ref[...])

    def issue(r, _):
        for k in range(TOP_K):
            _row_copy(stage, r, buf_ref, dst_ref[0, 0, r * TOP_K + k], sem.at[slot]).start(priority=k % 2)
        return 0

    lax.fori_loop(0, tt, issue, 0, unroll=DMA_UNROLL)

    @pl.when(i == 0)
    def _():
        zero_ref[...] = jnp.zeros_like(zero_ref)
        half = bm // 2
        n_exp = pad0_ref.shape[0] - 1

        def run(cp, pred, wait):
            @pl.when(pred)
            def _():
                cp.wait() if wait else cp.start()

        def pad_copies(e, wait):
            off, cnt = pad0_ref[e], padn_ref[e]
            for r in range(SUBLANES - 1):
                run(_row_copy(zero_ref, 0, buf_ref, off + r, zsem), r < (cnt & (SUBLANES - 1)), wait)
            off = off + (cnt & (SUBLANES - 1))
            sz = SUBLANES
            while sz <= half:
                take = (cnt & sz) != 0
                dst = buf_ref.at[pl.ds(pl.multiple_of(off, SUBLANES), sz)]
                run(pltpu.make_async_copy(zero_ref.at[pl.ds(0, sz)], dst, zsem), take, wait)
                off = off + jnp.where(take, sz, 0)
                sz *= 2
            return 0

        def tail_copy(c, wait):
            dst = buf_ref.at[pl.ds(pl.multiple_of(pad0_ref[n_exp] + c * half, SUBLANES), half)]
            cp = pltpu.make_async_copy(zero_ref, dst, zsem)
            cp.wait() if wait else cp.start()
            return 0

        n_tail = padn_ref[n_exp] // half
        for wait in (False, True):
            lax.fori_loop(0, n_exp, lambda e, c, w=wait: pad_copies(e, w), 0)
            lax.fori_loop(0, n_tail, lambda c, _, w=wait: tail_copy(c, w), 0)

    @pl.when(i == n - 1)
    def _():
        _wait_rows(stage, sem.at[slot], TOP_K)

    @pl.when((i == n - 1) & (n >= 2))
    def _():
        _wait_rows(xn_ref.at[1 - slot], sem.at[1 - slot], TOP_K)


def _dispatch(pad0, padn, dst3, hn, g, R, tt, bm):
    T, D = hn.shape
    grid_spec = pltpu.PrefetchScalarGridSpec(
        num_scalar_prefetch=2,
        grid=(T // tt,),
        in_specs=[
            pl.BlockSpec((1, 1, tt * TOP_K), lambda i, p0, pn: (i, 0, 0), memory_space=pltpu.SMEM),
            pl.BlockSpec((tt, D), lambda i, p0, pn: (i, 0)),
            pl.BlockSpec((1, D), lambda i, p0, pn: (0, 0)),
        ],
        out_specs=pl.BlockSpec(memory_space=pl.ANY),
        scratch_shapes=[pltpu.VMEM((2, tt, D // LANES, LANES), F32), pltpu.VMEM((bm // 2, D // LANES, LANES), F32),
                        pltpu.SemaphoreType.DMA((2,)), pltpu.SemaphoreType.DMA],
    )
    return pl.pallas_call(
        functools.partial(_dispatch_body, tt=tt, bm=bm),
        grid_spec=grid_spec,
        out_shape=jax.ShapeDtypeStruct((R, D // LANES, LANES), F32),
        compiler_params=_cparams(("arbitrary",)),
        name="moe_dispatch",
    )(pad0, padn, dst3, hn, g)


def _expert_body(te_ref, nu_ref, x_ref, w1_ref, w2_ref, perm_ref, b1g_ref, b1l_ref, b2_ref, y_ref,
                 w1g_scr, w1l_scr, w2_scr):
    i = pl.program_id(0)
    live = i < nu_ref[0]
    prev = te_ref[jnp.maximum(i - 1, 0)]

    @pl.when(live & ((i == 0) | (te_ref[i] != prev)))
    def _():
        half = MXU_DIM // 2
        for grp in range(w1_ref.shape[-1] // MXU_DIM):
            w = w1_ref[0, :, grp * MXU_DIM:(grp + 1) * MXU_DIM].astype(BF16)
            r = jnp.dot(w, perm_ref[...], preferred_element_type=F32).astype(BF16)
            w1g_scr[:, grp * half:(grp + 1) * half] = r[:, :half]
            w1l_scr[:, grp * half:(grp + 1) * half] = r[:, half:]
        w2_scr[...] = w2_ref[0].astype(BF16)

    @pl.when(live)
    def _():
        x = _from_row_tiles(x_ref).astype(BF16)
        glu = jnp.dot(x, w1g_scr[...], preferred_element_type=F32) + b1g_ref[0]
        lin = jnp.dot(x, w1l_scr[...], preferred_element_type=F32) + b1l_ref[0]
        glu = jnp.minimum(glu, SWIGLU_LIMIT)
        lin = jnp.clip(lin, -SWIGLU_LIMIT, SWIGLU_LIMIT)
        act = glu / (1.0 + jnp.exp(-SWIGLU_ALPHA * glu)) * (lin + 1.0)
        _to_row_tiles(y_ref, jnp.dot(act.astype(BF16), w2_scr[...], preferred_element_type=F32) + b2_ref[0])

    @pl.when(jnp.logical_not(live))
    def _():
        y_ref[...] = jnp.zeros_like(y_ref)


def _experts(tile_e, n_used, buf, w1, w2, b1g, b1l, b2, bm, first_expert):
    R, dsub, _ = buf.shape
    D = dsub * LANES
    F2 = w1.shape[-1]
    F = F2 // 2
    half = MXU_DIM // 2
    pi = jnp.arange(MXU_DIM)[:, None]
    pj = jnp.arange(MXU_DIM)[None, :]
    perm = jnp.where(pj < half, pi == 2 * pj, pi == 2 * (pj - half) + 1).astype(BF16)
    rows = pl.BlockSpec((bm, dsub, LANES), lambda i, te, nu: (jnp.minimum(i, nu[0] - 1), 0, 0))
    wspec = lambda a, b, e0=0: pl.BlockSpec((1, a, b), lambda i, te, nu: (e0 + te[i], 0, 0))
    grid_spec = pltpu.PrefetchScalarGridSpec(
        num_scalar_prefetch=2,
        grid=(R // bm,),
        in_specs=[rows, wspec(D, F2, first_expert), wspec(F, D, first_expert),
                  pl.BlockSpec((MXU_DIM, MXU_DIM), lambda i, te, nu: (0, 0)),
                  wspec(1, F), wspec(1, F), wspec(1, D)],
        out_specs=pl.BlockSpec((bm, dsub, LANES), lambda i, te, nu: (i, 0, 0)),
        scratch_shapes=[pltpu.VMEM((D, F), BF16), pltpu.VMEM((D, F), BF16), pltpu.VMEM((F, D), BF16)],
    )
    return pl.pallas_call(
        _expert_body,
        grid_spec=grid_spec,
        out_shape=jax.ShapeDtypeStruct((R, dsub, LANES), F32),
        compiler_params=_cparams(("arbitrary",), vmem=EXPERT_VMEM_LIMIT),
        name="moe_experts",
    )(tile_e, n_used, buf, w1, w2, perm, b1g, b1l, b2)


def _combine_body(dst_ref, h_ref, rg_ref, gf_ref, y_ref, o_ref, rows_ref, sem, *, tt, final_norm):
    s = pl.program_id(0)
    n_tiles = pl.num_programs(0) - 1

    @pl.when(s < n_tiles)
    def _():
        slot = s % 2

        def issue(r, _):
            for k in range(TOP_K):
                _row_copy(y_ref, dst_ref[0, 0, r * TOP_K + k], rows_ref.at[slot, k], r,
                          sem.at[slot]).start(priority=k % 2)
            return 0

        lax.fori_loop(0, tt, issue, 0, unroll=DMA_UNROLL)

    @pl.when(s >= 1)
    def _():
        slot = (s - 1) % 2
        _wait_rows(rows_ref.at[slot, 0], sem.at[slot], TOP_K)
        out = h_ref[...]
        gates = rg_ref[...]
        for k in range(TOP_K):
            out = out + gates[:, k:k + 1] * _from_row_tiles(rows_ref.at[slot, k])
        if final_norm:
            ms = jnp.mean(out * out, axis=-1, keepdims=True)
            out = out * lax.rsqrt(ms + RMS_EPS) * gf_ref[...]
        o_ref[...] = out


def _combine(dst3, hn, rg, g_final, ybuf, tt, final_norm):
    T, D = hn.shape
    n_tiles = T // tt
    prev = lambda s: (jnp.maximum(s - 1, 0), 0)
    return pl.pallas_call(
        functools.partial(_combine_body, tt=tt, final_norm=final_norm),
        grid=(n_tiles + 1,),
        in_specs=[
            pl.BlockSpec((1, 1, tt * TOP_K), lambda s: (jnp.minimum(s, n_tiles - 1), 0, 0), memory_space=pltpu.SMEM),
            pl.BlockSpec((tt, D), prev),
            pl.BlockSpec((tt, LANES), prev),
            pl.BlockSpec((1, D), lambda s: (0, 0)),
            pl.BlockSpec(memory_space=pl.ANY),
        ],
        out_specs=pl.BlockSpec((tt, D), prev),
        out_shape=jax.ShapeDtypeStruct((T, D), F32),
        scratch_shapes=[pltpu.VMEM((2, TOP_K, tt, D // LANES, LANES), F32), pltpu.SemaphoreType.DMA((2,))],
        compiler_params=_cparams(("arbitrary",)),
        name="moe_combine",
    )(dst3, hn, rg, g_final, ybuf)


def _tiles(B, S):
    T = B * S
    ts = min(512, S)
    tm = min(512, T)
    tt = min(256, T)
    bm = min(512, T)
    return ts, tm, tt, bm


def _block_tri(n, blk, strict=False):
    r = jnp.arange(n)[:, None]
    c = jnp.arange(n)[None, :]
    keep = (c < r) if strict else (c <= r)
    return (keep & (r // blk == c // blk)).astype(BF16)


def kernel(x, norm_mix, w_in, b_fox_f, fox_gain, hgrn_lb, hgrn_gain, pool_w, pool_scale, w_out,
           norm_moe, w_router, b_router, w1, b1, w2, b2, norm_final):
    B, S, D = x.shape
    L = w_in.shape[0]
    T = B * S
    E = w1.shape[1]
    ts, tm, tt, bm = _tiles(B, S)

    offs = [0]
    for wdt in (FOX_WIDTH, FOX_WIDTH, FOX_WIDTH, FOX_HEADS, HG_WIDTH, HG_WIDTH, HG_WIDTH, HG_WIDTH, POOL_WIDTH):
        offs.append(offs[-1] + wdt)
    col = lambda i: w_in[:, :, offs[i]:offs[i + 1]]
    scale = FOX_HEAD_DIM ** -0.5 * LOG2E
    wa = jnp.concatenate([col(0) * scale, col(1), col(2), col(4), col(6), col(7), col(8)], axis=-1).astype(BF16)
    wb = jnp.concatenate([col(5), col(3), jnp.zeros((L, D, LANES - FOX_HEADS), F32)], axis=-1).astype(BF16)
    bf_pad = jnp.pad(b_fox_f, ((0, 0), (0, LANES - FOX_HEADS)))
    p = jax.nn.softmax(hgrn_lb.astype(F32), axis=0)
    cs = jnp.cumsum(p, axis=0)
    lbs = cs - cs[0:1]
    eye = jnp.eye(len(POOL_WINDOWS), dtype=F32)
    pool_bd = jnp.einsum("lgcd,gh->lgchd", pool_w, eye).reshape(L, POOL_WIDTH, POOL_WIDTH).astype(BF16)
    wr_pad = jnp.pad(w_router, ((0, 0), (0, 0), (0, LANES - E)))
    wr_hi = wr_pad.astype(BF16)
    wr_lo = (wr_pad - wr_hi.astype(F32)).astype(BF16)
    wr = jnp.concatenate([wr_hi, wr_lo], axis=-1)
    br_pad = jnp.pad(b_router, ((0, 0), (0, LANES - E)), constant_values=NEG_BIG)
    w1_all = w1.reshape((L * E,) + w1.shape[2:])
    w2_all = w2.reshape((L * E,) + w2.shape[2:])
    b1g = b1[..., 0::2][:, :, None, :]
    b1l = b1[..., 1::2][:, :, None, :]
    b2r = b2[:, :, None, :]

    tri_seq = _block_tri(ts, ts)
    tri64 = _block_tri(HG_CHUNK, HG_CHUNK)
    li = jnp.arange(LANES)
    head_mean = ((li[:, None] // HG_DK == li[None, :] // HG_DK) / HG_DK).astype(BF16)
    tri_tok = _block_tri(tm, tm, strict=True)

    M = T * TOP_K
    R = -(-M // bm) * bm + E * bm
    NB = R // bm

    h = x.reshape(T, D)
    for l in range(L):
        a2, b2d = _inproj(h, norm_mix[l][None], wa[l], wb[l], tm)
        a3 = a2.reshape(B, S, A_COLS)
        b3 = b2d.reshape(B, S, B_COLS)
        ka, stats, o_hg, o_pool = _seqmix(a3, b3, bf_pad[l][None], tri_seq, lbs[l][None], hgrn_gain[l][None],
                                          tri64, head_mean, pool_bd[l], pool_scale[l][None], ts)
        heads = _fox_pairing(b_fox_f[l])
        by_head = lambda w: w.reshape((FOX_HEADS, FOX_HEAD_DIM) + w.shape[1:])[heads].reshape(w.shape)
        o_fox = _fox(_fox_first_tiles(stats), heads, a3, ka, by_head(fox_gain[l])[None], ts, min(32, ts))
        wo_l = jnp.concatenate([by_head(w_out[l, :FOX_WIDTH]), w_out[l, FOX_WIDTH:]], axis=0).astype(BF16)
        hn, ri, rg, cnt = _router(o_fox.reshape(T, FOX_WIDTH), o_hg.reshape(T, HG_WIDTH),
                                  o_pool.reshape(T, POOL_WIDTH), h, wo_l, norm_moe[l][None],
                                  wr[l], br_pad[l][None], tri_tok, tm)
        counts = cnt[0, :E].astype(I32)
        padded = (counts + bm - 1) // bm * bm
        pend = jnp.cumsum(padded)
        pstart = pend - padded
        first_row = jnp.sum(jnp.where(ri[:, 0:TOP_K, None] == jnp.arange(E, dtype=I32), pstart.astype(I32), 0), axis=-1)
        dst = first_row + ri[:, TOP_K:2 * TOP_K]
        n_used = (pend[-1] // bm).astype(I32)
        tile_ids = jnp.minimum(jnp.arange(NB, dtype=I32), n_used - 1)
        tile_e = jnp.minimum(jnp.sum(pend[None, :] <= (tile_ids * bm)[:, None], axis=1), E - 1).astype(I32)
        dst3 = dst.reshape(T // tt, 1, tt * TOP_K)
        pad0 = jnp.concatenate([pstart + counts, pend[-1:]]).astype(I32)
        padn = jnp.concatenate([padded - counts, R - pend[-1:]]).astype(I32)
        buf = _dispatch(pad0, padn, dst3, hn, norm_moe[l][None], R, tt, bm)
        ybuf = _experts(tile_e, n_used.reshape(1), buf, w1_all, w2_all, b1g[l], b1l[l], b2r[l], bm, l * E)
        h = _combine(dst3, hn, rg, norm_final[None], ybuf, tt, final_norm=(l == L - 1))
    return h.reshape(B, S, D)
```

```python
import functools

import jax
import jax.numpy as jnp
from jax import lax
from jax.experimental import pallas as pl
from jax.experimental.pallas import tpu as pltpu

F32 = jnp.float32
BF16 = jnp.bfloat16
I32 = jnp.int32

LANES = 128
SUBLANES = 8
VMEM_LIMIT = 56 * 1024 * 1024
EXPERT_VMEM_LIMIT = 60 * 1024 * 1024

FOX_HEADS = 8
FOX_HEAD_DIM = 64
FOX_WIDTH = FOX_HEADS * FOX_HEAD_DIM
HG_HEADS = 4
HG_DK = 64
HG_WIDTH = HG_HEADS * HG_DK
HG_CHUNK = 64
HG_SUB = 16
POOL_WINDOWS = (2, 4, 8, 16)
POOL_GROUP_DIM = 64
POOL_WIDTH = len(POOL_WINDOWS) * POOL_GROUP_DIM
POOL_HALO = 16
TOP_K = 4
SWIGLU_LIMIT = 7.0
SWIGLU_ALPHA = 1.702
RMS_EPS = 1e-5
NEG_BIG = -1e30
DMA_UNROLL = 8

HEAD_BLK = LANES
QBLK0, KBLK0, VBLK0 = 0, FOX_HEADS // 2, FOX_HEADS
GQ0 = 3 * FOX_HEADS // 2
GI0, GG0 = GQ0 + 2, GQ0 + 4
PU_BLK256 = (GQ0 + 6) // 2
A_COLS = (GQ0 + 8) * LANES
B_COLS = 384
BIAS_TERMS = 3
LOG2E = 1.4426950408889634
SKIP_LOG2 = 150.0
NORM_SLACK = 1.01
MXU_DIM = 256


def _head_lanes(parity):
    lane = lax.broadcasted_iota(I32, (1, LANES), 1)
    lower = lane < FOX_HEAD_DIM
    return (lower, FOX_HEAD_DIM) if parity == 0 else (jnp.logical_not(lower), 0)


def _cparams(sem, vmem=VMEM_LIMIT):
    return pltpu.CompilerParams(dimension_semantics=sem, vmem_limit_bytes=vmem)


def _split3(x):
    x1 = x.astype(BF16)
    r1 = x - x1.astype(F32)
    x2 = r1.astype(BF16)
    x3 = (r1 - x2.astype(F32)).astype(BF16)
    return x1, x2, x3


def _tri_dot(tri, x):
    x1, x2, x3 = _split3(x)
    d = lambda a: jnp.dot(tri, a, preferred_element_type=F32)
    return d(x1) + d(x2) + d(x3)


def _log_sigmoid(z):
    return jnp.minimum(z, 0.0) - jnp.log1p(jnp.exp(-jnp.abs(z)))


def _inproj_body(x_ref, g_ref, wa_ref, wb_ref, a_ref, b_ref):
    x = x_ref[...]
    ms = jnp.mean(x * x, axis=-1, keepdims=True)
    xn = (x * lax.rsqrt(ms + RMS_EPS) * g_ref[...]).astype(BF16)
    a_ref[...] = jnp.dot(xn, wa_ref[...], preferred_element_type=F32).astype(BF16)
    b_ref[...] = jnp.dot(xn, wb_ref[...], preferred_element_type=F32)


def _inproj(h, g, wa, wb, tm):
    T, D = h.shape
    return pl.pallas_call(
        _inproj_body,
        grid=(T // tm,),
        in_specs=[
            pl.BlockSpec((tm, D), lambda i: (i, 0)),
            pl.BlockSpec((1, D), lambda i: (0, 0)),
            pl.BlockSpec((D, A_COLS), lambda i: (0, 0)),
            pl.BlockSpec((D, B_COLS), lambda i: (0, 0)),
        ],
        out_specs=[
            pl.BlockSpec((tm, A_COLS), lambda i: (i, 0)),
            pl.BlockSpec((tm, B_COLS), lambda i: (i, 0)),
        ],
        out_shape=[jax.ShapeDtypeStruct((T, A_COLS), BF16), jax.ShapeDtypeStruct((T, B_COLS), F32)],
        compiler_params=_cparams(("parallel",)),
        name="inproj",
    )(h, g, wa, wb)


def _foxprep_body(ff_ref, bf_ref, tri_ref, hsel_ref, q_ref, k_ref, ka_ref, st_ref, carry_ref, *, ts):
    logf = _log_sigmoid(ff_ref[0] + bf_ref[...])
    c = _tri_dot(tri_ref[...], logf) + carry_ref[...]
    carry_ref[...] = c[ts - 1:ts, :]
    cl = c * LOG2E
    lane = lax.broadcasted_iota(I32, (1, LANES), 1)
    for h in range(FOX_HEADS):
        feat, spare0 = _head_lanes(h % 2)
        terms = _split3(jnp.broadcast_to(-cl[:, h:h + 1], (ts, LANES)))
        pair = k_ref[0, :, (h // 2) * HEAD_BLK:(h // 2 + 1) * HEAD_BLK]
        ka = pair * feat.astype(BF16)
        for i, term in enumerate(terms):
            ka = jnp.where(lane == spare0 + i, term, ka)
        ka_ref[0, :, h * HEAD_BLK:(h + 1) * HEAD_BLK] = ka

    def max_norm(ref):
        x = ref[0].astype(F32)
        n2 = jnp.dot((x * x).astype(BF16), hsel_ref[...], preferred_element_type=F32)
        return jnp.sqrt(jnp.max(n2, axis=0, keepdims=True)) * NORM_SLACK

    qn_row = max_norm(q_ref)
    kn_row = max_norm(k_ref)
    rowi = lax.broadcasted_iota(I32, (8, LANES), 0)
    st = jnp.where(rowi == 0, qn_row, jnp.where(rowi == 1, kn_row, jnp.where(rowi == 2, cl[0:1, :],
                   jnp.where(rowi == 3, cl[ts - 1:ts, :], 0.0))))
    st_ref[0, 0] = st


def _fox_body(js_ref, hd_ref, qa_ref, qb_ref, ka_ref, kb_ref, va_ref, vb_ref, g_ref, o_ref,
              s_ref, p_ref, m_ref, acc_ref, vp_ref, *, t, rc):
    q_refs, k_refs, v_refs = (qa_ref, qb_ref), (ka_ref, kb_ref), (va_ref, vb_ref)
    b = pl.program_id(0)
    hp = pl.program_id(1)
    qi = pl.program_id(2)
    nt = pl.num_programs(2)
    lane = lax.broadcasted_iota(I32, (1, LANES), 1)
    lo = lane < FOX_HEAD_DIM
    feat, sum_l, keep, one_at, qs = [], [], [], [], []
    for hh in range(2):
        f, spare0 = _head_lanes(hh)
        feat.append(f)
        sum_l.append(lane == spare0)
        keep.append(f.astype(BF16))
        one_at.append((lane == spare0).astype(BF16))
        ones_q = ((lane >= spare0) & (lane < spare0 + BIAS_TERMS)).astype(BF16)
        qs.append(q_refs[hh][0] * keep[hh] + ones_q)

    @pl.when(qi == 0)
    def _():
        for hh in range(2):
            vp_ref[hh] = v_refs[hh][0] * keep[hh] + one_at[hh]

    m_ref[...] = jnp.full(m_ref.shape, -jnp.inf, F32)
    acc_ref[...] = jnp.zeros(acc_ref.shape, F32)
    first = [js_ref[(b * FOX_HEADS + hd_ref[2 * hp + hh]) * nt + qi] for hh in range(2)]
    j0 = jnp.minimum(first[0], first[1])
    nt_dims = (((1,), (1,)), ((), ()))
    nblk = t // LANES

    def tile(j, masked):
        start = pl.multiple_of(j * t, t)
        for hh in range(2):
            kt = k_refs[hh][0, pl.ds(start, t), :]
            s_ref[hh] = lax.dot_general(qs[hh], kt, nt_dims, preferred_element_type=F32)
        for hh in range(2):
            for r in range(t // rc):
                rows = slice(r * rc, (r + 1) * rc)
                diag_blk = (r * rc) // LANES
                mx = None
                for cb in range(diag_blk + 1 if masked else nblk):
                    cols = slice(cb * LANES, (cb + 1) * LANES)
                    sc = s_ref[hh, rows, cols]
                    if masked and cb == diag_blk:
                        ri = lax.broadcasted_iota(I32, (rc, LANES), 0) + r * rc
                        ci = lax.broadcasted_iota(I32, (rc, LANES), 1) + cb * LANES
                        sc = jnp.where(ci <= ri, sc, -jnp.inf)
                        s_ref[hh, rows, cols] = sc
                    mx = sc if mx is None else jnp.maximum(mx, sc)
                m_old = m_ref[hh, rows, :]
                m_new = jnp.maximum(m_old, jnp.broadcast_to(jnp.max(mx, axis=-1, keepdims=True), (rc, LANES)))
                m_ref[hh, rows, :] = m_new
                acc_ref[hh, rows, :] = jnp.exp2(m_old - m_new) * acc_ref[hh, rows, :]
            for r in range(t // rc):
                rows = slice(r * rc, (r + 1) * rc)
                diag_blk = (r * rc) // LANES
                m_new = m_ref[hh, rows, :]
                for cb in range(nblk):
                    cols = slice(cb * LANES, (cb + 1) * LANES)
                    if masked and cb > diag_blk:
                        p_ref[hh, rows, cols] = jnp.zeros((rc, LANES), BF16)
                    else:
                        p_ref[hh, rows, cols] = jnp.exp2(s_ref[hh, rows, cols] - m_new).astype(BF16)
            acc_ref[hh] += jnp.dot(p_ref[hh], vp_ref[hh, pl.ds(start, t), :], preferred_element_type=F32)

    def body(j, carry):
        tile(j, False)
        return carry

    lax.fori_loop(j0, qi, body, 0)
    tile(qi, True)
    outs = []
    for hh in range(2):
        acc = acc_ref[hh]
        w = jnp.where(feat[hh], 1.0 / FOX_HEAD_DIM, jnp.where(sum_l[hh], RMS_EPS, 0.0))
        outs.append(acc * lax.rsqrt(jnp.sum(acc * acc * w, axis=-1, keepdims=True)))
    o_ref[0] = (jnp.where(lo, outs[0], outs[1]) * g_ref[...]).astype(BF16)


def _fox_first_tiles(stats):
    qn, kn, cs, ce = (stats[:, :, i, :FOX_HEADS].transpose(0, 2, 1) for i in range(4))
    bound = cs[..., :, None] - ce[..., None, :] + qn[..., :, None] * (kn[..., None, :] + kn[..., :, None])
    nt = qn.shape[-1]
    ti = jnp.arange(nt)[:, None]
    tj = jnp.arange(nt)[None, :]
    need = (tj == ti) | ((tj < ti) & (bound >= -SKIP_LOG2))
    return jnp.argmax(need, axis=-1).astype(I32).reshape(-1)


def _fox_pairing(b_f):
    even = jnp.argsort(b_f[0::2]) * 2
    odd = jnp.argsort(b_f[1::2]) * 2 + 1
    return jnp.stack([even, odd], axis=1).reshape(-1).astype(I32)


def _fox(first, heads, a3, ka, gain, t, rc):
    B, S, _ = a3.shape
    qblk = lambda side: pl.BlockSpec(
        (1, t, HEAD_BLK), lambda b, hp, qi, js, hd: (b, qi, QBLK0 + hd[2 * hp + side] // 2))
    kblk = lambda side: pl.BlockSpec(
        (1, S, HEAD_BLK), lambda b, hp, qi, js, hd: (b, 0, hd[2 * hp + side]))
    vblk = lambda side: pl.BlockSpec(
        (1, S, HEAD_BLK), lambda b, hp, qi, js, hd: (b, 0, VBLK0 + hd[2 * hp + side] // 2))
    grid_spec = pltpu.PrefetchScalarGridSpec(
        num_scalar_prefetch=2,
        grid=(B, FOX_HEADS // 2, S // t),
        in_specs=[
            qblk(0), qblk(1), kblk(0), kblk(1), vblk(0), vblk(1),
            pl.BlockSpec((1, LANES), lambda b, hp, qi, js, hd: (0, hp)),
        ],
        out_specs=pl.BlockSpec((1, t, LANES), lambda b, hp, qi, js, hd: (b, qi, hp)),
        scratch_shapes=[pltpu.VMEM((2, t, t), F32), pltpu.VMEM((2, t, t), BF16),
                        pltpu.VMEM((2, t, LANES), F32), pltpu.VMEM((2, t, LANES), F32),
                        pltpu.VMEM((2, S, HEAD_BLK), BF16)],
    )
    return pl.pallas_call(
        functools.partial(_fox_body, t=t, rc=rc),
        grid_spec=grid_spec,
        out_shape=jax.ShapeDtypeStruct((B, S, FOX_WIDTH), BF16),
        compiler_params=_cparams(("parallel", "parallel", "arbitrary")),
        name="fox_attention",
    )(first, heads, a3, a3, ka, ka, a3, a3, gain)


def _hgrn_body(q_ref, f_ref, v_ref, gg_ref, lb_ref, gain_ref, t64_ref, hm_ref, o_ref, st_ref, o_acc, *, tc):
    for half in range(HG_HEADS // 2):
        _hgrn_half(slice(half * LANES, (half + 1) * LANES), q_ref, f_ref, v_ref, gg_ref, lb_ref, gain_ref,
                   t64_ref, hm_ref, o_ref, st_ref.at[half], o_acc.at[half], tc)


def _hgrn_half(cols, q_ref, f_ref, v_ref, gg_ref, lb_ref, gain_ref, t64_ref, hm_ref, o_ref, st_ref, o_acc, tc):
    C, SB = HG_CHUNK, HG_SUB
    lane = lax.broadcasted_iota(I32, (1, LANES), 1)
    lo = lane < HG_DK
    lb = lb_ref[:, cols]
    fl = f_ref[0, :, cols]
    t2 = jnp.log1p(-lb) + _log_sigmoid(fl)
    t1 = jnp.where(lb > 0.0, jnp.log(jnp.maximum(lb, 1e-37)), NEG_BIG)
    logf = jnp.maximum(t1, t2) + jnp.log1p(jnp.exp(-jnp.abs(t1 - t2)))
    kk = (1.0 - lb) / (1.0 + jnp.exp(fl))
    q = q_ref[0, :, cols].astype(F32)
    nch = tc // C
    wide = jnp.concatenate([logf[ch * C:(ch + 1) * C] for ch in range(nch)], axis=1)
    a_wide = _tri_dot(t64_ref[...], wide)
    a = jnp.concatenate([a_wide[:, ch * LANES:(ch + 1) * LANES] for ch in range(nch)], axis=0)
    qa = (q * jnp.exp(a)).astype(BF16)

    rowblk = lax.broadcasted_iota(I32, (C, 1), 0) // SB
    r64 = lax.broadcasted_iota(I32, (C, C), 0)
    c64 = lax.broadcasted_iota(I32, (C, C), 1)
    causal = c64 <= r64
    rr = lax.broadcasted_iota(I32, (LANES, LANES), 0)
    cc = lax.broadcasted_iota(I32, (LANES, LANES), 1)
    same_head = (rr < HG_DK) == (cc < HG_DK)
    nt = (((1,), (1,)), ((), ()))
    tn = (((0,), (0,)), ((), ()))

    scores, updates, decays = [], [], []
    for ch in range(tc // C):
        sl = slice(ch * C, (ch + 1) * C)
        a_c, kk_c, v_c = a[sl], kk[sl], v_ref[0, sl, cols]
        a_last = a_c[C - 1:C, :]
        rows0, rows1 = [], []
        for blk in range(C // SB):
            b0 = ch * C + blk * SB
            e_blk = a_c[blk * SB - 1:blk * SB, :] if blk else jnp.zeros((1, LANES), F32)
            live = rowblk <= blk
            k_blk = jnp.where(live, kk_c * jnp.exp(jnp.where(live, e_blk - a_c, 0.0)), 0.0).astype(BF16)
            q_blk = q[b0:b0 + SB, :] * jnp.exp(a_c[blk * SB:(blk + 1) * SB, :] - e_blk)
            lhs = jnp.concatenate([jnp.where(lo, q_blk, 0.0), jnp.where(lo, 0.0, q_blk)], axis=0).astype(BF16)
            sc = lax.dot_general(lhs, k_blk, nt, preferred_element_type=F32)
            rows0.append(sc[0:SB])
            rows1.append(sc[SB:2 * SB])
        scores.append((jnp.where(causal, jnp.concatenate(rows0, axis=0), 0.0).astype(BF16),
                       jnp.where(causal, jnp.concatenate(rows1, axis=0), 0.0).astype(BF16)))
        kd = (kk_c * jnp.exp(a_last - a_c)).astype(BF16)
        updates.append(lax.dot_general(v_c, kd, tn, preferred_element_type=F32))
        decays.append(jnp.exp(a_last))
    intras = []
    for ch in range(tc // C):
        v_c = v_ref[0, ch * C:(ch + 1) * C, cols]
        intras.append(jnp.where(lo, jnp.dot(scores[ch][0], v_c, preferred_element_type=F32),
                                jnp.dot(scores[ch][1], v_c, preferred_element_type=F32)))
    st = st_ref[...]
    for ch in range(tc // C):
        sl = slice(ch * C, (ch + 1) * C)
        inter = lax.dot_general(qa[sl], st.astype(BF16), nt, preferred_element_type=F32)
        o_acc[sl, :] = inter + intras[ch]
        st = jnp.where(same_head, st * decays[ch] + updates[ch], 0.0)
    st_ref[...] = st

    o = o_acc[...]
    sq = o * o
    hi = sq.astype(BF16)
    lo_t = (sq - hi.astype(F32)).astype(BF16)
    ms = (jnp.dot(hi, hm_ref[...], preferred_element_type=F32)
          + jnp.dot(lo_t, hm_ref[...], preferred_element_type=F32))
    g = gg_ref[0, :, cols].astype(F32)
    silu = g / (1.0 + jnp.exp(-g))
    o_ref[0, :, cols] = (o * lax.rsqrt(ms + RMS_EPS) * gain_ref[:, cols] * silu).astype(BF16)


def _pool_body(u_ref, w_ref, sc_ref, o_ref, ext_ref, *, tc):
    si = pl.program_id(1)
    u = u_ref[0].astype(F32)
    ext_ref[POOL_HALO:POOL_HALO + tc, :] = u
    acc = u
    sums = {}
    for j in range(1, max(POOL_WINDOWS)):
        acc = acc + ext_ref[POOL_HALO - j:POOL_HALO - j + tc, :]
        if j + 1 in POOL_WINDOWS:
            sums[j + 1] = acc
    lane = lax.broadcasted_iota(I32, (1, POOL_WIDTH), 1)
    grp = lane // POOL_GROUP_DIM
    win = sums[POOL_WINDOWS[-1]]
    wlen = jnp.full((1, POOL_WIDTH), float(POOL_WINDOWS[-1]), F32)
    for gi in range(len(POOL_WINDOWS) - 2, -1, -1):
        win = jnp.where(grp == gi, sums[POOL_WINDOWS[gi]], win)
        wlen = jnp.where(grp == gi, float(POOL_WINDOWS[gi]), wlen)
    pos = (si * tc + lax.broadcasted_iota(I32, (tc, 1), 0) + 1).astype(F32)
    pooled = win / jnp.minimum(pos, wlen) - u
    y = jnp.dot(pooled.astype(BF16), w_ref[...], preferred_element_type=F32) * sc_ref[...]
    o_ref[0] = y.astype(BF16)
    ext_ref[0:POOL_HALO, :] = u[tc - POOL_HALO:tc, :]


def _seqmix_body(ff_ref, bf_ref, tri_ref, hsel_ref, q_ref, k_ref,
                 gq_ref, gf_ref, gi_ref, gg_ref, lb_ref, hgain_ref, t64_ref, hm_ref,
                 u_ref, pw_ref, psc_ref,
                 ka_ref, stats_ref, ohg_ref, opool_ref,
                 carry_ref, st_ref, o_acc, ext_ref, *, ts):
    @pl.when(pl.program_id(1) == 0)
    def _():
        carry_ref[...] = jnp.zeros_like(carry_ref)
        st_ref[...] = jnp.zeros_like(st_ref)
        ext_ref[0:POOL_HALO, :] = jnp.zeros((POOL_HALO, POOL_WIDTH), F32)

    _hgrn_body(gq_ref, gf_ref, gi_ref, gg_ref, lb_ref, hgain_ref, t64_ref, hm_ref, ohg_ref, st_ref, o_acc, tc=ts)
    _foxprep_body(ff_ref, bf_ref, tri_ref, hsel_ref, q_ref, k_ref, ka_ref, stats_ref, carry_ref, ts=ts)
    _pool_body(u_ref, pw_ref, psc_ref, opool_ref, ext_ref, tc=ts)


def _seqmix(a3, b3, bf_pad, tri, lb, hgain, tri64, head_mean, pool_bd, pool_scale, ts):
    B, S, _ = a3.shape
    hw = FOX_HEADS * HEAD_BLK
    head_sel = (jnp.arange(FOX_WIDTH)[:, None] // FOX_HEAD_DIM == jnp.arange(LANES)[None, :]).astype(BF16)
    const = lambda *shp: pl.BlockSpec(shp, lambda b, s: tuple(0 for _ in shp))
    cols = lambda w, blk128: pl.BlockSpec((1, ts, w), lambda b, s: (b, s, blk128 * LANES // w))
    halves = HG_HEADS // 2
    return pl.pallas_call(
        functools.partial(_seqmix_body, ts=ts),
        grid=(B, S // ts),
        in_specs=[
            cols(LANES, 2), const(1, LANES), const(ts, ts), const(FOX_WIDTH, LANES),
            cols(FOX_WIDTH, QBLK0), cols(FOX_WIDTH, KBLK0),
            cols(HG_WIDTH, GQ0), cols(HG_WIDTH, 0), cols(HG_WIDTH, GI0), cols(HG_WIDTH, GG0),
            const(1, HG_WIDTH), const(1, HG_WIDTH), const(HG_CHUNK, HG_CHUNK), const(LANES, LANES),
            cols(POOL_WIDTH, 2 * PU_BLK256), const(POOL_WIDTH, POOL_WIDTH), const(1, POOL_WIDTH),
        ],
        out_specs=[pl.BlockSpec((1, ts, hw), lambda b, s: (b, s, 0)),
                   pl.BlockSpec((1, 1, 8, LANES), lambda b, s: (b, s, 0, 0)),
                   pl.BlockSpec((1, ts, HG_WIDTH), lambda b, s: (b, s, 0)),
                   pl.BlockSpec((1, ts, POOL_WIDTH), lambda b, s: (b, s, 0))],
        out_shape=[jax.ShapeDtypeStruct((B, S, hw), BF16),
                   jax.ShapeDtypeStruct((B, S // ts, 8, LANES), F32),
                   jax.ShapeDtypeStruct((B, S, HG_WIDTH), BF16),
                   jax.ShapeDtypeStruct((B, S, POOL_WIDTH), BF16)],
        scratch_shapes=[pltpu.VMEM((1, LANES), F32), pltpu.VMEM((halves, LANES, LANES), F32),
                        pltpu.VMEM((halves, ts, LANES), F32), pltpu.VMEM((POOL_HALO + ts, POOL_WIDTH), F32)],
        compiler_params=_cparams(("parallel", "arbitrary")),
        name="seq_mixers",
    )(b3, bf_pad, tri, head_sel, a3, a3, a3, b3, a3, a3, lb, hgain, tri64, head_mean, a3, pool_bd, pool_scale)


def _router_body(of_ref, oh_ref, op_ref, h_ref, wo_ref, g_ref, wr_ref, br_ref, tri_ref,
                 hn_ref, ri_ref, rg_ref, cnt_ref, carry_ref, *, tm):
    @pl.when(pl.program_id(0) == 0)
    def _():
        carry_ref[...] = jnp.zeros_like(carry_ref)

    f0, f1 = FOX_WIDTH, FOX_WIDTH + HG_WIDTH
    mix = (jnp.dot(of_ref[...], wo_ref[0:f0, :], preferred_element_type=F32)
           + jnp.dot(oh_ref[...], wo_ref[f0:f1, :], preferred_element_type=F32)
           + jnp.dot(op_ref[...], wo_ref[f1:f1 + POOL_WIDTH, :], preferred_element_type=F32))
    hn = h_ref[...] + mix
    hn_ref[...] = hn
    ms = jnp.mean(hn * hn, axis=-1, keepdims=True)
    xn = hn * lax.rsqrt(ms + RMS_EPS) * g_ref[...]
    x1 = xn.astype(BF16)
    x2 = (xn - x1.astype(F32)).astype(BF16)
    r1 = jnp.dot(x1, wr_ref[...], preferred_element_type=F32)
    r2 = jnp.dot(x2, wr_ref[...], preferred_element_type=F32)
    logits = r1[:, :LANES] + r1[:, LANES:] + r2[:, :LANES] + br_ref[...]

    lanef = lax.broadcasted_iota(I32, (tm, LANES), 1).astype(F32)
    l = logits
    vals, idxs, sels = [], [], []
    for _ in range(TOP_K):
        mk = jnp.max(l, axis=-1, keepdims=True)
        ik = jnp.min(jnp.where(l == mk, lanef, float(LANES)), axis=-1, keepdims=True)
        sk = lanef == ik
        vals.append(mk)
        idxs.append(ik)
        sels.append(sk)
        l = jnp.where(sk, -jnp.inf, l)
    es = [jnp.exp(v - vals[0]) for v in vals]
    den = es[0] + es[1] + es[2] + es[3]
    sel = jnp.zeros((tm, LANES), F32)
    for sk in sels:
        sel = jnp.where(sk, 1.0, sel)
    rank = jnp.dot(tri_ref[...], sel.astype(BF16), preferred_element_type=F32) + carry_ref[...]
    carry = carry_ref[...] + jnp.sum(sel, axis=0, keepdims=True)
    carry_ref[...] = carry
    cnt_ref[...] = carry
    ri = jnp.zeros((tm, LANES), F32)
    rg = jnp.zeros((tm, LANES), F32)
    for k in range(TOP_K):
        pos_k = jnp.sum(jnp.where(sels[k], rank, 0.0), axis=-1, keepdims=True)
        ri = jnp.where(lanef == float(k), idxs[k], ri)
        ri = jnp.where(lanef == float(TOP_K + k), pos_k, ri)
        rg = jnp.where(lanef == float(k), es[k] / den, rg)
    ri_ref[...] = ri.astype(I32)
    rg_ref[...] = rg


def _router(o_fox, o_hg, o_pool, h, wo, g, wr, br, tri, tm):
    T, D = h.shape
    rowblk = lambda w: pl.BlockSpec((tm, w), lambda i: (i, 0))
    const = lambda shp: pl.BlockSpec(shp, lambda i: tuple(0 for _ in shp))
    return pl.pallas_call(
        functools.partial(_router_body, tm=tm),
        grid=(T // tm,),
        in_specs=[rowblk(FOX_WIDTH), rowblk(HG_WIDTH), rowblk(POOL_WIDTH), rowblk(D),
                  const((D, D)), const((1, D)), const((D, 2 * LANES)), const((1, LANES)), const((tm, tm))],
        out_specs=[rowblk(D), rowblk(LANES), rowblk(LANES), const((1, LANES))],
        out_shape=[jax.ShapeDtypeStruct((T, D), F32), jax.ShapeDtypeStruct((T, LANES), I32),
                   jax.ShapeDtypeStruct((T, LANES), F32), jax.ShapeDtypeStruct((1, LANES), F32)],
        scratch_shapes=[pltpu.VMEM((1, LANES), F32)],
        compiler_params=_cparams(("arbitrary",)),
        name="outproj_router",
    )(o_fox, o_hg, o_pool, h, wo, g, wr, br, tri)


def _to_row_tiles(ref, x):
    ref[...] = pltpu.einshape("r(cl)->rcl", x, c=ref.shape[-2])


def _from_row_tiles(ref):
    return pltpu.einshape("rcl->r(cl)", ref[...])


def _row_copy(src_ref, src_row, dst_ref, dst_row, sem):
    return pltpu.make_async_copy(src_ref.at[pl.ds(src_row, 1)], dst_ref.at[pl.ds(dst_row, 1)], sem)


def _wait_rows(tile_ref, sem, times):
    for _ in range(times):
        pltpu.make_async_copy(tile_ref, tile_ref, sem).wait()


def _dispatch_body(pad0_ref, padn_ref, dst_ref, h_ref, g_ref, buf_ref, xn_ref, zero_ref, sem, zsem, *, tt, bm):
    i = pl.program_id(0)
    n = pl.num_programs(0)
    slot = i % 2
    stage = xn_ref.at[slot]

    @pl.when(i >= 2)
    def _():
        _wait_rows(stage, sem.at[slot], TOP_K)

    x = h_ref[...]
    ms = jnp.mean(x * x, axis=-1, keepdims=True)
    _to_row_tiles(stage, x * lax.rsqrt(ms + RMS_EPS) * g_ref[...])

    def issue(r, _):
        for k in range(TOP_K):
            _row_copy(stage, r, buf_ref, dst_ref[0, 0, r * TOP_K + k], sem.at[slot]).start(priority=k % 2)
        return 0

    lax.fori_loop(0, tt, issue, 0, unroll=DMA_UNROLL)

    @pl.when(i == 0)
    def _():
        zero_ref[...] = jnp.zeros_like(zero_ref)
        half = bm // 2
        n_exp = pad0_ref.shape[0] - 1

        def run(cp, pred, wait):
            @pl.when(pred)
            def _():
                cp.wait() if wait else cp.start()

        def pad_copies(e, wait):
            off, cnt = pad0_ref[e], padn_ref[e]
            for r in range(SUBLANES - 1):
                run(_row_copy(zero_ref, 0, buf_ref, off + r, zsem), r < (cnt & (SUBLANES - 1)), wait)
            off = off + (cnt & (SUBLANES - 1))
            sz = SUBLANES
            while sz <= half:
                take = (cnt & sz) != 0
                dst = buf_ref.at[pl.ds(pl.multiple_of(off, SUBLANES), sz)]
                run(pltpu.make_async_copy(zero_ref.at[pl.ds(0, sz)], dst, zsem), take, wait)
                off = off + jnp.where(take, sz, 0)
                sz *= 2
            return 0

        def tail_copy(c, wait):
            dst = buf_ref.at[pl.ds(pl.multiple_of(pad0_ref[n_exp] + c * half, SUBLANES), half)]
            cp = pltpu.make_async_copy(zero_ref, dst, zsem)
            cp.wait() if wait else cp.start()
            return 0

        n_tail = padn_ref[n_exp] // half
        for wait in (False, True):
            lax.fori_loop(0, n_exp, lambda e, c, w=wait: pad_copies(e, w), 0)
            lax.fori_loop(0, n_tail, lambda c, _, w=wait: tail_copy(c, w), 0)

    @pl.when(i == n - 1)
    def _():
        _wait_rows(stage, sem.at[slot], TOP_K)

    @pl.when((i == n - 1) & (n >= 2))
    def _():
        _wait_rows(xn_ref.at[1 - slot], sem.at[1 - slot], TOP_K)


def _dispatch(pad0, padn, dst3, hn, g, R, tt, bm):
    T, D = hn.shape
    grid_spec = pltpu.PrefetchScalarGridSpec(
        num_scalar_prefetch=2,
        grid=(T // tt,),
        in_specs=[
            pl.BlockSpec((1, 1, tt * TOP_K), lambda i, p0, pn: (i, 0, 0), memory_space=pltpu.SMEM),
            pl.BlockSpec((tt, D), lambda i, p0, pn: (i, 0)),
            pl.BlockSpec((1, D), lambda i, p0, pn: (0, 0)),
        ],
        out_specs=pl.BlockSpec(memory_space=pl.ANY),
        scratch_shapes=[pltpu.VMEM((2, tt, D // LANES, LANES), F32), pltpu.VMEM((bm // 2, D // LANES, LANES), F32),
                        pltpu.SemaphoreType.DMA((2,)), pltpu.SemaphoreType.DMA],
    )
    return pl.pallas_call(
        functools.partial(_dispatch_body, tt=tt, bm=bm),
        grid_spec=grid_spec,
        out_shape=jax.ShapeDtypeStruct((R, D // LANES, LANES), F32),
        compiler_params=_cparams(("arbitrary",)),
        name="moe_dispatch",
    )(pad0, padn, dst3, hn, g)


def _expert_body(te_ref, nu_ref, nxt_ref, slot_ref, x_ref, w1_hbm, w2_hbm, perm_ref, b1g_ref, b1l_ref, b2_ref, y_ref,
                 w1g_scr, w1l_scr, w2_scr, w1_raw, w2_raw, sem, *, first_expert):
    i = pl.program_id(0)
    live = i < nu_ref[0]
    e = te_ref[i]
    prev = te_ref[jnp.maximum(i - 1, 0)]

    def fetch(expert, slot):
        return (pltpu.make_async_copy(w1_hbm.at[first_expert + expert], w1_raw.at[slot], sem.at[slot, 0]),
                pltpu.make_async_copy(w2_hbm.at[first_expert + expert], w2_raw.at[slot], sem.at[slot, 1]))

    @pl.when(i == 0)
    def _():
        for cp in fetch(e, slot_ref[e]):
            cp.start()

    @pl.when(live & ((i == 0) | (e != prev)))
    def _():
        slot = slot_ref[e]
        for cp in fetch(e, slot):
            cp.wait()
        nxt = nxt_ref[e]

        @pl.when(nxt >= 0)
        def _():
            for cp in fetch(nxt, 1 - slot):
                cp.start()

        half = MXU_DIM // 2
        for grp in range(w1_raw.shape[-1] // MXU_DIM):
            w = w1_raw[slot, :, grp * MXU_DIM:(grp + 1) * MXU_DIM].astype(BF16)
            r = jnp.dot(w, perm_ref[...], preferred_element_type=F32).astype(BF16)
            w1g_scr[:, grp * half:(grp + 1) * half] = r[:, :half]
            w1l_scr[:, grp * half:(grp + 1) * half] = r[:, half:]
        w2_scr[...] = w2_raw[slot].astype(BF16)

    @pl.when(live)
    def _():
        x = _from_row_tiles(x_ref).astype(BF16)
        glu = jnp.dot(x, w1g_scr[...], preferred_element_type=F32) + b1g_ref[0]
        lin = jnp.dot(x, w1l_scr[...], preferred_element_type=F32) + b1l_ref[0]
        glu = jnp.minimum(glu, SWIGLU_LIMIT)
        lin = jnp.clip(lin, -SWIGLU_LIMIT, SWIGLU_LIMIT)
        act = glu / (1.0 + jnp.exp(-SWIGLU_ALPHA * glu)) * (lin + 1.0)
        _to_row_tiles(y_ref, jnp.dot(act.astype(BF16), w2_scr[...], preferred_element_type=F32) + b2_ref[0])

    @pl.when(jnp.logical_not(live))
    def _():
        y_ref[...] = jnp.zeros_like(y_ref)


def _experts(tile_e, n_used, next_expert, slot_of, buf, w1, w2, b1g, b1l, b2, bm, first_expert):
    R, dsub, _ = buf.shape
    D = dsub * LANES
    F2 = w1.shape[-1]
    F = F2 // 2
    half = MXU_DIM // 2
    pi = jnp.arange(MXU_DIM)[:, None]
    pj = jnp.arange(MXU_DIM)[None, :]
    perm = jnp.where(pj < half, pi == 2 * pj, pi == 2 * (pj - half) + 1).astype(BF16)
    rows = pl.BlockSpec((bm, dsub, LANES), lambda i, te, nu, nx, sl: (jnp.minimum(i, nu[0] - 1), 0, 0))
    wspec = lambda a, b: pl.BlockSpec((1, a, b), lambda i, te, nu, nx, sl: (te[i], 0, 0))
    grid_spec = pltpu.PrefetchScalarGridSpec(
        num_scalar_prefetch=4,
        grid=(R // bm,),
        in_specs=[rows, pl.BlockSpec(memory_space=pl.ANY), pl.BlockSpec(memory_space=pl.ANY),
                  pl.BlockSpec((MXU_DIM, MXU_DIM), lambda i, te, nu, nx, sl: (0, 0)),
                  wspec(1, F), wspec(1, F), wspec(1, D)],
        out_specs=pl.BlockSpec((bm, dsub, LANES), lambda i, te, nu, nx, sl: (i, 0, 0)),
        scratch_shapes=[pltpu.VMEM((D, F), BF16), pltpu.VMEM((D, F), BF16), pltpu.VMEM((F, D), BF16),
                        pltpu.VMEM((2, D, F2), F32), pltpu.VMEM((2, F, D), F32), pltpu.SemaphoreType.DMA((2, 2))],
    )
    return pl.pallas_call(
        functools.partial(_expert_body, first_expert=first_expert),
        grid_spec=grid_spec,
        out_shape=jax.ShapeDtypeStruct((R, dsub, LANES), F32),
        compiler_params=_cparams(("arbitrary",), vmem=EXPERT_VMEM_LIMIT),
        name="moe_experts",
    )(tile_e, n_used, next_expert, slot_of, buf, w1, w2, perm, b1g, b1l, b2)


def _combine_body(dst_ref, h_ref, rg_ref, gf_ref, y_ref, o_ref, rows_ref, sem, *, tt, final_norm):
    s = pl.program_id(0)
    n_tiles = pl.num_programs(0) - 1

    @pl.when(s < n_tiles)
    def _():
        slot = s % 2

        def issue(r, _):
            for k in range(TOP_K):
                _row_copy(y_ref, dst_ref[0, 0, r * TOP_K + k], rows_ref.at[slot, k], r,
                          sem.at[slot]).start(priority=k % 2)
            return 0

        lax.fori_loop(0, tt, issue, 0, unroll=DMA_UNROLL)

    @pl.when(s >= 1)
    def _():
        slot = (s - 1) % 2
        _wait_rows(rows_ref.at[slot, 0], sem.at[slot], TOP_K)
        out = h_ref[...]
        gates = rg_ref[...]
        for k in range(TOP_K):
            out = out + gates[:, k:k + 1] * _from_row_tiles(rows_ref.at[slot, k])
        if final_norm:
            ms = jnp.mean(out * out, axis=-1, keepdims=True)
            out = out * lax.rsqrt(ms + RMS_EPS) * gf_ref[...]
        o_ref[...] = out


def _combine(dst3, hn, rg, g_final, ybuf, tt, final_norm):
    T, D = hn.shape
    n_tiles = T // tt
    prev = lambda s: (jnp.maximum(s - 1, 0), 0)
    return pl.pallas_call(
        functools.partial(_combine_body, tt=tt, final_norm=final_norm),
        grid=(n_tiles + 1,),
        in_specs=[
            pl.BlockSpec((1, 1, tt * TOP_K), lambda s: (jnp.minimum(s, n_tiles - 1), 0, 0), memory_space=pltpu.SMEM),
            pl.BlockSpec((tt, D), prev),
            pl.BlockSpec((tt, LANES), prev),
            pl.BlockSpec((1, D), lambda s: (0, 0)),
            pl.BlockSpec(memory_space=pl.ANY),
        ],
        out_specs=pl.BlockSpec((tt, D), prev),
        out_shape=jax.ShapeDtypeStruct((T, D), F32),
        scratch_shapes=[pltpu.VMEM((2, TOP_K, tt, D // LANES, LANES), F32), pltpu.SemaphoreType.DMA((2,))],
        compiler_params=_cparams(("arbitrary",)),
        name="moe_combine",
    )(dst3, hn, rg, g_final, ybuf)


def _tiles(B, S):
    T = B * S
    ts = min(512, S)
    tm = min(512, T)
    tt = min(256, T)
    bm = min(512, T)
    return ts, tm, tt, bm


def _block_tri(n, blk, strict=False):
    r = jnp.arange(n)[:, None]
    c = jnp.arange(n)[None, :]
    keep = (c < r) if strict else (c <= r)
    return (keep & (r // blk == c // blk)).astype(BF16)


def kernel(x, norm_mix, w_in, b_fox_f, fox_gain, hgrn_lb, hgrn_gain, pool_w, pool_scale, w_out,
           norm_moe, w_router, b_router, w1, b1, w2, b2, norm_final):
    B, S, D = x.shape
    L = w_in.shape[0]
    T = B * S
    E = w1.shape[1]
    ts, tm, tt, bm = _tiles(B, S)

    offs = [0]
    for wdt in (FOX_WIDTH, FOX_WIDTH, FOX_WIDTH, FOX_HEADS, HG_WIDTH, HG_WIDTH, HG_WIDTH, HG_WIDTH, POOL_WIDTH):
        offs.append(offs[-1] + wdt)
    col = lambda i: w_in[:, :, offs[i]:offs[i + 1]]
    scale = FOX_HEAD_DIM ** -0.5 * LOG2E
    wa = jnp.concatenate([col(0) * scale, col(1), col(2), col(4), col(6), col(7), col(8)], axis=-1).astype(BF16)
    wb = jnp.concatenate([col(5), col(3), jnp.zeros((L, D, LANES - FOX_HEADS), F32)], axis=-1).astype(BF16)
    bf_pad = jnp.pad(b_fox_f, ((0, 0), (0, LANES - FOX_HEADS)))
    p = jax.nn.softmax(hgrn_lb.astype(F32), axis=0)
    cs = jnp.cumsum(p, axis=0)
    lbs = cs - cs[0:1]
    eye = jnp.eye(len(POOL_WINDOWS), dtype=F32)
    pool_bd = jnp.einsum("lgcd,gh->lgchd", pool_w, eye).reshape(L, POOL_WIDTH, POOL_WIDTH).astype(BF16)
    wr_pad = jnp.pad(w_router, ((0, 0), (0, 0), (0, LANES - E)))
    wr_hi = wr_pad.astype(BF16)
    wr_lo = (wr_pad - wr_hi.astype(F32)).astype(BF16)
    wr = jnp.concatenate([wr_hi, wr_lo], axis=-1)
    br_pad = jnp.pad(b_router, ((0, 0), (0, LANES - E)), constant_values=NEG_BIG)
    w1_all = w1.reshape((L * E,) + w1.shape[2:])
    w2_all = w2.reshape((L * E,) + w2.shape[2:])
    b1g = b1[..., 0::2][:, :, None, :]
    b1l = b1[..., 1::2][:, :, None, :]
    b2r = b2[:, :, None, :]

    tri_seq = _block_tri(ts, ts)
    tri64 = _block_tri(HG_CHUNK, HG_CHUNK)
    li = jnp.arange(LANES)
    head_mean = ((li[:, None] // HG_DK == li[None, :] // HG_DK) / HG_DK).astype(BF16)
    tri_tok = _block_tri(tm, tm, strict=True)

    M = T * TOP_K
    R = -(-M // bm) * bm + E * bm
    NB = R // bm

    h = x.reshape(T, D)
    for l in range(L):
        a2, b2d = _inproj(h, norm_mix[l][None], wa[l], wb[l], tm)
        a3 = a2.reshape(B, S, A_COLS)
        b3 = b2d.reshape(B, S, B_COLS)
        ka, stats, o_hg, o_pool = _seqmix(a3, b3, bf_pad[l][None], tri_seq, lbs[l][None], hgrn_gain[l][None],
                                          tri64, head_mean, pool_bd[l], pool_scale[l][None], ts)
        heads = _fox_pairing(b_fox_f[l])
        by_head = lambda w: w.reshape((FOX_HEADS, FOX_HEAD_DIM) + w.shape[1:])[heads].reshape(w.shape)
        o_fox = _fox(_fox_first_tiles(stats), heads, a3, ka, by_head(fox_gain[l])[None], ts, min(32, ts))
        wo_l = jnp.concatenate([by_head(w_out[l, :FOX_WIDTH]), w_out[l, FOX_WIDTH:]], axis=0).astype(BF16)
        hn, ri, rg, cnt = _router(o_fox.reshape(T, FOX_WIDTH), o_hg.reshape(T, HG_WIDTH),
                                  o_pool.reshape(T, POOL_WIDTH), h, wo_l, norm_moe[l][None],
                                  wr[l], br_pad[l][None], tri_tok, tm)
        counts = cnt[0, :E].astype(I32)
        padded = (counts + bm - 1) // bm * bm
        pend = jnp.cumsum(padded)
        pstart = pend - padded
        first_row = jnp.sum(jnp.where(ri[:, 0:TOP_K, None] == jnp.arange(E, dtype=I32), pstart.astype(I32), 0), axis=-1)
        dst = first_row + ri[:, TOP_K:2 * TOP_K]
        n_used = (pend[-1] // bm).astype(I32)
        tile_ids = jnp.minimum(jnp.arange(NB, dtype=I32), n_used - 1)
        tile_e = jnp.minimum(jnp.sum(pend[None, :] <= (tile_ids * bm)[:, None], axis=1), E - 1).astype(I32)
        dst3 = dst.reshape(T // tt, 1, tt * TOP_K)
        pad0 = jnp.concatenate([pstart + counts, pend[-1:]]).astype(I32)
        padn = jnp.concatenate([padded - counts, R - pend[-1:]]).astype(I32)
        buf = _dispatch(pad0, padn, dst3, hn, norm_moe[l][None], R, tt, bm)
        has_rows = counts > 0
        ids = jnp.arange(E, dtype=I32)
        later = has_rows[None, :] & (ids[None, :] > ids[:, None])
        next_expert = jnp.where(later.any(axis=1), jnp.argmax(later, axis=1), -1).astype(I32)
        slot_of = ((jnp.cumsum(has_rows) - has_rows) % 2).astype(I32)
        ybuf = _experts(tile_e, n_used.reshape(1), next_expert, slot_of, buf, w1_all, w2_all,
                        b1g[l], b1l[l], b2r[l], bm, l * E)
        h = _combine(dst3, hn, rg, norm_final[None], ybuf, tt, final_norm=(l == L - 1))
    return h.reshape(B, S, D)
```
